```python
import jax, jax.numpy as jnp
from jax import lax
import numpy as np

D_MODEL = 1024
BATCH = 4
SEQ = 4096
DEPTH = 4

CHUNK = 64
Q_BLOCK = 128

D_MIX = D_MODEL
MLA_HEADS = 8
QK_NOPE = 64
QK_ROPE = 32
V_DIM = 64
Q_LORA = 256
KV_LORA = 128
ROPE_THETA = 10000.0
SGU_GROUPS = 4
SGU_GROUP_DIM = 64
SGU_WIDTH = SGU_GROUPS * SGU_GROUP_DIM
SGU_BLOCK = 128
LRU_HEADS = 4
LRU_HEAD_DIM = 64
LRU_WIDTH = LRU_HEADS * LRU_HEAD_DIM
CONV_W = 4
LRU_C = 8.0
D_IN = Q_LORA + KV_LORA + QK_ROPE + 2 * SGU_WIDTH + 2 * LRU_WIDTH
IN_SPLITS = (Q_LORA,
             Q_LORA + KV_LORA,
             Q_LORA + KV_LORA + QK_ROPE,
             Q_LORA + KV_LORA + QK_ROPE + SGU_WIDTH,
             Q_LORA + KV_LORA + QK_ROPE + 2 * SGU_WIDTH,
             Q_LORA + KV_LORA + QK_ROPE + 2 * SGU_WIDTH + LRU_WIDTH)
D_FF = 2816
N_EXPERTS = 8
TOP_K = 2
D_FF_EXPERT = 1408
N_DENSE = (DEPTH + 1) // 2
N_MOE = DEPTH // 2
NEG_INF = -1e30

kernel_name = "hybrid_mla_sgu_rglru_moe_trunk"


def rms_norm(x, g, eps=1e-6):
    xf = x.astype(jnp.float32)
    y = xf * lax.rsqrt(jnp.mean(xf * xf, axis=-1, keepdims=True) + eps)
    return (y * g.astype(jnp.float32)).astype(x.dtype)


def layer_norm(x, g, b, eps=1e-5):
    xf = x.astype(jnp.float32)
    mu = jnp.mean(xf, axis=-1, keepdims=True)
    var = jnp.mean(jnp.square(xf - mu), axis=-1, keepdims=True)
    y = (xf - mu) * lax.rsqrt(var + eps) * g.astype(jnp.float32) + b.astype(jnp.float32)
    return y.astype(x.dtype)


def rope_tables(seq):
    pos = jnp.arange(seq, dtype=jnp.float32)
    inv = ROPE_THETA ** (-jnp.arange(0, QK_ROPE, 2, dtype=jnp.float32) / QK_ROPE)
    ang = pos[:, None] * inv[None, :]
    return jnp.cos(ang), jnp.sin(ang)


def apply_rope(x, cos, sin):
    half = x.shape[-1] // 2
    x1 = x[..., :half].astype(jnp.float32)
    x2 = x[..., half:].astype(jnp.float32)
    return jnp.concatenate([x1 * cos - x2 * sin, x2 * cos + x1 * sin], axis=-1).astype(x.dtype)


def chunk_causal_attention(q, k, v):
    B, S, H, Dq = q.shape
    n_qb = S // Q_BLOCK
    scale = Dq ** -0.5
    qb = q.reshape(B, n_qb, Q_BLOCK, H, Dq).transpose(1, 0, 3, 2, 4)
    kt = k.transpose(0, 2, 1, 3)
    vt = v.transpose(0, 2, 1, 3)
    key_chunk = jnp.arange(S) // CHUNK

    def one_block(args):
        q_blk, blk = args
        s = jnp.einsum('bhqd,bhkd->bhqk', q_blk, kt, preferred_element_type=jnp.float32) * scale
        q_chunk = (blk * Q_BLOCK + jnp.arange(Q_BLOCK)) // CHUNK
        mask = key_chunk[None, :] <= q_chunk[:, None]
        p = jax.nn.softmax(jnp.where(mask, s, NEG_INF), axis=-1).astype(vt.dtype)
        return jnp.einsum('bhqk,bhkd->bhqd', p, vt)

    o = lax.map(one_block, (qb, jnp.arange(n_qb)))
    return o.transpose(1, 0, 3, 2, 4).reshape(B, S, H, vt.shape[-1])


def mla_mixer(z_q, z_kv, z_kr, q_norm_g, w_uq, kv_norm_g, w_ukv, cos, sin):
    B, S, _ = z_q.shape
    q = (rms_norm(z_q, q_norm_g) @ w_uq).reshape(B, S, MLA_HEADS, QK_NOPE + QK_ROPE)
    q_rope = apply_rope(q[..., QK_NOPE:], cos[:, None, :], sin[:, None, :])
    q = jnp.concatenate([q[..., :QK_NOPE], q_rope], axis=-1)
    kv = (rms_norm(z_kv, kv_norm_g) @ w_ukv).reshape(B, S, MLA_HEADS, QK_NOPE + V_DIM)
    k_rope = apply_rope(z_kr, cos, sin)
    k = jnp.concatenate([kv[..., :QK_NOPE],
                         jnp.broadcast_to(k_rope[:, :, None, :], (B, S, MLA_HEADS, QK_ROPE))], axis=-1)
    v = kv[..., QK_NOPE:]
    o = chunk_causal_attention(q, k, v)
    return o.reshape(B, S, MLA_HEADS * V_DIM)


def sgu_mixer(z_u, z_v, ln_g, ln_b, w_s, b_s):
    B, S, _ = z_u.shape
    u = jax.nn.gelu(z_u)
    v = layer_norm(jax.nn.gelu(z_v), ln_g, ln_b)
    vb = v.reshape(B, S // SGU_BLOCK, SGU_BLOCK, SGU_GROUPS, SGU_GROUP_DIM)
    w = w_s * jnp.tril(jnp.ones((SGU_BLOCK, SGU_BLOCK), dtype=w_s.dtype))[None]
    mixed = jnp.einsum('gts,bnsgc->bntgc', w, vb) + b_s.T[None, None, :, :, None]
    return u * mixed.reshape(B, S, SGU_WIDTH)


def rglru_mixer(z_x, z_y, conv_w, conv_b, wa, ba, wx, bx, lam):
    B, S, W = z_x.shape
    xp = jnp.pad(z_x, ((0, 0), (CONV_W - 1, 0), (0, 0)))
    xc = conv_b + xp[:, 0:S, :] * conv_w[0]
    for tap in range(1, CONV_W):
        xc = xc + xp[:, tap:tap + S, :] * conv_w[tap]
    xh = xc.reshape(B, S, LRU_HEADS, LRU_HEAD_DIM)
    r = jax.nn.sigmoid(jnp.einsum('bshi,hij->bshj', xh, wa).reshape(B, S, W) + ba)
    i = jax.nn.sigmoid(jnp.einsum('bshi,hij->bshj', xh, wx).reshape(B, S, W) + bx)
    log_a = -LRU_C * r.astype(jnp.float32) * jax.nn.softplus(-lam.astype(jnp.float32))
    a = jnp.exp(log_a)
    b = jnp.sqrt(-jnp.expm1(2.0 * log_a)) * (i * xc).astype(jnp.float32)

    def combine(left, right):
        a_l, b_l = left
        a_r, b_r = right
        return a_l * a_r, a_r * b_l + b_r

    _, h = lax.associative_scan(combine, (a, b), axis=1)
    return h.astype(z_x.dtype) * jax.nn.gelu(z_y)


def swiglu(h, wg, wu, wd):
    return (jax.nn.silu(h @ wg) * (h @ wu)) @ wd


def moe_swiglu(h, router, wg, wu, wd):
    B, S, D = h.shape
    t = h.reshape(B * S, D)
    logits = (t @ router).astype(jnp.float32)
    top_v, top_i = lax.top_k(logits, TOP_K)
    gates = jax.nn.softmax(top_v, axis=-1)
    comb = jnp.sum(jax.nn.one_hot(top_i, N_EXPERTS, dtype=jnp.float32) * gates[..., None], axis=1)
    comb = comb.astype(t.dtype)
    out = jnp.zeros_like(t)
    for e in range(N_EXPERTS):
        out = out + comb[:, e:e + 1] * swiglu(t, wg[e], wu[e], wd[e])
    return out.reshape(B, S, D)


def setup_inputs(seed: int = 0) -> dict:
    key = jax.random.key(seed)
    ks = iter(jax.random.split(key, 48))
    f32 = jnp.float32

    def nrm(shape, fan_in, mult=1.0):
        return jax.random.normal(next(ks), shape, f32) * (mult * fan_in ** -0.5)

    def gain(shape):
        return 1.0 + 0.02 * jax.random.normal(next(ks), shape, f32)

    def small(shape, s=0.02):
        return s * jax.random.normal(next(ks), shape, f32)

    L = DEPTH
    u = jax.random.uniform(next(ks), (L, LRU_WIDTH), f32, minval=0.9, maxval=0.999)
    p = u ** (1.0 / LRU_C)
    lam = jnp.log(p) - jnp.log1p(-p)
    return {
        "x": jax.random.normal(next(ks), (BATCH, SEQ, D_MODEL), f32),
        "c": jax.random.normal(next(ks), (BATCH, D_MODEL), f32),
        "w_mod": nrm((L, D_MODEL, 6 * D_MODEL), D_MODEL, 0.5),
        "b_mod": small((L, 6 * D_MODEL)),
        "pre_mix_g": gain((L, D_MODEL)),
        "post_mix_g": gain((L, D_MODEL)),
        "w_in": nrm((L, D_MODEL, D_IN), D_MODEL),
        "q_norm_g": gain((L, Q_LORA)),
        "w_uq": nrm((L, Q_LORA, MLA_HEADS * (QK_NOPE + QK_ROPE)), Q_LORA),
        "kv_norm_g": gain((L, KV_LORA)),
        "w_ukv": nrm((L, KV_LORA, MLA_HEADS * (QK_NOPE + V_DIM)), KV_LORA),
        "sgu_ln_g": gain((L, SGU_WIDTH)),
        "sgu_ln_b": small((L, SGU_WIDTH)),
        "sgu_w": nrm((L, SGU_GROUPS, SGU_BLOCK, SGU_BLOCK), SGU_BLOCK),
        "sgu_b": gain((L, SGU_GROUPS, SGU_BLOCK)),
        "conv_w": nrm((L, CONV_W, LRU_WIDTH), CONV_W),
        "conv_b": small((L, LRU_WIDTH)),
        "lru_wa": nrm((L, LRU_HEADS, LRU_HEAD_DIM, LRU_HEAD_DIM), LRU_HEAD_DIM),
        "lru_ba": small((L, LRU_WIDTH)),
        "lru_wx": nrm((L, LRU_HEADS, LRU_HEAD_DIM, LRU_HEAD_DIM), LRU_HEAD_DIM),
        "lru_bx": small((L, LRU_WIDTH)),
        "lru_lambda": lam,
        "w_o": nrm((L, D_MIX, D_MODEL), D_MIX),
        "pre_ffn_g": gain((L, D_MODEL)),
        "post_ffn_g": gain((L, D_MODEL)),
        "ffn_w_gate": nrm((N_DENSE, D_MODEL, D_FF), D_MODEL),
        "ffn_w_up": nrm((N_DENSE, D_MODEL, D_FF), D_MODEL),
        "ffn_w_down": nrm((N_DENSE, D_FF, D_MODEL), D_FF),
        "moe_router": nrm((N_MOE, D_MODEL, N_EXPERTS), D_MODEL),
        "moe_w_gate": nrm((N_MOE, N_EXPERTS, D_MODEL, D_FF_EXPERT), D_MODEL),
        "moe_w_up": nrm((N_MOE, N_EXPERTS, D_MODEL, D_FF_EXPERT), D_MODEL),
        "moe_w_down": nrm((N_MOE, N_EXPERTS, D_FF_EXPERT, D_MODEL), D_FF_EXPERT),
    }


def reference(x, c, w_mod, b_mod, pre_mix_g, post_mix_g, w_in, q_norm_g, w_uq, kv_norm_g, w_ukv,
              sgu_ln_g, sgu_ln_b, sgu_w, sgu_b, conv_w, conv_b, lru_wa, lru_ba, lru_wx, lru_bx,
              lru_lambda, w_o, pre_ffn_g, post_ffn_g, ffn_w_gate, ffn_w_up, ffn_w_down,
              moe_router, moe_w_gate, moe_w_up, moe_w_down):
    S = x.shape[1]
    cos, sin = rope_tables(S)
    cond = jax.nn.silu(c)
    for l in range(DEPTH):
        mod = cond @ w_mod[l] + b_mod[l]
        sh_m, sc_m, g_m, sh_f, sc_f, g_f = [m[:, None, :] for m in jnp.split(mod, 6, axis=-1)]

        h = rms_norm(x, pre_mix_g[l]) * (1.0 + sc_m) + sh_m
        z = h @ w_in[l]
        z_q, z_kv, z_kr, z_u, z_v, z_x, z_y = jnp.split(z, IN_SPLITS, axis=-1)
        y_mla = mla_mixer(z_q, z_kv, z_kr, q_norm_g[l], w_uq[l], kv_norm_g[l], w_ukv[l], cos, sin)
        y_sgu = sgu_mixer(z_u, z_v, sgu_ln_g[l], sgu_ln_b[l], sgu_w[l], sgu_b[l])
        y_lru = rglru_mixer(z_x, z_y, conv_w[l], conv_b[l], lru_wa[l], lru_ba[l],
                            lru_wx[l], lru_bx[l], lru_lambda[l])
        y = jnp.concatenate([y_mla, y_sgu, y_lru], axis=-1) @ w_o[l]
        x = x + g_m * rms_norm(y, post_mix_g[l])

        h = rms_norm(x, pre_ffn_g[l]) * (1.0 + sc_f) + sh_f
        if l % 2 == 0:
            f = swiglu(h, ffn_w_gate[l // 2], ffn_w_up[l // 2], ffn_w_down[l // 2])
        else:
            f = moe_swiglu(h, moe_router[l // 2], moe_w_gate[l // 2], moe_w_up[l // 2], moe_w_down[l // 2])
        x = x + g_f * rms_norm(f, post_ffn_g[l])
    return x
```

```python
import functools

import jax
import jax.numpy as jnp
from jax import lax
from jax.experimental import pallas as pl
from jax.experimental.pallas import tpu as pltpu

F32 = jnp.float32
BF16 = jnp.bfloat16
HIGHEST = lax.Precision.HIGHEST

MLA_HEADS = 8
QK_NOPE = 64
QK_ROPE = 32
V_DIM = 64
Q_LORA = 256
KV_LORA = 128
ROPE_THETA = 10000.0
CHUNK = 64
SGU_GROUPS = 4
SGU_GROUP_DIM = 64
SGU_WIDTH = SGU_GROUPS * SGU_GROUP_DIM
SGU_BLOCK = 128
LRU_HEADS = 4
LRU_HEAD_DIM = 64
LRU_WIDTH = LRU_HEADS * LRU_HEAD_DIM
CONV_W = 4
LRU_C = 8.0
N_EXPERTS = 8
NEG_INF = -1e30
RMS_EPS = 1e-6
LN_EPS = 1e-5

LANE = 128
SUBLANE = 8
HEAD_PAD = LANE

C_Q = 0
C_KV = C_Q + Q_LORA
C_U = C_KV + KV_LORA
C_V = C_U + SGU_WIDTH
C_X = C_V + SGU_WIDTH
C_Y = C_X + LRU_WIDTH
C_KR = C_Y + LRU_WIDTH
D_IN_PAD = C_KR + LANE

VMEM_LIMIT = 56 * 1024 * 1024

TM_MIX = 512
TQ = 512
TK = 512
TM_OUT = 512
TM_FFN = 512


def _rms(x, g):
    return x * lax.rsqrt(jnp.mean(x * x, axis=-1, keepdims=True) + RMS_EPS) * g


def _gelu(x):
    return 0.5 * x * (1.0 + jnp.tanh(0.7978845608028654 * (x + 0.044715 * (x * x * x))))


def _sigmoid(x):
    return 1.0 / (1.0 + jnp.exp(-x))


def _silu(x):
    return x * _sigmoid(x)


def _bdot(a, b):
    return jnp.dot(a, b, preferred_element_type=F32)


def _mod_kernel(c_ref, w_ref, b_ref, o_ref):
    c = c_ref[...]
    o_ref[0] = jnp.dot(_silu(c), w_ref[0], preferred_element_type=F32, precision=HIGHEST) + b_ref[0]


def _modulation(c, w_mod, b_mod):
    L, D, D6 = w_mod.shape
    B = c.shape[0]
    return pl.pallas_call(
        _mod_kernel,
        grid=(L, D6 // D),
        in_specs=[
            pl.BlockSpec((B, D), lambda l, j: (0, 0)),
            pl.BlockSpec((1, D, D), lambda l, j: (l, 0, j)),
            pl.BlockSpec((1, 1, D), lambda l, j: (l, 0, j)),
        ],
        out_specs=pl.BlockSpec((1, B, D), lambda l, j: (l, 0, j)),
        out_shape=jax.ShapeDtypeStruct((L, B, D6), F32),
        compiler_params=pltpu.CompilerParams(vmem_limit_bytes=VMEM_LIMIT),
        name="modulation",
    )(c, w_mod, b_mod.reshape(L, 1, D6))


def _shift_rows(cur, prev8, k):
    rolled = pltpu.roll(cur, k, 0)
    fix = pltpu.roll(prev8, k, 0)
    row = lax.broadcasted_iota(jnp.int32, fix.shape, 0)
    top = jnp.where(row < k, fix, rolled[:SUBLANE])
    return jnp.concatenate([top, rolled[SUBLANE:]], axis=0)


def _mix_in_kernel(x_ref, sh_ref, sc_ref, preg_ref, win_ref, qg_ref, wq_ref, kvg_ref, wk_ref, wv_ref,
                   cosq_ref, sinq_ref, tk_ref, lng_ref, lnb_ref, ws_ref, bs_ref,
                   cw_ref, cb_ref, wa_ref, ba_ref, wx_ref, bx_ref, lam_ref,
                   q_ref, k_ref, v_ref, y_ref, prev_ref, hc_ref):
    tm = x_ref.shape[1]

    @pl.when(pl.program_id(1) == 0)
    def _():
        prev_ref[...] = jnp.zeros_like(prev_ref)
        hc_ref[...] = jnp.zeros_like(hc_ref)

    x = x_ref[0]
    h = _rms(x, preg_ref[...]) * (1.0 + sc_ref[0]) + sh_ref[0]
    z = _bdot(h.astype(BF16), win_ref[...])

    qn = _rms(z[:, C_Q:C_Q + Q_LORA], qg_ref[...]).astype(BF16)
    qq = _bdot(qn, wq_ref[...])
    cosq = cosq_ref[...]
    sinq = sinq_ref[...]
    qw = MLA_HEADS * HEAD_PAD
    for hd in range(MLA_HEADS):
        lo = hd * HEAD_PAD
        qh = qq[:, lo:lo + HEAD_PAD] * cosq + qq[:, qw + lo:qw + lo + HEAD_PAD] * sinq
        q_ref[0, :, lo:lo + HEAD_PAD] = qh.astype(BF16)
    kvn = _rms(z[:, C_KV:C_KV + KV_LORA], kvg_ref[...]).astype(BF16)
    kr = (z[:, C_KR:C_KR + LANE] * tk_ref[...]).astype(BF16)
    k_in = jnp.concatenate([kvn, kr], axis=-1)
    k_ref[0] = _bdot(k_in, wk_ref[...]).astype(BF16)
    v_ref[0] = _bdot(kvn, wv_ref[...]).astype(BF16)

    u = _gelu(z[:, C_U:C_U + SGU_WIDTH])
    gv = _gelu(z[:, C_V:C_V + SGU_WIDTH])
    mu = jnp.mean(gv, axis=-1, keepdims=True)
    var = jnp.mean(jnp.square(gv - mu), axis=-1, keepdims=True)
    vn = ((gv - mu) * lax.rsqrt(var + LN_EPS) * lng_ref[...] + lnb_ref[...]).astype(BF16)
    ws = ws_ref[...]
    r_t = lax.broadcasted_iota(jnp.int32, ws.shape, 0) % SGU_BLOCK
    c_s = lax.broadcasted_iota(jnp.int32, ws.shape, 1)
    ws = jnp.where(c_s <= r_t, ws, 0.0).astype(BF16)
    grp = lax.broadcasted_iota(jnp.int32, (SGU_BLOCK, SGU_WIDTH), 1) // SGU_GROUP_DIM
    for blk in range(tm // SGU_BLOCK):
        r0 = blk * SGU_BLOCK
        res = _bdot(ws, vn[r0:r0 + SGU_BLOCK])
        mixed = bs_ref[...]
        for g in range(SGU_GROUPS):
            mixed = mixed + jnp.where(grp == g, res[g * SGU_BLOCK:(g + 1) * SGU_BLOCK], 0.0)
        y_ref[0, r0:r0 + SGU_BLOCK, 0:SGU_WIDTH] = (u[r0:r0 + SGU_BLOCK] * mixed).astype(BF16)

    zx = z[:, C_X:C_X + LRU_WIDTH]
    prev8 = prev_ref[...]
    cw = cw_ref[...]
    xc = cb_ref[...] + zx * cw[CONV_W - 1:CONV_W]
    for k in range(1, CONV_W):
        xc = xc + _shift_rows(zx, prev8, k) * cw[CONV_W - 1 - k:CONV_W - k]
    prev_ref[...] = zx[tm - SUBLANE:]
    xcb = xc.astype(BF16)
    r = _sigmoid(_bdot(xcb, wa_ref[...]) + ba_ref[...])
    ig = _sigmoid(_bdot(xcb, wx_ref[...]) + bx_ref[...])
    lam = lam_ref[...]
    softplus_neg_lam = jnp.maximum(-lam, 0.0) + jnp.log(1.0 + jnp.exp(-jnp.abs(lam)))
    log_a = -LRU_C * r * softplus_neg_lam
    a = jnp.exp(log_a)
    t = jnp.tanh(log_a)
    b = jnp.sqrt(-2.0 * t / (1.0 - t)) * (ig * xc)
    row = lax.broadcasted_iota(jnp.int32, a.shape, 0)
    k = 1
    while k < tm:
        valid = row >= k
        a_sh = pltpu.roll(a, k, 0)
        b_sh = pltpu.roll(b, k, 0)
        b = jnp.where(valid, b + a * b_sh, b)
        a = jnp.where(valid, a * a_sh, a)
        k *= 2
    hs = b + a * hc_ref[0:1]
    hc_ref[0:1] = hs[tm - 1:tm]
    y_ref[0, :, SGU_WIDTH:SGU_WIDTH + LRU_WIDTH] = (hs * _gelu(z[:, C_Y:C_Y + LRU_WIDTH])).astype(BF16)


def _mix_in(x, sh, sc, pre_g, p, tabs):
    B, S, D = x.shape
    tm = TM_MIX
    full = lambda a: pl.BlockSpec(a.shape, lambda b, s: (0,) * a.ndim)
    per_b = pl.BlockSpec((1, 1, D), lambda b, s: (b, 0, 0))
    tab = pl.BlockSpec((tm, LANE), lambda b, s: (s, 0))
    consts = [p["w_in"], p["qg"], p["w_q"], p["kvg"], p["w_k"], p["w_v"]]
    consts2 = [p["ln_g"], p["ln_b"], p["w_s"], p["b_s"], p["conv_w"], p["conv_b"], p["wa"], p["ba"],
               p["wx"], p["bx"], p["lam"]]
    qw = MLA_HEADS * HEAD_PAD
    return pl.pallas_call(
        _mix_in_kernel,
        grid=(B, S // tm),
        in_specs=[pl.BlockSpec((1, tm, D), lambda b, s: (b, s, 0)), per_b, per_b, full(pre_g)]
        + [full(a) for a in consts] + [tab, tab, tab] + [full(a) for a in consts2],
        out_specs=[
            pl.BlockSpec((1, tm, qw), lambda b, s: (b, s, 0)),
            pl.BlockSpec((1, tm, qw), lambda b, s: (b, s, 0)),
            pl.BlockSpec((1, tm, MLA_HEADS * V_DIM), lambda b, s: (b, s, 0)),
            pl.BlockSpec((1, tm, SGU_WIDTH + LRU_WIDTH), lambda b, s: (b, s, 0)),
        ],
        out_shape=[
            jax.ShapeDtypeStruct((B, S, qw), BF16),
            jax.ShapeDtypeStruct((B, S, qw), BF16),
            jax.ShapeDtypeStruct((B, S, MLA_HEADS * V_DIM), BF16),
            jax.ShapeDtypeStruct((B, S, SGU_WIDTH + LRU_WIDTH), BF16),
        ],
        scratch_shapes=[pltpu.VMEM((SUBLANE, LRU_WIDTH), F32), pltpu.VMEM((SUBLANE, LRU_WIDTH), F32)],
        compiler_params=pltpu.CompilerParams(
            dimension_semantics=("arbitrary", "arbitrary"), vmem_limit_bytes=VMEM_LIMIT),
        name="mix_in",
    )(x, sh, sc, pre_g, *consts, tabs["cosq"], tabs["sinq"], tabs["tk"], *consts2)


def _attn_kernel(q_ref, k_ref, v_ref, o_ref):
    tq = q_ref.shape[1]
    i = pl.program_id(2)
    heads = q_ref.shape[2] // HEAD_PAD
    outs = []
    for hd in range(heads):
        q = q_ref[0, :, hd * HEAD_PAD:(hd + 1) * HEAD_PAD]

        def scores(j):
            kj = k_ref[0, pl.ds(pl.multiple_of(j * TK, TK), TK), hd * HEAD_PAD:(hd + 1) * HEAD_PAD]
            return lax.dot_general(q, kj, (((1,), (1,)), ((), ())), preferred_element_type=F32)

        def update(j, s, carry):
            m, l, acc = carry
            m_new = jnp.maximum(m, jnp.max(s, axis=-1, keepdims=True))
            alpha = jnp.exp(m - m_new)
            p = jnp.exp(s - m_new)
            vj = v_ref[0, pl.ds(pl.multiple_of(j * TK, TK), TK), hd * V_DIM:(hd + 1) * V_DIM]
            acc = alpha * acc + _bdot(p.astype(BF16), vj)
            return m_new, alpha * l + jnp.sum(p, axis=-1, keepdims=True), acc

        def body(j, carry):
            return update(j, scores(j), carry)

        init = (jnp.full((tq, 1), NEG_INF, F32), jnp.zeros((tq, 1), F32), jnp.zeros((tq, V_DIM), F32))
        carry = lax.fori_loop(0, i, body, init)
        qc = lax.broadcasted_iota(jnp.int32, (tq, TK), 0) // CHUNK
        kc = lax.broadcasted_iota(jnp.int32, (tq, TK), 1) // CHUNK
        s = jnp.where(kc <= qc, scores(i), NEG_INF)
        m, l, acc = update(i, s, carry)
        outs.append(acc / l)
    o_ref[0] = jnp.concatenate(outs, axis=-1).astype(BF16)


def _attention(q, k, v):
    B, S, _ = q.shape
    hp = 2
    return pl.pallas_call(
        _attn_kernel,
        grid=(B, MLA_HEADS // hp, S // TQ),
        in_specs=[
            pl.BlockSpec((1, TQ, hp * HEAD_PAD), lambda b, h, i: (b, i, h)),
            pl.BlockSpec((1, S, hp * HEAD_PAD), lambda b, h, i: (b, 0, h)),
            pl.BlockSpec((1, S, hp * V_DIM), lambda b, h, i: (b, 0, h)),
        ],
        out_specs=pl.BlockSpec((1, TQ, hp * V_DIM), lambda b, h, i: (b, i, h)),
        out_shape=jax.ShapeDtypeStruct((B, S, MLA_HEADS * V_DIM), BF16),
        compiler_params=pltpu.CompilerParams(
            dimension_semantics=("arbitrary", "arbitrary", "arbitrary"), vmem_limit_bytes=VMEM_LIMIT),
        name="attention",
    )(q, k, v)


def _mix_out_kernel(o_ref, y_ref, x_ref, wo_ref, postg_ref, gm_ref, preg_ref, scf_ref, shf_ref,
                    xo_ref, h_ref):
    no = o_ref.shape[1]
    y = _bdot(o_ref[...], wo_ref[0:no]) + _bdot(y_ref[...], wo_ref[no:])
    x = x_ref[...] + gm_ref[0] * _rms(y, postg_ref[...])
    xo_ref[...] = x
    h_ref[...] = (_rms(x, preg_ref[...]) * (1.0 + scf_ref[0]) + shf_ref[0]).astype(BF16)


def _mix_out_moe_kernel(o_ref, y_ref, x_ref, wo_ref, postg_ref, gm_ref, preg_ref, scf_ref, shf_ref,
                        router_ref, xo_ref, h_ref, comb_ref):
    no = o_ref.shape[1]
    y = _bdot(o_ref[...], wo_ref[0:no]) + _bdot(y_ref[...], wo_ref[no:])
    x = x_ref[...] + gm_ref[0] * _rms(y, postg_ref[...])
    xo_ref[...] = x
    h = _rms(x, preg_ref[...]) * (1.0 + scf_ref[0]) + shf_ref[0]
    h_ref[...] = h.astype(BF16)
    logits = jnp.dot(h, router_ref[...], preferred_element_type=F32, precision=HIGHEST)
    lane = lax.broadcasted_iota(jnp.int32, logits.shape, 1)
    logits = jnp.where(lane < N_EXPERTS, logits, -jnp.inf)
    m1 = jnp.max(logits, axis=-1, keepdims=True)
    i1 = jnp.min(jnp.where(logits == m1, lane, LANE), axis=-1, keepdims=True)
    rest = jnp.where(lane == i1, -jnp.inf, logits)
    m2 = jnp.max(rest, axis=-1, keepdims=True)
    i2 = jnp.min(jnp.where(rest == m2, lane, LANE), axis=-1, keepdims=True)
    e = jnp.exp(m2 - m1)
    g1 = 1.0 / (1.0 + e)
    comb_ref[...] = jnp.where(lane == i1, g1, 0.0) + jnp.where(lane == i2, e * g1, 0.0)


def _mix_out(o, y, x, w_o, post_g, g_m, pre_g, sc_f, sh_f, router, tiles_per_batch):
    N, D = x.shape
    tm = TM_OUT
    row = lambda w: pl.BlockSpec((tm, w), lambda i: (i, 0))
    full = lambda a: pl.BlockSpec(a.shape, lambda i: (0,) * a.ndim)
    per_b = pl.BlockSpec((1, 1, D), lambda i: (i // tiles_per_batch, 0, 0))
    in_specs = [row(o.shape[1]), row(y.shape[1]), row(D), full(w_o), full(post_g), per_b, full(pre_g), per_b, per_b]
    out_specs = [row(D), row(D)]
    out_shape = [jax.ShapeDtypeStruct((N, D), F32), jax.ShapeDtypeStruct((N, D), BF16)]
    args = [o, y, x, w_o, post_g, g_m, pre_g, sc_f, sh_f]
    body = _mix_out_kernel
    if router is not None:
        in_specs.append(full(router))
        out_specs.append(row(LANE))
        out_shape.append(jax.ShapeDtypeStruct((N, LANE), F32))
        args.append(router)
        body = _mix_out_moe_kernel
    return pl.pallas_call(
        body,
        grid=(N // tm,),
        in_specs=in_specs,
        out_specs=out_specs,
        out_shape=out_shape,
        compiler_params=pltpu.CompilerParams(dimension_semantics=("arbitrary",), vmem_limit_bytes=VMEM_LIMIT),
        name="mix_out",
    )(*args)


def _swiglu(h, wg, wu, wd):
    g = _bdot(h, wg)
    u = _bdot(h, wu)
    return _bdot((_silu(g) * u).astype(BF16), wd)


def _ffn_kernel(h_ref, x_ref, wg_ref, wu_ref, wd_ref, postg_ref, gf_ref, o_ref, acc_ref):
    j = pl.program_id(1)

    @pl.when(j == 0)
    def _():
        acc_ref[...] = jnp.zeros_like(acc_ref)

    acc_ref[...] += _swiglu(h_ref[...], wg_ref[...], wu_ref[...], wd_ref[...])

    @pl.when(j == pl.num_programs(1) - 1)
    def _():
        o_ref[...] = x_ref[...] + gf_ref[0] * _rms(acc_ref[...], postg_ref[...])


def _ffn(h, x, wg, wu, wd, post_g, g_f, tiles_per_batch):
    N, D = x.shape
    FF = wg.shape[1]
    tm = TM_FFN
    tf = FF // 2
    return pl.pallas_call(
        _ffn_kernel,
        grid=(N // tm, FF // tf),
        in_specs=[
            pl.BlockSpec((tm, D), lambda i, j: (i, 0)),
            pl.BlockSpec((tm, D), lambda i, j: (i, 0)),
            pl.BlockSpec((D, tf), lambda i, j: (0, j)),
            pl.BlockSpec((D, tf), lambda i, j: (0, j)),
            pl.BlockSpec((tf, D), lambda i, j: (j, 0)),
            pl.BlockSpec((1, D), lambda i, j: (0, 0)),
            pl.BlockSpec((1, 1, D), lambda i, j: (i // tiles_per_batch, 0, 0)),
        ],
        out_specs=pl.BlockSpec((tm, D), lambda i, j: (i, 0)),
        out_shape=jax.ShapeDtypeStruct((N, D), F32),
        scratch_shapes=[pltpu.VMEM((tm, D), F32)],
        compiler_params=pltpu.CompilerParams(
            dimension_semantics=("arbitrary", "arbitrary"), vmem_limit_bytes=VMEM_LIMIT),
        name="ffn_dense",
    )(h, x, wg, wu, wd, post_g, g_f)


def _moe_kernel(h_ref, x_ref, comb_ref, wg_ref, wu_ref, wd_ref, postg_ref, gf_ref, o_ref, acc_ref):
    e = pl.program_id(1)

    @pl.when(e == 0)
    def _():
        acc_ref[...] = jnp.zeros_like(acc_ref)

    comb = comb_ref[...]
    lane = lax.broadcasted_iota(jnp.int32, comb.shape, 1)
    w = jnp.sum(jnp.where(lane == e, comb, 0.0), axis=-1, keepdims=True)
    acc_ref[...] += w * _swiglu(h_ref[...], wg_ref[0], wu_ref[0], wd_ref[0])

    @pl.when(e == pl.num_programs(1) - 1)
    def _():
        o_ref[...] = x_ref[...] + gf_ref[0] * _rms(acc_ref[...], postg_ref[...])


def _moe(h, x, comb, wg, wu, wd, post_g, g_f, tiles_per_batch):
    N, D = x.shape
    E, _, FF = wg.shape
    tm = TM_FFN
    return pl.pallas_call(
        _moe_kernel,
        grid=(N // tm, E),
        in_specs=[
            pl.BlockSpec((tm, D), lambda i, e: (i, 0)),
            pl.BlockSpec((tm, D), lambda i, e: (i, 0)),
            pl.BlockSpec((tm, LANE), lambda i, e: (i, 0)),
            pl.BlockSpec((1, D, FF), lambda i, e: (e, 0, 0)),
            pl.BlockSpec((1, D, FF), lambda i, e: (e, 0, 0)),
            pl.BlockSpec((1, FF, D), lambda i, e: (e, 0, 0)),
            pl.BlockSpec((1, D), lambda i, e: (0, 0)),
            pl.BlockSpec((1, 1, D), lambda i, e: (i // tiles_per_batch, 0, 0)),
        ],
        out_specs=pl.BlockSpec((tm, D), lambda i, e: (i, 0)),
        out_shape=jax.ShapeDtypeStruct((N, D), F32),
        scratch_shapes=[pltpu.VMEM((tm, D), F32)],
        compiler_params=pltpu.CompilerParams(
            dimension_semantics=("arbitrary", "arbitrary"), vmem_limit_bytes=VMEM_LIMIT),
        name="ffn_moe",
    )(h, x, comb, wg, wu, wd, post_g, g_f)


def _rope_partner(w):
    half = QK_ROPE // 2
    return jnp.concatenate([-w[..., half:], w[..., :half]], axis=-1)


def _rope_tables(S):
    pos = jnp.arange(S, dtype=F32)
    inv = ROPE_THETA ** (-jnp.arange(0, QK_ROPE, 2, dtype=F32) / QK_ROPE)
    ang = pos[:, None] * inv[None, :]
    cos, sin = jnp.cos(ang), jnp.sin(ang)
    cos2 = jnp.concatenate([cos, cos], axis=-1)
    sin2 = jnp.concatenate([sin, sin], axis=-1)
    scale = (QK_NOPE + QK_ROPE) ** -0.5
    pad = HEAD_PAD - QK_NOPE - QK_ROPE
    cosq = jnp.concatenate([jnp.ones((S, QK_NOPE), F32), cos2, jnp.zeros((S, pad), F32)], axis=-1) * scale
    sinq = jnp.concatenate([jnp.zeros((S, QK_NOPE), F32), sin2, jnp.zeros((S, pad), F32)], axis=-1) * scale
    tk = jnp.concatenate([cos2, sin2, jnp.zeros((S, LANE - 2 * QK_ROPE), F32)], axis=-1)
    return {"cosq": cosq, "sinq": sinq, "tk": tk}


def _layer_params(l, w_in, q_norm_g, w_uq, kv_norm_g, w_ukv, sgu_ln_g, sgu_ln_b, sgu_w, sgu_b,
                  conv_w, conv_b, lru_wa, lru_ba, lru_wx, lru_bx, lru_lambda):
    D = w_in.shape[1]
    wi = w_in[l]
    o_q, o_kv = 0, Q_LORA
    o_kr = o_kv + KV_LORA
    o_u = o_kr + QK_ROPE
    o_v = o_u + SGU_WIDTH
    o_x = o_v + SGU_WIDTH
    o_y = o_x + LRU_WIDTH
    w_kr = wi[:, o_kr:o_kr + QK_ROPE]
    w_in_p = jnp.concatenate([
        wi[:, o_q:o_q + Q_LORA], wi[:, o_kv:o_kv + KV_LORA], wi[:, o_u:o_u + SGU_WIDTH],
        wi[:, o_v:o_v + SGU_WIDTH], wi[:, o_x:o_x + LRU_WIDTH], wi[:, o_y:o_y + LRU_WIDTH],
        w_kr, _rope_partner(w_kr), jnp.zeros((D, LANE - 2 * QK_ROPE), F32)], axis=-1).astype(BF16)

    H = MLA_HEADS
    pad = HEAD_PAD - QK_NOPE - QK_ROPE
    wq = w_uq[l].reshape(Q_LORA, H, QK_NOPE + QK_ROPE)
    zq = jnp.zeros((Q_LORA, H, pad), F32)
    wq_main = jnp.concatenate([wq, zq], axis=-1).reshape(Q_LORA, H * HEAD_PAD)
    wq_part = jnp.concatenate([jnp.zeros((Q_LORA, H, QK_NOPE), F32), _rope_partner(wq[..., QK_NOPE:]), zq],
                              axis=-1).reshape(Q_LORA, H * HEAD_PAD)
    w_q = jnp.concatenate([wq_main, wq_part], axis=-1).astype(BF16)

    wkv = w_ukv[l].reshape(KV_LORA, H, QK_NOPE + V_DIM)
    wk_lat = jnp.concatenate([wkv[..., :QK_NOPE], jnp.zeros((KV_LORA, H, HEAD_PAD - QK_NOPE), F32)],
                             axis=-1).reshape(KV_LORA, H * HEAD_PAD)
    eye = jnp.eye(QK_ROPE, dtype=F32)
    place = jnp.concatenate([jnp.zeros((QK_ROPE, QK_NOPE), F32), eye, jnp.zeros((QK_ROPE, pad), F32)], axis=-1)
    place = jnp.tile(place, (1, H))
    w_k = jnp.concatenate([wk_lat, place, place, jnp.zeros((LANE - 2 * QK_ROPE, H * HEAD_PAD), F32)],
                          axis=0).astype(BF16)
    w_v = wkv[..., QK_NOPE:].reshape(KV_LORA, H * V_DIM).astype(BF16)

    def block_diag(w):
        out = jnp.zeros((LRU_WIDTH, LRU_WIDTH), F32)
        for hd in range(LRU_HEADS):
            s = hd * LRU_HEAD_DIM
            out = out.at[s:s + LRU_HEAD_DIM, s:s + LRU_HEAD_DIM].set(w[hd])
        return out.astype(BF16)

    row = lambda a: a[l].reshape(1, -1)
    return {
        "w_in": w_in_p, "qg": row(q_norm_g), "w_q": w_q, "kvg": row(kv_norm_g), "w_k": w_k, "w_v": w_v,
        "ln_g": row(sgu_ln_g), "ln_b": row(sgu_ln_b),
        "w_s": sgu_w[l].reshape(SGU_GROUPS * SGU_BLOCK, SGU_BLOCK),
        "b_s": jnp.repeat(sgu_b[l].T, SGU_GROUP_DIM, axis=1),
        "conv_w": conv_w[l], "conv_b": row(conv_b), "wa": block_diag(lru_wa[l]), "ba": row(lru_ba),
        "wx": block_diag(lru_wx[l]), "bx": row(lru_bx), "lam": row(lru_lambda),
    }


def kernel(x, c, w_mod, b_mod, pre_mix_g, post_mix_g, w_in, q_norm_g, w_uq, kv_norm_g, w_ukv, sgu_ln_g,
           sgu_ln_b, sgu_w, sgu_b, conv_w, conv_b, lru_wa, lru_ba, lru_wx, lru_bx, lru_lambda, w_o,
           pre_ffn_g, post_ffn_g, ffn_w_gate, ffn_w_up, ffn_w_down, moe_router, moe_w_gate, moe_w_up,
           moe_w_down):
    B, S, D = x.shape
    L = w_mod.shape[0]
    N = B * S
    tabs = _rope_tables(S)
    mod = _modulation(c, w_mod, b_mod)
    xf = x.reshape(N, D)
    for l in range(L):
        sh_m, sc_m, g_m, sh_f, sc_f, g_f = [m.reshape(B, 1, D) for m in jnp.split(mod[l], 6, axis=-1)]
        p = _layer_params(l, w_in, q_norm_g, w_uq, kv_norm_g, w_ukv, sgu_ln_g, sgu_ln_b, sgu_w, sgu_b,
                          conv_w, conv_b, lru_wa, lru_ba, lru_wx, lru_bx, lru_lambda)
        q, k, v, ymix = _mix_in(xf.reshape(B, S, D), sh_m, sc_m, pre_mix_g[l].reshape(1, D), p, tabs)
        o = _attention(q, k, v)
        moe = l % 2 == 1
        router = None
        if moe:
            router = jnp.pad(moe_router[l // 2], ((0, 0), (0, LANE - N_EXPERTS)))
        res = _mix_out(o.reshape(N, -1), ymix.reshape(N, -1), xf, w_o[l].astype(BF16),
                       post_mix_g[l].reshape(1, D), g_m, pre_ffn_g[l].reshape(1, D), sc_f, sh_f, router,
                       S // TM_OUT)
        post_g = post_ffn_g[l].reshape(1, D)
        if moe:
            xf, h2, comb = res
            xf = _moe(h2, xf, comb, moe_w_gate[l // 2].astype(BF16), moe_w_up[l // 2].astype(BF16),
                      moe_w_down[l // 2].astype(BF16), post_g, g_f, S // TM_FFN)
        else:
            xf, h2 = res
            xf = _ffn(h2, xf, ffn_w_gate[l // 2].astype(BF16), ffn_w_up[l // 2].astype(BF16),
                      ffn_w_down[l // 2].astype(BF16), post_g, g_f, S // TM_FFN)
    return xf.reshape(B, S, D)
```

```python
import functools

import jax
import jax.numpy as jnp
from jax import lax
from jax.experimental import pallas as pl
from jax.experimental.pallas import tpu as pltpu

F32 = jnp.float32
BF16 = jnp.bfloat16
HIGHEST = lax.Precision.HIGHEST

MLA_HEADS = 8
QK_NOPE = 64
QK_ROPE = 32
V_DIM = 64
Q_LORA = 256
KV_LORA = 128
ROPE_THETA = 10000.0
CHUNK = 64
SGU_GROUPS = 4
SGU_GROUP_DIM = 64
SGU_WIDTH = SGU_GROUPS * SGU_GROUP_DIM
SGU_BLOCK = 128
LRU_HEADS = 4
LRU_HEAD_DIM = 64
LRU_WIDTH = LRU_HEADS * LRU_HEAD_DIM
CONV_W = 4
LRU_C = 8.0
N_EXPERTS = 8
NEG_INF = -1e30
RMS_EPS = 1e-6
LN_EPS = 1e-5

LANE = 128
SUBLANE = 8
HEAD_PAD = LANE

C_Q = 0
C_KV = C_Q + Q_LORA
C_U = C_KV + KV_LORA
C_V = C_U + SGU_WIDTH
C_X = C_V + SGU_WIDTH
C_Y = C_X + LRU_WIDTH
C_KR = C_Y + LRU_WIDTH
D_IN_PAD = C_KR + LANE

VMEM_LIMIT = 56 * 1024 * 1024

TM_MIX = 512
TQ = 512
TK = 512
ATTN_HEADS_PER_STEP = 4
TM_OUT = 512
TM_FFN = 512


def _rms(x, g):
    return x * lax.rsqrt(jnp.mean(x * x, axis=-1, keepdims=True) + RMS_EPS) * g


def _gelu(x):
    return 0.5 * x * (1.0 + jnp.tanh(0.7978845608028654 * (x + 0.044715 * (x * x * x))))


def _sigmoid(x):
    return 1.0 / (1.0 + jnp.exp(-x))


def _silu(x):
    return x * _sigmoid(x)


def _bdot(a, b):
    return jnp.dot(a, b, preferred_element_type=F32)


def _mod_kernel(c_ref, w_ref, b_ref, o_ref):
    c = c_ref[...]
    o_ref[0] = jnp.dot(_silu(c), w_ref[0], preferred_element_type=F32, precision=HIGHEST) + b_ref[0]


def _modulation(c, w_mod, b_mod):
    L, D, D6 = w_mod.shape
    B = c.shape[0]
    return pl.pallas_call(
        _mod_kernel,
        grid=(L, D6 // D),
        in_specs=[
            pl.BlockSpec((B, D), lambda l, j: (0, 0)),
            pl.BlockSpec((1, D, D), lambda l, j: (l, 0, j)),
            pl.BlockSpec((1, 1, D), lambda l, j: (l, 0, j)),
        ],
        out_specs=pl.BlockSpec((1, B, D), lambda l, j: (l, 0, j)),
        out_shape=jax.ShapeDtypeStruct((L, B, D6), F32),
        compiler_params=pltpu.CompilerParams(vmem_limit_bytes=VMEM_LIMIT),
        name="modulation",
    )(c, w_mod, b_mod.reshape(L, 1, D6))


def _shift_rows(cur, prev8, k):
    rolled = pltpu.roll(cur, k, 0)
    fix = pltpu.roll(prev8, k, 0)
    row = lax.broadcasted_iota(jnp.int32, fix.shape, 0)
    top = jnp.where(row < k, fix, rolled[:SUBLANE])
    return jnp.concatenate([top, rolled[SUBLANE:]], axis=0)


def _mix_in_kernel(x_ref, sh_ref, sc_ref, preg_ref, win_ref, qg_ref, wq_ref, kvg_ref, wk_ref, wv_ref,
                   cosq_ref, sinq_ref, tk_ref, lng_ref, lnb_ref, ws_ref, bs_ref,
                   cw_ref, cb_ref, wa_ref, ba_ref, wx_ref, bx_ref, lam_ref,
                   q_ref, k_ref, v_ref, y_ref, prev_ref, hc_ref):
    tm = x_ref.shape[1]

    @pl.when(pl.program_id(1) == 0)
    def _():
        prev_ref[...] = jnp.zeros_like(prev_ref)
        hc_ref[...] = jnp.zeros_like(hc_ref)

    x = x_ref[0]
    h = _rms(x, preg_ref[...]) * (1.0 + sc_ref[0]) + sh_ref[0]
    z = _bdot(h.astype(BF16), win_ref[...])

    nt = (((1,), (1,)), ((), ()))
    qn = _rms(z[:, C_Q:C_Q + Q_LORA], qg_ref[...]).astype(BF16)
    qq = lax.dot_general(wq_ref[...], qn, nt, preferred_element_type=F32)
    cosq = cosq_ref[...]
    sinq = sinq_ref[...]
    qw = MLA_HEADS * HEAD_PAD
    for hd in range(MLA_HEADS):
        lo = hd * HEAD_PAD
        qh = qq[lo:lo + HEAD_PAD] * cosq + qq[qw + lo:qw + lo + HEAD_PAD] * sinq
        q_ref[0, lo:lo + HEAD_PAD, :] = qh.astype(BF16)
    kvn = _rms(z[:, C_KV:C_KV + KV_LORA], kvg_ref[...]).astype(BF16)
    kr = (z[:, C_KR:C_KR + LANE] * tk_ref[...]).astype(BF16)
    k_in = jnp.concatenate([kvn, kr], axis=-1)
    k_ref[0] = _bdot(k_in, wk_ref[...]).astype(BF16)
    vv = lax.dot_general(wv_ref[...], kvn, nt, preferred_element_type=F32)
    ones_row = lax.broadcasted_iota(jnp.int32, vv.shape, 0) % HEAD_PAD == V_DIM
    v_ref[0, 0] = jnp.where(ones_row, 1.0, vv).astype(BF16)

    u = _gelu(z[:, C_U:C_U + SGU_WIDTH])
    gv = _gelu(z[:, C_V:C_V + SGU_WIDTH])
    mu = jnp.mean(gv, axis=-1, keepdims=True)
    var = jnp.mean(jnp.square(gv - mu), axis=-1, keepdims=True)
    vn = ((gv - mu) * lax.rsqrt(var + LN_EPS) * lng_ref[...] + lnb_ref[...]).astype(BF16)
    ws = ws_ref[...]
    r_t = lax.broadcasted_iota(jnp.int32, ws.shape, 0) % SGU_BLOCK
    c_s = lax.broadcasted_iota(jnp.int32, ws.shape, 1)
    ws = jnp.where(c_s <= r_t, ws, 0.0).astype(BF16)
    grp = lax.broadcasted_iota(jnp.int32, (SGU_BLOCK, SGU_WIDTH), 1) // SGU_GROUP_DIM
    for blk in range(tm // SGU_BLOCK):
        r0 = blk * SGU_BLOCK
        res = _bdot(ws, vn[r0:r0 + SGU_BLOCK])
        mixed = bs_ref[...]
        for g in range(SGU_GROUPS):
            mixed = mixed + jnp.where(grp == g, res[g * SGU_BLOCK:(g + 1) * SGU_BLOCK], 0.0)
        y_ref[0, r0:r0 + SGU_BLOCK, 0:SGU_WIDTH] = (u[r0:r0 + SGU_BLOCK] * mixed).astype(BF16)

    zx = z[:, C_X:C_X + LRU_WIDTH]
    prev8 = prev_ref[...]
    cw = cw_ref[...]
    xc = cb_ref[...] + zx * cw[CONV_W - 1:CONV_W]
    for k in range(1, CONV_W):
        xc = xc + _shift_rows(zx, prev8, k) * cw[CONV_W - 1 - k:CONV_W - k]
    prev_ref[...] = zx[tm - SUBLANE:]
    xcb = xc.astype(BF16)
    r = _sigmoid(_bdot(xcb, wa_ref[...]) + ba_ref[...])
    ig = _sigmoid(_bdot(xcb, wx_ref[...]) + bx_ref[...])
    lam = lam_ref[...]
    softplus_neg_lam = jnp.maximum(-lam, 0.0) + jnp.log(1.0 + jnp.exp(-jnp.abs(lam)))
    log_a = -LRU_C * r * softplus_neg_lam
    a = jnp.exp(log_a)
    t = jnp.tanh(log_a)
    b = jnp.sqrt(-2.0 * t / (1.0 - t)) * (ig * xc)
    row = lax.broadcasted_iota(jnp.int32, a.shape, 0)
    k = 1
    while k < tm:
        valid = row >= k
        a_sh = pltpu.roll(a, k, 0)
        b_sh = pltpu.roll(b, k, 0)
        b = jnp.where(valid, b + a * b_sh, b)
        a = jnp.where(valid, a * a_sh, a)
        k *= 2
    hs = b + a * hc_ref[0:1]
    hc_ref[0:1] = hs[tm - 1:tm]
    y_ref[0, :, SGU_WIDTH:SGU_WIDTH + LRU_WIDTH] = (hs * _gelu(z[:, C_Y:C_Y + LRU_WIDTH])).astype(BF16)


def _mix_in(x, sh, sc, pre_g, p, tabs):
    B, S, D = x.shape
    tm = TM_MIX
    full = lambda a: pl.BlockSpec(a.shape, lambda b, s: (0,) * a.ndim)
    per_b = pl.BlockSpec((1, 1, D), lambda b, s: (b, 0, 0))
    tab = pl.BlockSpec((tm, LANE), lambda b, s: (s, 0))
    tab_t = pl.BlockSpec((HEAD_PAD, tm), lambda b, s: (0, s))
    assert tm == TK
    consts = [p["w_in"], p["qg"], p["w_q"], p["kvg"], p["w_k"], p["w_v"]]
    consts2 = [p["ln_g"], p["ln_b"], p["w_s"], p["b_s"], p["conv_w"], p["conv_b"], p["wa"], p["ba"],
               p["wx"], p["bx"], p["lam"]]
    qw = MLA_HEADS * HEAD_PAD
    return pl.pallas_call(
        _mix_in_kernel,
        grid=(B, S // tm),
        in_specs=[pl.BlockSpec((1, tm, D), lambda b, s: (b, s, 0)), per_b, per_b, full(pre_g)]
        + [full(a) for a in consts] + [tab_t, tab_t, tab] + [full(a) for a in consts2],
        out_specs=[
            pl.BlockSpec((1, qw, tm), lambda b, s: (b, 0, s)),
            pl.BlockSpec((1, tm, qw), lambda b, s: (b, s, 0)),
            pl.BlockSpec((1, 1, qw, tm), lambda b, s: (b, s, 0, 0)),
            pl.BlockSpec((1, tm, SGU_WIDTH + LRU_WIDTH), lambda b, s: (b, s, 0)),
        ],
        out_shape=[
            jax.ShapeDtypeStruct((B, qw, S), BF16),
            jax.ShapeDtypeStruct((B, S, qw), BF16),
            jax.ShapeDtypeStruct((B, S // tm, qw, tm), BF16),
            jax.ShapeDtypeStruct((B, S, SGU_WIDTH + LRU_WIDTH), BF16),
        ],
        scratch_shapes=[pltpu.VMEM((SUBLANE, LRU_WIDTH), F32), pltpu.VMEM((SUBLANE, LRU_WIDTH), F32)],
        compiler_params=pltpu.CompilerParams(
            dimension_semantics=("arbitrary", "arbitrary"), vmem_limit_bytes=VMEM_LIMIT),
        name="mix_in",
    )(x, sh, sc, pre_g, *consts, tabs["cosq"], tabs["sinq"], tabs["tk"], *consts2)


def _attn_kernel(q_ref, k_ref, v_ref, o_ref, m_ref, acc_ref, p_ref, a_ref):
    i = pl.program_id(2)
    heads = q_ref.shape[1] // HEAD_PAD
    feat = lambda hd: slice(hd * HEAD_PAD, (hd + 1) * HEAD_PAD)

    m_ref[...] = jnp.full(m_ref.shape, NEG_INF, F32)
    acc_ref[...] = jnp.zeros(acc_ref.shape, F32)
    p_ref[...] = jnp.zeros(p_ref.shape, BF16)
    a_ref[...] = jnp.ones(a_ref.shape, F32)

    def scores(t, hd):
        keys = pl.ds(pl.multiple_of(t * TK, TK), TK)
        return _bdot(k_ref[0, keys, feat(hd)], q_ref[0, feat(hd), :])

    def weights(s, hd):
        m = m_ref[hd]
        m_new = jnp.maximum(m, jnp.max(s, axis=0, keepdims=True))
        p_ref[hd] = jnp.exp2(s - m_new).astype(BF16)
        a_ref[hd] = jnp.exp2(m - m_new)
        m_ref[hd] = m_new

    def accumulate(t, hd):
        acc_ref[hd] = a_ref[hd] * acc_ref[hd] + _bdot(v_ref[0, t, feat(hd), :], p_ref[hd])

    def tile(t, masked):
        ss = [scores(t, hd) for hd in range(heads)]
        for hd in range(heads):
            accumulate(jnp.maximum(t - 1, 0), hd)
        for hd in range(heads):
            s = ss[hd]
            if masked:
                kc = lax.broadcasted_iota(jnp.int32, s.shape, 0) // CHUNK
                qc = lax.broadcasted_iota(jnp.int32, s.shape, 1) // CHUNK
                s = jnp.where(kc <= qc, s, NEG_INF)
            weights(s, hd)

    def body(t, c):
        tile(t, False)
        return c

    lax.fori_loop(0, i, body, 0)
    tile(i, True)
    outs = []
    for hd in range(heads):
        accumulate(i, hd)
        acc = acc_ref[hd]
        outs.append((acc / acc[V_DIM:V_DIM + 1]).T[:, :V_DIM])
    o_ref[0] = jnp.concatenate(outs, axis=-1).astype(BF16)


def _attention(q, k, v):
    B, S, _ = k.shape
    hp = ATTN_HEADS_PER_STEP
    assert TQ == TK
    return pl.pallas_call(
        _attn_kernel,
        grid=(B, MLA_HEADS // hp, S // TQ),
        in_specs=[
            pl.BlockSpec((1, hp * HEAD_PAD, TQ), lambda b, h, i: (b, h, i)),
            pl.BlockSpec((1, S, hp * HEAD_PAD), lambda b, h, i: (b, 0, h)),
            pl.BlockSpec((1, S // TK, hp * HEAD_PAD, TK), lambda b, h, i: (b, 0, h, 0)),
        ],
        out_specs=pl.BlockSpec((1, TQ, hp * V_DIM), lambda b, h, i: (b, i, h)),
        out_shape=jax.ShapeDtypeStruct((B, S, MLA_HEADS * V_DIM), BF16),
        scratch_shapes=[
            pltpu.VMEM((hp, 1, TQ), F32),
            pltpu.VMEM((hp, HEAD_PAD, TQ), F32),
            pltpu.VMEM((hp, TK, TQ), BF16),
            pltpu.VMEM((hp, 1, TQ), F32),
        ],
        compiler_params=pltpu.CompilerParams(
            dimension_semantics=("arbitrary", "arbitrary", "arbitrary"), vmem_limit_bytes=VMEM_LIMIT),
        name="attention",
    )(q, k, v)


def _mix_out_kernel(o_ref, y_ref, x_ref, wo_ref, postg_ref, gm_ref, preg_ref, scf_ref, shf_ref,
                    xo_ref, h_ref):
    no = o_ref.shape[1]
    y = _bdot(o_ref[...], wo_ref[0:no]) + _bdot(y_ref[...], wo_ref[no:])
    x = x_ref[...] + gm_ref[0] * _rms(y, postg_ref[...])
    xo_ref[...] = x
    h_ref[...] = (_rms(x, preg_ref[...]) * (1.0 + scf_ref[0]) + shf_ref[0]).astype(BF16)


def _mix_out_moe_kernel(o_ref, y_ref, x_ref, wo_ref, postg_ref, gm_ref, preg_ref, scf_ref, shf_ref,
                        router_ref, xo_ref, h_ref, comb_ref):
    no = o_ref.shape[1]
    y = _bdot(o_ref[...], wo_ref[0:no]) + _bdot(y_ref[...], wo_ref[no:])
    x = x_ref[...] + gm_ref[0] * _rms(y, postg_ref[...])
    xo_ref[...] = x
    h = _rms(x, preg_ref[...]) * (1.0 + scf_ref[0]) + shf_ref[0]
    h_ref[...] = h.astype(BF16)
    logits = jnp.dot(h, router_ref[...], preferred_element_type=F32, precision=HIGHEST)
    lane = lax.broadcasted_iota(jnp.int32, logits.shape, 1)
    logits = jnp.where(lane < N_EXPERTS, logits, -jnp.inf)
    m1 = jnp.max(logits, axis=-1, keepdims=True)
    i1 = jnp.min(jnp.where(logits == m1, lane, LANE), axis=-1, keepdims=True)
    rest = jnp.where(lane == i1, -jnp.inf, logits)
    m2 = jnp.max(rest, axis=-1, keepdims=True)
    i2 = jnp.min(jnp.where(rest == m2, lane, LANE), axis=-1, keepdims=True)
    e = jnp.exp(m2 - m1)
    g1 = 1.0 / (1.0 + e)
    comb_ref[...] = jnp.where(lane == i1, g1, 0.0) + jnp.where(lane == i2, e * g1, 0.0)


def _mix_out(o, y, x, w_o, post_g, g_m, pre_g, sc_f, sh_f, router, tiles_per_batch):
    N, D = x.shape
    tm = TM_OUT
    row = lambda w: pl.BlockSpec((tm, w), lambda i: (i, 0))
    full = lambda a: pl.BlockSpec(a.shape, lambda i: (0,) * a.ndim)
    per_b = pl.BlockSpec((1, 1, D), lambda i: (i // tiles_per_batch, 0, 0))
    in_specs = [row(o.shape[1]), row(y.shape[1]), row(D), full(w_o), full(post_g), per_b, full(pre_g), per_b, per_b]
    out_specs = [row(D), row(D)]
    out_shape = [jax.ShapeDtypeStruct((N, D), F32), jax.ShapeDtypeStruct((N, D), BF16)]
    args = [o, y, x, w_o, post_g, g_m, pre_g, sc_f, sh_f]
    body = _mix_out_kernel
    if router is not None:
        in_specs.append(full(router))
        out_specs.append(row(LANE))
        out_shape.append(jax.ShapeDtypeStruct((N, LANE), F32))
        args.append(router)
        body = _mix_out_moe_kernel
    return pl.pallas_call(
        body,
        grid=(N // tm,),
        in_specs=in_specs,
        out_specs=out_specs,
        out_shape=out_shape,
        compiler_params=pltpu.CompilerParams(dimension_semantics=("arbitrary",), vmem_limit_bytes=VMEM_LIMIT),
        name="mix_out",
    )(*args)


def _swiglu(h, wg, wu, wd):
    g = _bdot(h, wg)
    u = _bdot(h, wu)
    return _bdot((_silu(g) * u).astype(BF16), wd)


def _ffn_kernel(h_ref, x_ref, wg_ref, wu_ref, wd_ref, postg_ref, gf_ref, o_ref, acc_ref):
    j = pl.program_id(1)

    @pl.when(j == 0)
    def _():
        acc_ref[...] = jnp.zeros_like(acc_ref)

    acc_ref[...] += _swiglu(h_ref[...], wg_ref[...], wu_ref[...], wd_ref[...])

    @pl.when(j == pl.num_programs(1) - 1)
    def _():
        o_ref[...] = x_ref[...] + gf_ref[0] * _rms(acc_ref[...], postg_ref[...])


def _ffn(h, x, wg, wu, wd, post_g, g_f, tiles_per_batch):
    N, D = x.shape
    FF = wg.shape[1]
    tm = TM_FFN
    tf = FF // 2
    return pl.pallas_call(
        _ffn_kernel,
        grid=(N // tm, FF // tf),
        in_specs=[
            pl.BlockSpec((tm, D), lambda i, j: (i, 0)),
            pl.BlockSpec((tm, D), lambda i, j: (i, 0)),
            pl.BlockSpec((D, tf), lambda i, j: (0, j)),
            pl.BlockSpec((D, tf), lambda i, j: (0, j)),
            pl.BlockSpec((tf, D), lambda i, j: (j, 0)),
            pl.BlockSpec((1, D), lambda i, j: (0, 0)),
            pl.BlockSpec((1, 1, D), lambda i, j: (i // tiles_per_batch, 0, 0)),
        ],
        out_specs=pl.BlockSpec((tm, D), lambda i, j: (i, 0)),
        out_shape=jax.ShapeDtypeStruct((N, D), F32),
        scratch_shapes=[pltpu.VMEM((tm, D), F32)],
        compiler_params=pltpu.CompilerParams(
            dimension_semantics=("arbitrary", "arbitrary"), vmem_limit_bytes=VMEM_LIMIT),
        name="ffn_dense",
    )(h, x, wg, wu, wd, post_g, g_f)


def _moe_kernel(h_ref, x_ref, comb_ref, wg_ref, wu_ref, wd_ref, postg_ref, gf_ref, o_ref, acc_ref):
    e = pl.program_id(1)

    @pl.when(e == 0)
    def _():
        acc_ref[...] = jnp.zeros_like(acc_ref)

    comb = comb_ref[...]
    lane = lax.broadcasted_iota(jnp.int32, comb.shape, 1)
    w = jnp.sum(jnp.where(lane == e, comb, 0.0), axis=-1, keepdims=True)
    acc_ref[...] += w * _swiglu(h_ref[...], wg_ref[0], wu_ref[0], wd_ref[0])

    @pl.when(e == pl.num_programs(1) - 1)
    def _():
        o_ref[...] = x_ref[...] + gf_ref[0] * _rms(acc_ref[...], postg_ref[...])


def _moe(h, x, comb, wg, wu, wd, post_g, g_f, tiles_per_batch):
    N, D = x.shape
    E, _, FF = wg.shape
    tm = TM_FFN
    return pl.pallas_call(
        _moe_kernel,
        grid=(N // tm, E),
        in_specs=[
            pl.BlockSpec((tm, D), lambda i, e: (i, 0)),
            pl.BlockSpec((tm, D), lambda i, e: (i, 0)),
            pl.BlockSpec((tm, LANE), lambda i, e: (i, 0)),
            pl.BlockSpec((1, D, FF), lambda i, e: (e, 0, 0)),
            pl.BlockSpec((1, D, FF), lambda i, e: (e, 0, 0)),
            pl.BlockSpec((1, FF, D), lambda i, e: (e, 0, 0)),
            pl.BlockSpec((1, D), lambda i, e: (0, 0)),
            pl.BlockSpec((1, 1, D), lambda i, e: (i // tiles_per_batch, 0, 0)),
        ],
        out_specs=pl.BlockSpec((tm, D), lambda i, e: (i, 0)),
        out_shape=jax.ShapeDtypeStruct((N, D), F32),
        scratch_shapes=[pltpu.VMEM((tm, D), F32)],
        compiler_params=pltpu.CompilerParams(
            dimension_semantics=("arbitrary", "arbitrary"), vmem_limit_bytes=VMEM_LIMIT),
        name="ffn_moe",
    )(h, x, comb, wg, wu, wd, post_g, g_f)


def _rope_partner(w):
    half = QK_ROPE // 2
    return jnp.concatenate([-w[..., half:], w[..., :half]], axis=-1)


def _rope_tables(S):
    pos = jnp.arange(S, dtype=F32)
    inv = ROPE_THETA ** (-jnp.arange(0, QK_ROPE, 2, dtype=F32) / QK_ROPE)
    ang = pos[:, None] * inv[None, :]
    cos, sin = jnp.cos(ang), jnp.sin(ang)
    cos2 = jnp.concatenate([cos, cos], axis=-1)
    sin2 = jnp.concatenate([sin, sin], axis=-1)
    scale = (QK_NOPE + QK_ROPE) ** -0.5 * 1.4426950408889634
    pad = HEAD_PAD - QK_NOPE - QK_ROPE
    cosq = jnp.concatenate([jnp.ones((S, QK_NOPE), F32), cos2, jnp.zeros((S, pad), F32)], axis=-1) * scale
    sinq = jnp.concatenate([jnp.zeros((S, QK_NOPE), F32), sin2, jnp.zeros((S, pad), F32)], axis=-1) * scale
    tk = jnp.concatenate([cos2, sin2, jnp.zeros((S, LANE - 2 * QK_ROPE), F32)], axis=-1)
    return {"cosq": cosq.T, "sinq": sinq.T, "tk": tk}


def _layer_params(l, w_in, q_norm_g, w_uq, kv_norm_g, w_ukv, sgu_ln_g, sgu_ln_b, sgu_w, sgu_b,
                  conv_w, conv_b, lru_wa, lru_ba, lru_wx, lru_bx, lru_lambda):
    D = w_in.shape[1]
    wi = w_in[l]
    o_q, o_kv = 0, Q_LORA
    o_kr = o_kv + KV_LORA
    o_u = o_kr + QK_ROPE
    o_v = o_u + SGU_WIDTH
    o_x = o_v + SGU_WIDTH
    o_y = o_x + LRU_WIDTH
    w_kr = wi[:, o_kr:o_kr + QK_ROPE]
    w_in_p = jnp.concatenate([
        wi[:, o_q:o_q + Q_LORA], wi[:, o_kv:o_kv + KV_LORA], wi[:, o_u:o_u + SGU_WIDTH],
        wi[:, o_v:o_v + SGU_WIDTH], wi[:, o_x:o_x + LRU_WIDTH], wi[:, o_y:o_y + LRU_WIDTH],
        w_kr, _rope_partner(w_kr), jnp.zeros((D, LANE - 2 * QK_ROPE), F32)], axis=-1).astype(BF16)

    H = MLA_HEADS
    pad = HEAD_PAD - QK_NOPE - QK_ROPE
    wq = w_uq[l].reshape(Q_LORA, H, QK_NOPE + QK_ROPE)
    zq = jnp.zeros((Q_LORA, H, pad), F32)
    wq_main = jnp.concatenate([wq, zq], axis=-1).reshape(Q_LORA, H * HEAD_PAD)
    wq_part = jnp.concatenate([jnp.zeros((Q_LORA, H, QK_NOPE), F32), _rope_partner(wq[..., QK_NOPE:]), zq],
                              axis=-1).reshape(Q_LORA, H * HEAD_PAD)
    w_q = jnp.concatenate([wq_main, wq_part], axis=-1).T.astype(BF16)

    wkv = w_ukv[l].reshape(KV_LORA, H, QK_NOPE + V_DIM)
    wk_lat = jnp.concatenate([wkv[..., :QK_NOPE], jnp.zeros((KV_LORA, H, HEAD_PAD - QK_NOPE), F32)],
                             axis=-1).reshape(KV_LORA, H * HEAD_PAD)
    eye = jnp.eye(QK_ROPE, dtype=F32)
    place = jnp.concatenate([jnp.zeros((QK_ROPE, QK_NOPE), F32), eye, jnp.zeros((QK_ROPE, pad), F32)], axis=-1)
    place = jnp.tile(place, (1, H))
    w_k = jnp.concatenate([wk_lat, place, place, jnp.zeros((LANE - 2 * QK_ROPE, H * HEAD_PAD), F32)],
                          axis=0).astype(BF16)
    w_v = jnp.concatenate([wkv[..., QK_NOPE:], jnp.zeros((KV_LORA, H, HEAD_PAD - V_DIM), F32)],
                          axis=-1).reshape(KV_LORA, H * HEAD_PAD).T.astype(BF16)

    def block_diag(w):
        out = jnp.zeros((LRU_WIDTH, LRU_WIDTH), F32)
        for hd in range(LRU_HEADS):
            s = hd * LRU_HEAD_DIM
            out = out.at[s:s + LRU_HEAD_DIM, s:s + LRU_HEAD_DIM].set(w[hd])
        return out.astype(BF16)

    row = lambda a: a[l].reshape(1, -1)
    return {
        "w_in": w_in_p, "qg": row(q_norm_g), "w_q": w_q, "kvg": row(kv_norm_g), "w_k": w_k, "w_v": w_v,
        "ln_g": row(sgu_ln_g), "ln_b": row(sgu_ln_b),
        "w_s": sgu_w[l].reshape(SGU_GROUPS * SGU_BLOCK, SGU_BLOCK),
        "b_s": jnp.repeat(sgu_b[l].T, SGU_GROUP_DIM, axis=1),
        "conv_w": conv_w[l], "conv_b": row(conv_b), "wa": block_diag(lru_wa[l]), "ba": row(lru_ba),
        "wx": block_diag(lru_wx[l]), "bx": row(lru_bx), "lam": row(lru_lambda),
    }


def kernel(x, c, w_mod, b_mod, pre_mix_g, post_mix_g, w_in, q_norm_g, w_uq, kv_norm_g, w_ukv, sgu_ln_g,
           sgu_ln_b, sgu_w, sgu_b, conv_w, conv_b, lru_wa, lru_ba, lru_wx, lru_bx, lru_lambda, w_o,
           pre_ffn_g, post_ffn_g, ffn_w_gate, ffn_w_up, ffn_w_down, moe_router, moe_w_gate, moe_w_up,
           moe_w_down):
    B, S, D = x.shape
    L = w_mod.shape[0]
    N = B * S
    tabs = _rope_tables(S)
    mod = _modulation(c, w_mod, b_mod)
    xf = x.reshape(N, D)
    for l in range(L):
        sh_m, sc_m, g_m, sh_f, sc_f, g_f = [m.reshape(B, 1, D) for m in jnp.split(mod[l], 6, axis=-1)]
        p = _layer_params(l, w_in, q_norm_g, w_uq, kv_norm_g, w_ukv, sgu_ln_g, sgu_ln_b, sgu_w, sgu_b,
                          conv_w, conv_b, lru_wa, lru_ba, lru_wx, lru_bx, lru_lambda)
        q, k, v, ymix = _mix_in(xf.reshape(B, S, D), sh_m, sc_m, pre_mix_g[l].reshape(1, D), p, tabs)
        o = _attention(q, k, v)
        moe = l % 2 == 1
        router = None
        if moe:
            router = jnp.pad(moe_router[l // 2], ((0, 0), (0, LANE - N_EXPERTS)))
        res = _mix_out(o.reshape(N, -1), ymix.reshape(N, -1), xf, w_o[l].astype(BF16),
                       post_mix_g[l].reshape(1, D), g_m, pre_ffn_g[l].reshape(1, D), sc_f, sh_f, router,
                       S // TM_OUT)
        post_g = post_ffn_g[l].reshape(1, D)
        if moe:
            xf, h2, comb = res
            xf = _moe(h2, xf, comb, moe_w_gate[l // 2].astype(BF16), moe_w_up[l // 2].astype(BF16),
                      moe_w_down[l // 2].astype(BF16), post_g, g_f, S // TM_FFN)
        else:
            xf, h2 = res
            xf = _ffn(h2, xf, ffn_w_gate[l // 2].astype(BF16), ffn_w_up[l // 2].astype(BF16),
                      ffn_w_down[l // 2].astype(BF16), post_g, g_f, S // TM_FFN)
    return xf.reshape(B, S, D)
```

```python
import functools

import jax
import jax.numpy as jnp
from jax import lax
from jax.experimental import pallas as pl
from jax.experimental.pallas import tpu as pltpu

F32 = jnp.float32
BF16 = jnp.bfloat16
HIGHEST = lax.Precision.HIGHEST

MLA_HEADS = 8
QK_NOPE = 64
QK_ROPE = 32
V_DIM = 64
Q_LORA = 256
KV_LORA = 128
ROPE_THETA = 10000.0
CHUNK = 64
SGU_GROUPS = 4
SGU_GROUP_DIM = 64
SGU_WIDTH = SGU_GROUPS * SGU_GROUP_DIM
SGU_BLOCK = 128
LRU_HEADS = 4
LRU_HEAD_DIM = 64
LRU_WIDTH = LRU_HEADS * LRU_HEAD_DIM
CONV_W = 4
LRU_C = 8.0
N_EXPERTS = 8
NEG_INF = -1e30
RMS_EPS = 1e-6
LN_EPS = 1e-5

LANE = 128
SUBLANE = 8
HEAD_PAD = LANE

C_Q = 0
C_KV = C_Q + Q_LORA
C_U = C_KV + KV_LORA
C_V = C_U + SGU_WIDTH
C_X = C_V + SGU_WIDTH
C_Y = C_X + LRU_WIDTH
C_KR = C_Y + LRU_WIDTH
D_IN_PAD = C_KR + LANE

VMEM_LIMIT = 56 * 1024 * 1024

TM_MIX = 512
TQ = 512
TK = 512
ATTN_HEADS_PER_STEP = 4
TM_OUT = 512
TM_FFN = 512


def _rms(x, g):
    return x * lax.rsqrt(jnp.mean(x * x, axis=-1, keepdims=True) + RMS_EPS) * g


def _gelu(x):
    return 0.5 * x * (1.0 + jnp.tanh(0.7978845608028654 * (x + 0.044715 * (x * x * x))))


def _sigmoid(x):
    return 1.0 / (1.0 + jnp.exp(-x))


def _silu(x):
    return x * _sigmoid(x)


def _bdot(a, b):
    return jnp.dot(a, b, preferred_element_type=F32)


def _mod_kernel(c_ref, w_ref, b_ref, o_ref):
    c = c_ref[...]
    o_ref[0] = jnp.dot(_silu(c), w_ref[0], preferred_element_type=F32, precision=HIGHEST) + b_ref[0]


def _modulation(c, w_mod, b_mod):
    L, D, D6 = w_mod.shape
    B = c.shape[0]
    return pl.pallas_call(
        _mod_kernel,
        grid=(L, D6 // D),
        in_specs=[
            pl.BlockSpec((B, D), lambda l, j: (0, 0)),
            pl.BlockSpec((1, D, D), lambda l, j: (l, 0, j)),
            pl.BlockSpec((1, 1, D), lambda l, j: (l, 0, j)),
        ],
        out_specs=pl.BlockSpec((1, B, D), lambda l, j: (l, 0, j)),
        out_shape=jax.ShapeDtypeStruct((L, B, D6), F32),
        compiler_params=pltpu.CompilerParams(vmem_limit_bytes=VMEM_LIMIT),
        name="modulation",
    )(c, w_mod, b_mod.reshape(L, 1, D6))


def _shift_rows(cur, prev8, k):
    rolled = pltpu.roll(cur, k, 0)
    fix = pltpu.roll(prev8, k, 0)
    row = lax.broadcasted_iota(jnp.int32, fix.shape, 0)
    top = jnp.where(row < k, fix, rolled[:SUBLANE])
    return jnp.concatenate([top, rolled[SUBLANE:]], axis=0)


def _mix_in_kernel(x_ref, sh_ref, sc_ref, preg_ref, win_ref, qg_ref, wq_ref, kvg_ref, wk_ref, wv_ref,
                   cosq_ref, sinq_ref, tk_ref, lng_ref, lnb_ref, ws_ref, bs_ref,
                   cw_ref, cb_ref, wa_ref, ba_ref, wx_ref, bx_ref, lam_ref,
                   q_ref, k_ref, v_ref, y_ref, prev_ref, hc_ref):
    tm = x_ref.shape[1]

    @pl.when(pl.program_id(1) == 0)
    def _():
        prev_ref[...] = jnp.zeros_like(prev_ref)
        hc_ref[...] = jnp.zeros_like(hc_ref)

    x = x_ref[0]
    h = _rms(x, preg_ref[...]) * (1.0 + sc_ref[0]) + sh_ref[0]
    z = _bdot(h.astype(BF16), win_ref[...])

    nt = (((1,), (1,)), ((), ()))
    qn = _rms(z[:, C_Q:C_Q + Q_LORA], qg_ref[...]).astype(BF16)
    qq = lax.dot_general(wq_ref[...], qn, nt, preferred_element_type=F32)
    cosq = cosq_ref[...]
    sinq = sinq_ref[...]
    qw = MLA_HEADS * HEAD_PAD
    for hd in range(MLA_HEADS):
        lo = hd * HEAD_PAD
        qh = qq[lo:lo + HEAD_PAD] * cosq + qq[qw + lo:qw + lo + HEAD_PAD] * sinq
        q_ref[0, lo:lo + HEAD_PAD, :] = qh.astype(BF16)
    kvn = _rms(z[:, C_KV:C_KV + KV_LORA], kvg_ref[...]).astype(BF16)
    kr = (z[:, C_KR:C_KR + LANE] * tk_ref[...]).astype(BF16)
    k_in = jnp.concatenate([kvn, kr], axis=-1)
    k_ref[0] = _bdot(k_in, wk_ref[...]).astype(BF16)
    vv = lax.dot_general(wv_ref[...], kvn, nt, preferred_element_type=F32)
    ones_row = lax.broadcasted_iota(jnp.int32, vv.shape, 0) % HEAD_PAD == V_DIM
    v_ref[0, 0] = jnp.where(ones_row, 1.0, vv).astype(BF16)

    u = _gelu(z[:, C_U:C_U + SGU_WIDTH])
    gv = _gelu(z[:, C_V:C_V + SGU_WIDTH])
    mu = jnp.mean(gv, axis=-1, keepdims=True)
    var = jnp.mean(jnp.square(gv - mu), axis=-1, keepdims=True)
    vn = ((gv - mu) * lax.rsqrt(var + LN_EPS) * lng_ref[...] + lnb_ref[...]).astype(BF16)
    ws = ws_ref[...]
    r_t = lax.broadcasted_iota(jnp.int32, ws.shape, 0) % SGU_BLOCK
    c_s = lax.broadcasted_iota(jnp.int32, ws.shape, 1)
    ws = jnp.where(c_s <= r_t, ws, 0.0).astype(BF16)
    grp = lax.broadcasted_iota(jnp.int32, (SGU_BLOCK, SGU_WIDTH), 1) // SGU_GROUP_DIM
    for blk in range(tm // SGU_BLOCK):
        r0 = blk * SGU_BLOCK
        res = _bdot(ws, vn[r0:r0 + SGU_BLOCK])
        mixed = bs_ref[...]
        for g in range(SGU_GROUPS):
            mixed = mixed + jnp.where(grp == g, res[g * SGU_BLOCK:(g + 1) * SGU_BLOCK], 0.0)
        y_ref[0, r0:r0 + SGU_BLOCK, 0:SGU_WIDTH] = (u[r0:r0 + SGU_BLOCK] * mixed).astype(BF16)

    zx = z[:, C_X:C_X + LRU_WIDTH]
    prev8 = prev_ref[...]
    cw = cw_ref[...]
    xc = cb_ref[...] + zx * cw[CONV_W - 1:CONV_W]
    for k in range(1, CONV_W):
        xc = xc + _shift_rows(zx, prev8, k) * cw[CONV_W - 1 - k:CONV_W - k]
    prev_ref[...] = zx[tm - SUBLANE:]
    xcb = xc.astype(BF16)
    r = _sigmoid(_bdot(xcb, wa_ref[...]) + ba_ref[...])
    ig = _sigmoid(_bdot(xcb, wx_ref[...]) + bx_ref[...])
    lam = lam_ref[...]
    softplus_neg_lam = jnp.maximum(-lam, 0.0) + jnp.log(1.0 + jnp.exp(-jnp.abs(lam)))
    log_a = -LRU_C * r * softplus_neg_lam
    a = jnp.exp(log_a)
    t = jnp.tanh(log_a)
    b = jnp.sqrt(-2.0 * t / (1.0 - t)) * (ig * xc)
    row = lax.broadcasted_iota(jnp.int32, a.shape, 0)
    k = 1
    while k < tm:
        valid = row >= k
        a_sh = pltpu.roll(a, k, 0)
        b_sh = pltpu.roll(b, k, 0)
        b = jnp.where(valid, b + a * b_sh, b)
        a = jnp.where(valid, a * a_sh, a)
        k *= 2
    hs = b + a * hc_ref[0:1]
    hc_ref[0:1] = hs[tm - 1:tm]
    y_ref[0, :, SGU_WIDTH:SGU_WIDTH + LRU_WIDTH] = (hs * _gelu(z[:, C_Y:C_Y + LRU_WIDTH])).astype(BF16)


def _mix_in(x, sh, sc, pre_g, p, tabs):
    B, S, D = x.shape
    tm = TM_MIX
    full = lambda a: pl.BlockSpec(a.shape, lambda b, s: (0,) * a.ndim)
    per_b = pl.BlockSpec((1, 1, D), lambda b, s: (b, 0, 0))
    tab = pl.BlockSpec((tm, LANE), lambda b, s: (s, 0))
    tab_t = pl.BlockSpec((HEAD_PAD, tm), lambda b, s: (0, s))
    assert tm == TK
    consts = [p["w_in"], p["qg"], p["w_q"], p["kvg"], p["w_k"], p["w_v"]]
    consts2 = [p["ln_g"], p["ln_b"], p["w_s"], p["b_s"], p["conv_w"], p["conv_b"], p["wa"], p["ba"],
               p["wx"], p["bx"], p["lam"]]
    qw = MLA_HEADS * HEAD_PAD
    return pl.pallas_call(
        _mix_in_kernel,
        grid=(B, S // tm),
        in_specs=[pl.BlockSpec((1, tm, D), lambda b, s: (b, s, 0)), per_b, per_b, full(pre_g)]
        + [full(a) for a in consts] + [tab_t, tab_t, tab] + [full(a) for a in consts2],
        out_specs=[
            pl.BlockSpec((1, qw, tm), lambda b, s: (b, 0, s)),
            pl.BlockSpec((1, tm, qw), lambda b, s: (b, s, 0)),
            pl.BlockSpec((1, 1, qw, tm), lambda b, s: (b, s, 0, 0)),
            pl.BlockSpec((1, tm, SGU_WIDTH + LRU_WIDTH), lambda b, s: (b, s, 0)),
        ],
        out_shape=[
            jax.ShapeDtypeStruct((B, qw, S), BF16),
            jax.ShapeDtypeStruct((B, S, qw), BF16),
            jax.ShapeDtypeStruct((B, S // tm, qw, tm), BF16),
            jax.ShapeDtypeStruct((B, S, SGU_WIDTH + LRU_WIDTH), BF16),
        ],
        scratch_shapes=[pltpu.VMEM((SUBLANE, LRU_WIDTH), F32), pltpu.VMEM((SUBLANE, LRU_WIDTH), F32)],
        compiler_params=pltpu.CompilerParams(
            dimension_semantics=("arbitrary", "arbitrary"), vmem_limit_bytes=VMEM_LIMIT),
        name="mix_in",
    )(x, sh, sc, pre_g, *consts, tabs["cosq"], tabs["sinq"], tabs["tk"], *consts2)


def _attn_kernel(q_ref, k_ref, v_ref, o_ref, m_ref, acc_ref, p_ref, a_ref):
    i = pl.program_id(2)
    heads = q_ref.shape[1] // HEAD_PAD
    feat = lambda hd: slice(hd * HEAD_PAD, (hd + 1) * HEAD_PAD)

    m_ref[...] = jnp.full(m_ref.shape, NEG_INF, F32)
    acc_ref[...] = jnp.zeros(acc_ref.shape, F32)
    p_ref[...] = jnp.zeros(p_ref.shape, BF16)
    a_ref[...] = jnp.ones(a_ref.shape, F32)

    def scores(t, hd):
        keys = pl.ds(pl.multiple_of(t * TK, TK), TK)
        return _bdot(k_ref[0, keys, feat(hd)], q_ref[0, feat(hd), :])

    def weights(s, hd):
        m = m_ref[hd]
        m_new = jnp.maximum(m, jnp.max(s, axis=0, keepdims=True))
        p_ref[hd] = jnp.exp2(s - m_new).astype(BF16)
        a_ref[hd] = jnp.exp2(m - m_new)
        m_ref[hd] = m_new

    def accumulate(t, hd):
        acc_ref[hd] = a_ref[hd] * acc_ref[hd] + _bdot(v_ref[0, t, feat(hd), :], p_ref[hd])

    def tile(t, masked):
        ss = [scores(t, hd) for hd in range(heads)]
        for hd in range(heads):
            s = ss[hd]
            if masked:
                kc = lax.broadcasted_iota(jnp.int32, s.shape, 0) // CHUNK
                qc = lax.broadcasted_iota(jnp.int32, s.shape, 1) // CHUNK
                s = jnp.where(kc <= qc, s, NEG_INF)
            weights(s, hd)
            accumulate(t, hd)

    def body(t, c):
        tile(t, False)
        return c

    lax.fori_loop(0, i, body, 0)
    tile(i, True)
    outs = []
    for hd in range(heads):
        acc = acc_ref[hd]
        outs.append((acc / acc[V_DIM:V_DIM + 1]).T[:, :V_DIM])
    o_ref[0] = jnp.concatenate(outs, axis=-1).astype(BF16)


def _attention(q, k, v):
    B, S, _ = k.shape
    hp = ATTN_HEADS_PER_STEP
    assert TQ == TK
    return pl.pallas_call(
        _attn_kernel,
        grid=(B, MLA_HEADS // hp, S // TQ),
        in_specs=[
            pl.BlockSpec((1, hp * HEAD_PAD, TQ), lambda b, h, i: (b, h, i)),
            pl.BlockSpec((1, S, hp * HEAD_PAD), lambda b, h, i: (b, 0, h)),
            pl.BlockSpec((1, S // TK, hp * HEAD_PAD, TK), lambda b, h, i: (b, 0, h, 0)),
        ],
        out_specs=pl.BlockSpec((1, TQ, hp * V_DIM), lambda b, h, i: (b, i, h)),
        out_shape=jax.ShapeDtypeStruct((B, S, MLA_HEADS * V_DIM), BF16),
        scratch_shapes=[
            pltpu.VMEM((hp, 1, TQ), F32),
            pltpu.VMEM((hp, HEAD_PAD, TQ), F32),
            pltpu.VMEM((hp, TK, TQ), BF16),
            pltpu.VMEM((hp, 1, TQ), F32),
        ],
        compiler_params=pltpu.CompilerParams(
            dimension_semantics=("arbitrary", "arbitrary", "arbitrary"), vmem_limit_bytes=VMEM_LIMIT),
        name="attention",
    )(q, k, v)


def _mix_out_kernel(o_ref, y_ref, x_ref, wo_ref, postg_ref, gm_ref, preg_ref, scf_ref, shf_ref,
                    xo_ref, h_ref):
    no = o_ref.shape[1]
    y = _bdot(o_ref[...], wo_ref[0:no]) + _bdot(y_ref[...], wo_ref[no:])
    x = x_ref[...] + gm_ref[0] * _rms(y, postg_ref[...])
    xo_ref[...] = x
    h_ref[...] = (_rms(x, preg_ref[...]) * (1.0 + scf_ref[0]) + shf_ref[0]).astype(BF16)


def _mix_out_moe_kernel(o_ref, y_ref, x_ref, wo_ref, postg_ref, gm_ref, preg_ref, scf_ref, shf_ref,
                        router_ref, xo_ref, h_ref, comb_ref):
    no = o_ref.shape[1]
    y = _bdot(o_ref[...], wo_ref[0:no]) + _bdot(y_ref[...], wo_ref[no:])
    x = x_ref[...] + gm_ref[0] * _rms(y, postg_ref[...])
    xo_ref[...] = x
    h = _rms(x, preg_ref[...]) * (1.0 + scf_ref[0]) + shf_ref[0]
    h_ref[...] = h.astype(BF16)
    logits = jnp.dot(h, router_ref[...], preferred_element_type=F32, precision=HIGHEST)
    lane = lax.broadcasted_iota(jnp.int32, logits.shape, 1)
    logits = jnp.where(lane < N_EXPERTS, logits, -jnp.inf)
    m1 = jnp.max(logits, axis=-1, keepdims=True)
    i1 = jnp.min(jnp.where(logits == m1, lane, LANE), axis=-1, keepdims=True)
    rest = jnp.where(lane == i1, -jnp.inf, logits)
    m2 = jnp.max(rest, axis=-1, keepdims=True)
    i2 = jnp.min(jnp.where(rest == m2, lane, LANE), axis=-1, keepdims=True)
    e = jnp.exp(m2 - m1)
    g1 = 1.0 / (1.0 + e)
    comb_ref[...] = jnp.where(lane == i1, g1, 0.0) + jnp.where(lane == i2, e * g1, 0.0)


def _mix_out(o, y, x, w_o, post_g, g_m, pre_g, sc_f, sh_f, router, tiles_per_batch):
    N, D = x.shape
    tm = TM_OUT
    row = lambda w: pl.BlockSpec((tm, w), lambda i: (i, 0))
    full = lambda a: pl.BlockSpec(a.shape, lambda i: (0,) * a.ndim)
    per_b = pl.BlockSpec((1, 1, D), lambda i: (i // tiles_per_batch, 0, 0))
    in_specs = [row(o.shape[1]), row(y.shape[1]), row(D), full(w_o), full(post_g), per_b, full(pre_g), per_b, per_b]
    out_specs = [row(D), row(D)]
    out_shape = [jax.ShapeDtypeStruct((N, D), F32), jax.ShapeDtypeStruct((N, D), BF16)]
    args = [o, y, x, w_o, post_g, g_m, pre_g, sc_f, sh_f]
    body = _mix_out_kernel
    if router is not None:
        in_specs.append(full(router))
        out_specs.append(row(LANE))
        out_shape.append(jax.ShapeDtypeStruct((N, LANE), F32))
        args.append(router)
        body = _mix_out_moe_kernel
    return pl.pallas_call(
        body,
        grid=(N // tm,),
        in_specs=in_specs,
        out_specs=out_specs,
        out_shape=out_shape,
        compiler_params=pltpu.CompilerParams(dimension_semantics=("arbitrary",), vmem_limit_bytes=VMEM_LIMIT),
        name="mix_out",
    )(*args)


def _swiglu(h, wg, wu, wd):
    g = _bdot(h, wg)
    u = _bdot(h, wu)
    return _bdot((_silu(g) * u).astype(BF16), wd)


def _ffn_kernel(h_ref, x_ref, wg_ref, wu_ref, wd_ref, postg_ref, gf_ref, o_ref, acc_ref):
    j = pl.program_id(1)

    @pl.when(j == 0)
    def _():
        acc_ref[...] = jnp.zeros_like(acc_ref)

    acc_ref[...] += _swiglu(h_ref[...], wg_ref[...], wu_ref[...], wd_ref[...])

    @pl.when(j == pl.num_programs(1) - 1)
    def _():
        o_ref[...] = x_ref[...] + gf_ref[0] * _rms(acc_ref[...], postg_ref[...])


def _ffn(h, x, wg, wu, wd, post_g, g_f, tiles_per_batch):
    N, D = x.shape
    FF = wg.shape[1]
    tm = TM_FFN
    tf = FF // 2
    return pl.pallas_call(
        _ffn_kernel,
        grid=(N // tm, FF // tf),
        in_specs=[
            pl.BlockSpec((tm, D), lambda i, j: (i, 0)),
            pl.BlockSpec((tm, D), lambda i, j: (i, 0)),
            pl.BlockSpec((D, tf), lambda i, j: (0, j)),
            pl.BlockSpec((D, tf), lambda i, j: (0, j)),
            pl.BlockSpec((tf, D), lambda i, j: (j, 0)),
            pl.BlockSpec((1, D), lambda i, j: (0, 0)),
            pl.BlockSpec((1, 1, D), lambda i, j: (i // tiles_per_batch, 0, 0)),
        ],
        out_specs=pl.BlockSpec((tm, D), lambda i, j: (i, 0)),
        out_shape=jax.ShapeDtypeStruct((N, D), F32),
        scratch_shapes=[pltpu.VMEM((tm, D), F32)],
        compiler_params=pltpu.CompilerParams(
            dimension_semantics=("arbitrary", "arbitrary"), vmem_limit_bytes=VMEM_LIMIT),
        name="ffn_dense",
    )(h, x, wg, wu, wd, post_g, g_f)


def _moe_kernel(h_ref, x_ref, comb_ref, wg_ref, wu_ref, wd_ref, postg_ref, gf_ref, o_ref, acc_ref):
    e = pl.program_id(1)

    @pl.when(e == 0)
    def _():
        acc_ref[...] = jnp.zeros_like(acc_ref)

    comb = comb_ref[...]
    lane = lax.broadcasted_iota(jnp.int32, comb.shape, 1)
    w = jnp.sum(jnp.where(lane == e, comb, 0.0), axis=-1, keepdims=True)
    acc_ref[...] += w * _swiglu(h_ref[...], wg_ref[0], wu_ref[0], wd_ref[0])

    @pl.when(e == pl.num_programs(1) - 1)
    def _():
        o_ref[...] = x_ref[...] + gf_ref[0] * _rms(acc_ref[...], postg_ref[...])


def _moe(h, x, comb, wg, wu, wd, post_g, g_f, tiles_per_batch):
    N, D = x.shape
    E, _, FF = wg.shape
    tm = TM_FFN
    return pl.pallas_call(
        _moe_kernel,
        grid=(N // tm, E),
        in_specs=[
            pl.BlockSpec((tm, D), lambda i, e: (i, 0)),
            pl.BlockSpec((tm, D), lambda i, e: (i, 0)),
            pl.BlockSpec((tm, LANE), lambda i, e: (i, 0)),
            pl.BlockSpec((1, D, FF), lambda i, e: (e, 0, 0)),
            pl.BlockSpec((1, D, FF), lambda i, e: (e, 0, 0)),
            pl.BlockSpec((1, FF, D), lambda i, e: (e, 0, 0)),
            pl.BlockSpec((1, D), lambda i, e: (0, 0)),
            pl.BlockSpec((1, 1, D), lambda i, e: (i // tiles_per_batch, 0, 0)),
        ],
        out_specs=pl.BlockSpec((tm, D), lambda i, e: (i, 0)),
        out_shape=jax.ShapeDtypeStruct((N, D), F32),
        scratch_shapes=[pltpu.VMEM((tm, D), F32)],
        compiler_params=pltpu.CompilerParams(
            dimension_semantics=("arbitrary", "arbitrary"), vmem_limit_bytes=VMEM_LIMIT),
        name="ffn_moe",
    )(h, x, comb, wg, wu, wd, post_g, g_f)


def _rope_partner(w):
    half = QK_ROPE // 2
    return jnp.concatenate([-w[..., half:], w[..., :half]], axis=-1)


def _rope_tables(S):
    pos = jnp.arange(S, dtype=F32)
    inv = ROPE_THETA ** (-jnp.arange(0, QK_ROPE, 2, dtype=F32) / QK_ROPE)
    ang = pos[:, None] * inv[None, :]
    cos, sin = jnp.cos(ang), jnp.sin(ang)
    cos2 = jnp.concatenate([cos, cos], axis=-1)
    sin2 = jnp.concatenate([sin, sin], axis=-1)
    scale = (QK_NOPE + QK_ROPE) ** -0.5 * 1.4426950408889634
    pad = HEAD_PAD - QK_NOPE - QK_ROPE
    cosq = jnp.concatenate([jnp.ones((S, QK_NOPE), F32), cos2, jnp.zeros((S, pad), F32)], axis=-1) * scale
    sinq = jnp.concatenate([jnp.zeros((S, QK_NOPE), F32), sin2, jnp.zeros((S, pad), F32)], axis=-1) * scale
    tk = jnp.concatenate([cos2, sin2, jnp.zeros((S, LANE - 2 * QK_ROPE), F32)], axis=-1)
    return {"cosq": cosq.T, "sinq": sinq.T, "tk": tk}


def _layer_params(l, w_in, q_norm_g, w_uq, kv_norm_g, w_ukv, sgu_ln_g, sgu_ln_b, sgu_w, sgu_b,
                  conv_w, conv_b, lru_wa, lru_ba, lru_wx, lru_bx, lru_lambda):
    D = w_in.shape[1]
    wi = w_in[l]
    o_q, o_kv = 0, Q_LORA
    o_kr = o_kv + KV_LORA
    o_u = o_kr + QK_ROPE
    o_v = o_u + SGU_WIDTH
    o_x = o_v + SGU_WIDTH
    o_y = o_x + LRU_WIDTH
    w_kr = wi[:, o_kr:o_kr + QK_ROPE]
    w_in_p = jnp.concatenate([
        wi[:, o_q:o_q + Q_LORA], wi[:, o_kv:o_kv + KV_LORA], wi[:, o_u:o_u + SGU_WIDTH],
        wi[:, o_v:o_v + SGU_WIDTH], wi[:, o_x:o_x + LRU_WIDTH], wi[:, o_y:o_y + LRU_WIDTH],
        w_kr, _rope_partner(w_kr), jnp.zeros((D, LANE - 2 * QK_ROPE), F32)], axis=-1).astype(BF16)

    H = MLA_HEADS
    pad = HEAD_PAD - QK_NOPE - QK_ROPE
    wq = w_uq[l].reshape(Q_LORA, H, QK_NOPE + QK_ROPE)
    zq = jnp.zeros((Q_LORA, H, pad), F32)
    wq_main = jnp.concatenate([wq, zq], axis=-1).reshape(Q_LORA, H * HEAD_PAD)
    wq_part = jnp.concatenate([jnp.zeros((Q_LORA, H, QK_NOPE), F32), _rope_partner(wq[..., QK_NOPE:]), zq],
                              axis=-1).reshape(Q_LORA, H * HEAD_PAD)
    w_q = jnp.concatenate([wq_main, wq_part], axis=-1).T.astype(BF16)

    wkv = w_ukv[l].reshape(KV_LORA, H, QK_NOPE + V_DIM)
    wk_lat = jnp.concatenate([wkv[..., :QK_NOPE], jnp.zeros((KV_LORA, H, HEAD_PAD - QK_NOPE), F32)],
                             axis=-1).reshape(KV_LORA, H * HEAD_PAD)
    eye = jnp.eye(QK_ROPE, dtype=F32)
    place = jnp.concatenate([jnp.zeros((QK_ROPE, QK_NOPE), F32), eye, jnp.zeros((QK_ROPE, pad), F32)], axis=-1)
    place = jnp.tile(place, (1, H))
    w_k = jnp.concatenate([wk_lat, place, place, jnp.zeros((LANE - 2 * QK_ROPE, H * HEAD_PAD), F32)],
                          axis=0).astype(BF16)
    w_v = jnp.concatenate([wkv[..., QK_NOPE:], jnp.zeros((KV_LORA, H, HEAD_PAD - V_DIM), F32)],
                          axis=-1).reshape(KV_LORA, H * HEAD_PAD).T.astype(BF16)

    def block_diag(w):
        out = jnp.zeros((LRU_WIDTH, LRU_WIDTH), F32)
        for hd in range(LRU_HEADS):
            s = hd * LRU_HEAD_DIM
            out = out.at[s:s + LRU_HEAD_DIM, s:s + LRU_HEAD_DIM].set(w[hd])
        return out.astype(BF16)

    row = lambda a: a[l].reshape(1, -1)
    return {
        "w_in": w_in_p, "qg": row(q_norm_g), "w_q": w_q, "kvg": row(kv_norm_g), "w_k": w_k, "w_v": w_v,
        "ln_g": row(sgu_ln_g), "ln_b": row(sgu_ln_b),
        "w_s": sgu_w[l].reshape(SGU_GROUPS * SGU_BLOCK, SGU_BLOCK),
        "b_s": jnp.repeat(sgu_b[l].T, SGU_GROUP_DIM, axis=1),
        "conv_w": conv_w[l], "conv_b": row(conv_b), "wa": block_diag(lru_wa[l]), "ba": row(lru_ba),
        "wx": block_diag(lru_wx[l]), "bx": row(lru_bx), "lam": row(lru_lambda),
    }


def kernel(x, c, w_mod, b_mod, pre_mix_g, post_mix_g, w_in, q_norm_g, w_uq, kv_norm_g, w_ukv, sgu_ln_g,
           sgu_ln_b, sgu_w, sgu_b, conv_w, conv_b, lru_wa, lru_ba, lru_wx, lru_bx, lru_lambda, w_o,
           pre_ffn_g, post_ffn_g, ffn_w_gate, ffn_w_up, ffn_w_down, moe_router, moe_w_gate, moe_w_up,
           moe_w_down):
    B, S, D = x.shape
    L = w_mod.shape[0]
    N = B * S
    tabs = _rope_tables(S)
    mod = _modulation(c, w_mod, b_mod)
    xf = x.reshape(N, D)
    for l in range(L):
        sh_m, sc_m, g_m, sh_f, sc_f, g_f = [m.reshape(B, 1, D) for m in jnp.split(mod[l], 6, axis=-1)]
        p = _layer_params(l, w_in, q_norm_g, w_uq, kv_norm_g, w_ukv, sgu_ln_g, sgu_ln_b, sgu_w, sgu_b,
                          conv_w, conv_b, lru_wa, lru_ba, lru_wx, lru_bx, lru_lambda)
        q, k, v, ymix = _mix_in(xf.reshape(B, S, D), sh_m, sc_m, pre_mix_g[l].reshape(1, D), p, tabs)
        o = _attention(q, k, v)
        moe = l % 2 == 1
        router = None
        if moe:
            router = jnp.pad(moe_router[l // 2], ((0, 0), (0, LANE - N_EXPERTS)))
        res = _mix_out(o.reshape(N, -1), ymix.reshape(N, -1), xf, w_o[l].astype(BF16),
                       post_mix_g[l].reshape(1, D), g_m, pre_ffn_g[l].reshape(1, D), sc_f, sh_f, router,
                       S // TM_OUT)
        post_g = post_ffn_g[l].reshape(1, D)
        if moe:
            xf, h2, comb = res
            xf = _moe(h2, xf, comb, moe_w_gate[l // 2].astype(BF16), moe_w_up[l // 2].astype(BF16),
                      moe_w_down[l // 2].astype(BF16), post_g, g_f, S // TM_FFN)
        else:
            xf, h2 = res
            xf = _ffn(h2, xf, ffn_w_gate[l // 2].astype(BF16), ffn_w_up[l // 2].astype(BF16),
                      ffn_w_down[l // 2].astype(BF16), post_g, g_f, S // TM_FFN)
    return xf.reshape(B, S, D)
```

```python
import functools

import jax
import jax.numpy as jnp
from jax import lax
from jax.experimental import pallas as pl
from jax.experimental.pallas import tpu as pltpu

F32 = jnp.float32
BF16 = jnp.bfloat16
HIGHEST = lax.Precision.HIGHEST

MLA_HEADS = 8
QK_NOPE = 64
QK_ROPE = 32
V_DIM = 64
Q_LORA = 256
KV_LORA = 128
ROPE_THETA = 10000.0
CHUNK = 64
SGU_GROUPS = 4
SGU_GROUP_DIM = 64
SGU_WIDTH = SGU_GROUPS * SGU_GROUP_DIM
SGU_BLOCK = 128
LRU_HEADS = 4
LRU_HEAD_DIM = 64
LRU_WIDTH = LRU_HEADS * LRU_HEAD_DIM
CONV_W = 4
LRU_C = 8.0
N_EXPERTS = 8
NEG_INF = -1e30
RMS_EPS = 1e-6
LN_EPS = 1e-5

LANE = 128
SUBLANE = 8
HEAD_PAD = LANE

C_Q = 0
C_KV = C_Q + Q_LORA
C_U = C_KV + KV_LORA
C_V = C_U + SGU_WIDTH
C_X = C_V + SGU_WIDTH
C_Y = C_X + LRU_WIDTH
C_KR = C_Y + LRU_WIDTH
D_IN_PAD = C_KR + LANE

VMEM_LIMIT = 56 * 1024 * 1024

TM_MIX = 512
TQ = 512
TK = 512
ATTN_HEADS_PER_STEP = 4
TM_OUT = 512
TM_FFN = 512
MOE_TILE = 512
MOE_DMA_BLOCK = 2048


def _rms(x, g):
    return x * lax.rsqrt(jnp.mean(x * x, axis=-1, keepdims=True) + RMS_EPS) * g


def _gelu(x):
    return 0.5 * x * (1.0 + jnp.tanh(0.7978845608028654 * (x + 0.044715 * (x * x * x))))


def _sigmoid(x):
    return 1.0 / (1.0 + jnp.exp(-x))


def _silu(x):
    return x * _sigmoid(x)


def _bdot(a, b):
    return jnp.dot(a, b, preferred_element_type=F32)


def _mod_kernel(c_ref, w_ref, b_ref, o_ref):
    c = c_ref[...]
    o_ref[0] = jnp.dot(_silu(c), w_ref[0], preferred_element_type=F32, precision=HIGHEST) + b_ref[0]


def _modulation(c, w_mod, b_mod):
    L, D, D6 = w_mod.shape
    B = c.shape[0]
    return pl.pallas_call(
        _mod_kernel,
        grid=(L, D6 // D),
        in_specs=[
            pl.BlockSpec((B, D), lambda l, j: (0, 0)),
            pl.BlockSpec((1, D, D), lambda l, j: (l, 0, j)),
            pl.BlockSpec((1, 1, D), lambda l, j: (l, 0, j)),
        ],
        out_specs=pl.BlockSpec((1, B, D), lambda l, j: (l, 0, j)),
        out_shape=jax.ShapeDtypeStruct((L, B, D6), F32),
        compiler_params=pltpu.CompilerParams(vmem_limit_bytes=VMEM_LIMIT),
        name="modulation",
    )(c, w_mod, b_mod.reshape(L, 1, D6))


def _shift_rows(cur, prev8, k):
    rolled = pltpu.roll(cur, k, 0)
    fix = pltpu.roll(prev8, k, 0)
    row = lax.broadcasted_iota(jnp.int32, fix.shape, 0)
    top = jnp.where(row < k, fix, rolled[:SUBLANE])
    return jnp.concatenate([top, rolled[SUBLANE:]], axis=0)


def _mix_in_kernel(x_ref, sh_ref, sc_ref, preg_ref, win_ref, qg_ref, wq_ref, kvg_ref, wk_ref, wv_ref,
                   cosq_ref, sinq_ref, tk_ref, lng_ref, lnb_ref, ws_ref, bs_ref,
                   cw_ref, cb_ref, wa_ref, ba_ref, wx_ref, bx_ref, lam_ref,
                   q_ref, k_ref, v_ref, y_ref, prev_ref, hc_ref):
    tm = x_ref.shape[1]

    @pl.when(pl.program_id(1) == 0)
    def _():
        prev_ref[...] = jnp.zeros_like(prev_ref)
        hc_ref[...] = jnp.zeros_like(hc_ref)

    x = x_ref[0]
    h = _rms(x, preg_ref[...]) * (1.0 + sc_ref[0]) + sh_ref[0]
    z = _bdot(h.astype(BF16), win_ref[...])

    nt = (((1,), (1,)), ((), ()))
    qn = _rms(z[:, C_Q:C_Q + Q_LORA], qg_ref[...]).astype(BF16)
    qq = lax.dot_general(wq_ref[...], qn, nt, preferred_element_type=F32)
    cosq = cosq_ref[...]
    sinq = sinq_ref[...]
    qw = MLA_HEADS * HEAD_PAD
    for hd in range(MLA_HEADS):
        lo = hd * HEAD_PAD
        qh = qq[lo:lo + HEAD_PAD] * cosq + qq[qw + lo:qw + lo + HEAD_PAD] * sinq
        q_ref[0, lo:lo + HEAD_PAD, :] = qh.astype(BF16)
    kvn = _rms(z[:, C_KV:C_KV + KV_LORA], kvg_ref[...]).astype(BF16)
    kr = (z[:, C_KR:C_KR + LANE] * tk_ref[...]).astype(BF16)
    k_in = jnp.concatenate([kvn, kr], axis=-1)
    k_ref[0] = _bdot(k_in, wk_ref[...]).astype(BF16)
    vv = lax.dot_general(wv_ref[...], kvn, nt, preferred_element_type=F32)
    ones_row = lax.broadcasted_iota(jnp.int32, vv.shape, 0) % HEAD_PAD == V_DIM
    v_ref[0, 0] = jnp.where(ones_row, 1.0, vv).astype(BF16)

    u = _gelu(z[:, C_U:C_U + SGU_WIDTH])
    gv = _gelu(z[:, C_V:C_V + SGU_WIDTH])
    mu = jnp.mean(gv, axis=-1, keepdims=True)
    var = jnp.mean(jnp.square(gv - mu), axis=-1, keepdims=True)
    vn = ((gv - mu) * lax.rsqrt(var + LN_EPS) * lng_ref[...] + lnb_ref[...]).astype(BF16)
    ws = ws_ref[...]
    r_t = lax.broadcasted_iota(jnp.int32, ws.shape, 0) % SGU_BLOCK
    c_s = lax.broadcasted_iota(jnp.int32, ws.shape, 1)
    ws = jnp.where(c_s <= r_t, ws, 0.0).astype(BF16)
    grp = lax.broadcasted_iota(jnp.int32, (SGU_BLOCK, SGU_WIDTH), 1) // SGU_GROUP_DIM
    for blk in range(tm // SGU_BLOCK):
        r0 = blk * SGU_BLOCK
        res = _bdot(ws, vn[r0:r0 + SGU_BLOCK])
        mixed = bs_ref[...]
        for g in range(SGU_GROUPS):
            mixed = mixed + jnp.where(grp == g, res[g * SGU_BLOCK:(g + 1) * SGU_BLOCK], 0.0)
        y_ref[0, r0:r0 + SGU_BLOCK, 0:SGU_WIDTH] = (u[r0:r0 + SGU_BLOCK] * mixed).astype(BF16)

    zx = z[:, C_X:C_X + LRU_WIDTH]
    prev8 = prev_ref[...]
    cw = cw_ref[...]
    xc = cb_ref[...] + zx * cw[CONV_W - 1:CONV_W]
    for k in range(1, CONV_W):
        xc = xc + _shift_rows(zx, prev8, k) * cw[CONV_W - 1 - k:CONV_W - k]
    prev_ref[...] = zx[tm - SUBLANE:]
    xcb = xc.astype(BF16)
    r = _sigmoid(_bdot(xcb, wa_ref[...]) + ba_ref[...])
    ig = _sigmoid(_bdot(xcb, wx_ref[...]) + bx_ref[...])
    lam = lam_ref[...]
    softplus_neg_lam = jnp.maximum(-lam, 0.0) + jnp.log(1.0 + jnp.exp(-jnp.abs(lam)))
    log_a = -LRU_C * r * softplus_neg_lam
    a = jnp.exp(log_a)
    t = jnp.tanh(log_a)
    b = jnp.sqrt(-2.0 * t / (1.0 - t)) * (ig * xc)
    row = lax.broadcasted_iota(jnp.int32, a.shape, 0)
    k = 1
    while k < tm:
        valid = row >= k
        a_sh = pltpu.roll(a, k, 0)
        b_sh = pltpu.roll(b, k, 0)
        b = jnp.where(valid, b + a * b_sh, b)
        a = jnp.where(valid, a * a_sh, a)
        k *= 2
    hs = b + a * hc_ref[0:1]
    hc_ref[0:1] = hs[tm - 1:tm]
    y_ref[0, :, SGU_WIDTH:SGU_WIDTH + LRU_WIDTH] = (hs * _gelu(z[:, C_Y:C_Y + LRU_WIDTH])).astype(BF16)


def _mix_in(x, sh, sc, pre_g, p, tabs):
    B, S, D = x.shape
    tm = TM_MIX
    full = lambda a: pl.BlockSpec(a.shape, lambda b, s: (0,) * a.ndim)
    per_b = pl.BlockSpec((1, 1, D), lambda b, s: (b, 0, 0))
    tab = pl.BlockSpec((tm, LANE), lambda b, s: (s, 0))
    tab_t = pl.BlockSpec((HEAD_PAD, tm), lambda b, s: (0, s))
    assert tm == TK
    consts = [p["w_in"], p["qg"], p["w_q"], p["kvg"], p["w_k"], p["w_v"]]
    consts2 = [p["ln_g"], p["ln_b"], p["w_s"], p["b_s"], p["conv_w"], p["conv_b"], p["wa"], p["ba"],
               p["wx"], p["bx"], p["lam"]]
    qw = MLA_HEADS * HEAD_PAD
    return pl.pallas_call(
        _mix_in_kernel,
        grid=(B, S // tm),
        in_specs=[pl.BlockSpec((1, tm, D), lambda b, s: (b, s, 0)), per_b, per_b, full(pre_g)]
        + [full(a) for a in consts] + [tab_t, tab_t, tab] + [full(a) for a in consts2],
        out_specs=[
            pl.BlockSpec((1, qw, tm), lambda b, s: (b, 0, s)),
            pl.BlockSpec((1, tm, qw), lambda b, s: (b, s, 0)),
            pl.BlockSpec((1, 1, qw, tm), lambda b, s: (b, s, 0, 0)),
            pl.BlockSpec((1, tm, SGU_WIDTH + LRU_WIDTH), lambda b, s: (b, s, 0)),
        ],
        out_shape=[
            jax.ShapeDtypeStruct((B, qw, S), BF16),
            jax.ShapeDtypeStruct((B, S, qw), BF16),
            jax.ShapeDtypeStruct((B, S // tm, qw, tm), BF16),
            jax.ShapeDtypeStruct((B, S, SGU_WIDTH + LRU_WIDTH), BF16),
        ],
        scratch_shapes=[pltpu.VMEM((SUBLANE, LRU_WIDTH), F32), pltpu.VMEM((SUBLANE, LRU_WIDTH), F32)],
        compiler_params=pltpu.CompilerParams(
            dimension_semantics=("arbitrary", "arbitrary"), vmem_limit_bytes=VMEM_LIMIT),
        name="mix_in",
    )(x, sh, sc, pre_g, *consts, tabs["cosq"], tabs["sinq"], tabs["tk"], *consts2)


def _attn_kernel(q_ref, k_ref, v_ref, o_ref, m_ref, acc_ref, p_ref, a_ref):
    i = pl.program_id(2)
    heads = q_ref.shape[1] // HEAD_PAD
    feat = lambda hd: slice(hd * HEAD_PAD, (hd + 1) * HEAD_PAD)

    m_ref[...] = jnp.full(m_ref.shape, NEG_INF, F32)
    acc_ref[...] = jnp.zeros(acc_ref.shape, F32)
    p_ref[...] = jnp.zeros(p_ref.shape, BF16)
    a_ref[...] = jnp.ones(a_ref.shape, F32)

    def scores(t, hd):
        keys = pl.ds(pl.multiple_of(t * TK, TK), TK)
        return _bdot(k_ref[0, keys, feat(hd)], q_ref[0, feat(hd), :])

    def weights(s, hd):
        m = m_ref[hd]
        m_new = jnp.maximum(m, jnp.max(s, axis=0, keepdims=True))
        p_ref[hd] = jnp.exp2(s - m_new).astype(BF16)
        a_ref[hd] = jnp.exp2(m - m_new)
        m_ref[hd] = m_new

    def accumulate(t, hd):
        acc_ref[hd] = a_ref[hd] * acc_ref[hd] + _bdot(v_ref[0, t, feat(hd), :], p_ref[hd])

    def tile(t, masked):
        ss = [scores(t, hd) for hd in range(heads)]
        for hd in range(heads):
            s = ss[hd]
            if masked:
                kc = lax.broadcasted_iota(jnp.int32, s.shape, 0) // CHUNK
                qc = lax.broadcasted_iota(jnp.int32, s.shape, 1) // CHUNK
                s = jnp.where(kc <= qc, s, NEG_INF)
            weights(s, hd)
            accumulate(t, hd)

    def body(t, c):
        tile(t, False)
        return c

    lax.fori_loop(0, i, body, 0)
    tile(i, True)
    outs = []
    for hd in range(heads):
        acc = acc_ref[hd]
        outs.append((acc / acc[V_DIM:V_DIM + 1]).T[:, :V_DIM])
    o_ref[0] = jnp.concatenate(outs, axis=-1).astype(BF16)


def _attention(q, k, v):
    B, S, _ = k.shape
    hp = ATTN_HEADS_PER_STEP
    assert TQ == TK
    return pl.pallas_call(
        _attn_kernel,
        grid=(B, MLA_HEADS // hp, S // TQ),
        in_specs=[
            pl.BlockSpec((1, hp * HEAD_PAD, TQ), lambda b, h, i: (b, h, i)),
            pl.BlockSpec((1, S, hp * HEAD_PAD), lambda b, h, i: (b, 0, h)),
            pl.BlockSpec((1, S // TK, hp * HEAD_PAD, TK), lambda b, h, i: (b, 0, h, 0)),
        ],
        out_specs=pl.BlockSpec((1, TQ, hp * V_DIM), lambda b, h, i: (b, i, h)),
        out_shape=jax.ShapeDtypeStruct((B, S, MLA_HEADS * V_DIM), BF16),
        scratch_shapes=[
            pltpu.VMEM((hp, 1, TQ), F32),
            pltpu.VMEM((hp, HEAD_PAD, TQ), F32),
            pltpu.VMEM((hp, TK, TQ), BF16),
            pltpu.VMEM((hp, 1, TQ), F32),
        ],
        compiler_params=pltpu.CompilerParams(
            dimension_semantics=("arbitrary", "arbitrary", "arbitrary"), vmem_limit_bytes=VMEM_LIMIT),
        name="attention",
    )(q, k, v)


def _mix_out_kernel(o_ref, y_ref, x_ref, wo_ref, postg_ref, gm_ref, preg_ref, scf_ref, shf_ref,
                    xo_ref, h_ref):
    no = o_ref.shape[1]
    y = _bdot(o_ref[...], wo_ref[0:no]) + _bdot(y_ref[...], wo_ref[no:])
    x = x_ref[...] + gm_ref[0] * _rms(y, postg_ref[...])
    xo_ref[...] = x
    h_ref[...] = (_rms(x, preg_ref[...]) * (1.0 + scf_ref[0]) + shf_ref[0]).astype(BF16)


def _pack_rows(y):
    w = y.shape[1] // 2
    bits = lambda a: lax.bitcast_convert_type(a.astype(BF16).astype(F32), jnp.uint32)
    return (bits(y[:, w:]) & jnp.uint32(0xFFFF0000)) | (bits(y[:, :w]) >> 16)


def _unpack_rows(words):
    lo = lax.bitcast_convert_type(words << 16, F32)
    hi = lax.bitcast_convert_type(words & jnp.uint32(0xFFFF0000), F32)
    return lo, hi


def _mix_out_moe_kernel(o_ref, y_ref, x_ref, wo_ref, postg_ref, gm_ref, preg_ref, scf_ref, shf_ref,
                        router_ref, xo_ref, h_ref, route_ref):
    no = o_ref.shape[1]
    y = _bdot(o_ref[...], wo_ref[0:no]) + _bdot(y_ref[...], wo_ref[no:])
    x = x_ref[...] + gm_ref[0] * _rms(y, postg_ref[...])
    xo_ref[...] = x
    h = _rms(x, preg_ref[...]) * (1.0 + scf_ref[0]) + shf_ref[0]
    h_ref[...] = _pack_rows(h)
    logits = jnp.dot(h, router_ref[...], preferred_element_type=F32, precision=HIGHEST)
    lane = lax.broadcasted_iota(jnp.int32, logits.shape, 1)
    logits = jnp.where(lane < N_EXPERTS, logits, -jnp.inf)
    m1 = jnp.max(logits, axis=-1, keepdims=True)
    i1 = jnp.min(jnp.where(logits == m1, lane, LANE), axis=-1, keepdims=True)
    rest = jnp.where(lane == i1, -jnp.inf, logits)
    m2 = jnp.max(rest, axis=-1, keepdims=True)
    i2 = jnp.min(jnp.where(rest == m2, lane, LANE), axis=-1, keepdims=True)
    e = jnp.exp(m2 - m1)
    g1 = 1.0 / (1.0 + e)
    route_ref[...] = (jnp.where(lane == 0, i1.astype(F32), 0.0) + jnp.where(lane == 1, i2.astype(F32), 0.0)
                      + jnp.where(lane == 2, g1, 0.0) + jnp.where(lane == 3, e * g1, 0.0))


def _mix_out(o, y, x, w_o, post_g, g_m, pre_g, sc_f, sh_f, router, tiles_per_batch):
    N, D = x.shape
    tm = TM_OUT
    row = lambda w: pl.BlockSpec((tm, w), lambda i: (i, 0))
    full = lambda a: pl.BlockSpec(a.shape, lambda i: (0,) * a.ndim)
    per_b = pl.BlockSpec((1, 1, D), lambda i: (i // tiles_per_batch, 0, 0))
    in_specs = [row(o.shape[1]), row(y.shape[1]), row(D), full(w_o), full(post_g), per_b, full(pre_g), per_b, per_b]
    out_specs = [row(D), row(D)]
    out_shape = [jax.ShapeDtypeStruct((N, D), F32), jax.ShapeDtypeStruct((N, D), BF16)]
    args = [o, y, x, w_o, post_g, g_m, pre_g, sc_f, sh_f]
    body = _mix_out_kernel
    if router is not None:
        in_specs.append(full(router))
        out_specs = [row(D), row(D // 2), row(LANE)]
        out_shape = [jax.ShapeDtypeStruct((N, D), F32), jax.ShapeDtypeStruct((N, D // 2), jnp.uint32),
                     jax.ShapeDtypeStruct((N, LANE), F32)]
        args.append(router)
        body = _mix_out_moe_kernel
    return pl.pallas_call(
        body,
        grid=(N // tm,),
        in_specs=in_specs,
        out_specs=out_specs,
        out_shape=out_shape,
        compiler_params=pltpu.CompilerParams(dimension_semantics=("arbitrary",), vmem_limit_bytes=VMEM_LIMIT),
        name="mix_out",
    )(*args)


def _swiglu(h, wg, wu, wd):
    g = _bdot(h, wg)
    u = _bdot(h, wu)
    return _bdot((_silu(g) * u).astype(BF16), wd)


def _ffn_kernel(h_ref, x_ref, wg_ref, wu_ref, wd_ref, postg_ref, gf_ref, o_ref, acc_ref):
    j = pl.program_id(1)

    @pl.when(j == 0)
    def _():
        acc_ref[...] = jnp.zeros_like(acc_ref)

    acc_ref[...] += _swiglu(h_ref[...], wg_ref[...], wu_ref[...], wd_ref[...])

    @pl.when(j == pl.num_programs(1) - 1)
    def _():
        o_ref[...] = x_ref[...] + gf_ref[0] * _rms(acc_ref[...], postg_ref[...])


def _ffn(h, x, wg, wu, wd, post_g, g_f, tiles_per_batch):
    N, D = x.shape
    FF = wg.shape[1]
    tm = TM_FFN
    tf = FF // 2
    return pl.pallas_call(
        _ffn_kernel,
        grid=(N // tm, FF // tf),
        in_specs=[
            pl.BlockSpec((tm, D), lambda i, j: (i, 0)),
            pl.BlockSpec((tm, D), lambda i, j: (i, 0)),
            pl.BlockSpec((D, tf), lambda i, j: (0, j)),
            pl.BlockSpec((D, tf), lambda i, j: (0, j)),
            pl.BlockSpec((tf, D), lambda i, j: (j, 0)),
            pl.BlockSpec((1, D), lambda i, j: (0, 0)),
            pl.BlockSpec((1, 1, D), lambda i, j: (i // tiles_per_batch, 0, 0)),
        ],
        out_specs=pl.BlockSpec((tm, D), lambda i, j: (i, 0)),
        out_shape=jax.ShapeDtypeStruct((N, D), F32),
        scratch_shapes=[pltpu.VMEM((tm, D), F32)],
        compiler_params=pltpu.CompilerParams(
            dimension_semantics=("arbitrary", "arbitrary"), vmem_limit_bytes=VMEM_LIMIT),
        name="ffn_dense",
    )(h, x, wg, wu, wd, post_g, g_f)


def _row_copy(src_hbm, dst, src_row, dst_row, sem):
    return pltpu.make_async_copy(src_hbm.at[pl.ds(src_row, 1)], dst.at[pl.ds(dst_row, 1)], sem)


def _moe_dispatch_kernel(src_ref, h_hbm, hs_hbm, sem):
    n = src_ref.shape[2]
    base = pl.program_id(0) * n

    def start(r, c):
        _row_copy(h_hbm, hs_hbm, src_ref[0, 0, r], base + r, sem).start()
        return c

    lax.fori_loop(0, n, start, 0, unroll=8)

    def wait(r, c):
        _row_copy(h_hbm, hs_hbm, 0, base + r, sem).wait()
        return c

    lax.fori_loop(0, n, wait, 0, unroll=8)


def _moe_dispatch(src, h_packed, n_slots):
    n = MOE_DMA_BLOCK
    return pl.pallas_call(
        _moe_dispatch_kernel,
        grid=(n_slots // n,),
        in_specs=[
            pl.BlockSpec((1, 1, n), lambda i: (i, 0, 0), memory_space=pltpu.SMEM),
            pl.BlockSpec(memory_space=pl.ANY),
        ],
        out_specs=pl.BlockSpec(memory_space=pl.ANY),
        out_shape=jax.ShapeDtypeStruct((n_slots, h_packed.shape[1]), jnp.uint32),
        scratch_shapes=[pltpu.SemaphoreType.DMA(())],
        compiler_params=pltpu.CompilerParams(dimension_semantics=("arbitrary",)),
        name="moe_dispatch",
    )(src.reshape(n_slots // n, 1, n), h_packed)


def _moe_expert_kernel(texp_ref, nused_ref, hs_ref, wg_ref, wu_ref, wd_ref, ys_ref):
    del texp_ref

    @pl.when(pl.program_id(0) < nused_ref[0])
    def _():
        lo, hi = _unpack_rows(hs_ref[...])
        lo, hi = lo.astype(BF16), hi.astype(BF16)
        half = lo.shape[1]
        g = _bdot(lo, wg_ref[0, :half]) + _bdot(hi, wg_ref[0, half:])
        u = _bdot(lo, wu_ref[0, :half]) + _bdot(hi, wu_ref[0, half:])
        ys_ref[...] = _pack_rows(_bdot((_silu(g) * u).astype(BF16), wd_ref[0]))

    @pl.when(pl.program_id(0) >= nused_ref[0])
    def _():
        ys_ref[...] = jnp.zeros(ys_ref.shape, jnp.uint32)


def _moe_experts(tile_expert, n_used, hs, wg, wu, wd):
    n_slots, half = hs.shape
    E, D, FF = wg.shape
    t = MOE_TILE
    return pl.pallas_call(
        _moe_expert_kernel,
        grid_spec=pltpu.PrefetchScalarGridSpec(
            num_scalar_prefetch=2,
            grid=(n_slots // t,),
            in_specs=[
                pl.BlockSpec((t, half), lambda i, te, nu: (i, 0)),
                pl.BlockSpec((1, D, FF), lambda i, te, nu: (te[i], 0, 0)),
                pl.BlockSpec((1, D, FF), lambda i, te, nu: (te[i], 0, 0)),
                pl.BlockSpec((1, FF, D), lambda i, te, nu: (te[i], 0, 0)),
            ],
            out_specs=pl.BlockSpec((t, half), lambda i, te, nu: (i, 0)),
        ),
        out_shape=jax.ShapeDtypeStruct((n_slots, half), jnp.uint32),
        compiler_params=pltpu.CompilerParams(dimension_semantics=("arbitrary",), vmem_limit_bytes=VMEM_LIMIT),
        name="moe_experts",
    )(tile_expert, n_used, hs, wg, wu, wd)


def _moe_combine_kernel(slot_ref, ys_hbm, route_ref, x_ref, postg_ref, gf_ref, o_ref, buf_ref, sem):
    tm = x_ref.shape[0]

    def start(r, c):
        for k in range(2):
            _row_copy(ys_hbm, buf_ref.at[k], slot_ref[0, 0, 2 * r + k], r, sem).start()
        return c

    lax.fori_loop(0, tm, start, 0, unroll=4)

    def wait(r, c):
        for k in range(2):
            _row_copy(ys_hbm, buf_ref.at[k], 0, r, sem).wait()
        return c

    lax.fori_loop(0, tm, wait, 0, unroll=4)
    route = route_ref[...]
    lane = lax.broadcasted_iota(jnp.int32, route.shape, 1)
    f = None
    for k in range(2):
        gate = jnp.sum(jnp.where(lane == 2 + k, route, 0.0), axis=-1, keepdims=True)
        y = jnp.concatenate(_unpack_rows(buf_ref[k]), axis=-1)
        f = gate * y if f is None else f + gate * y
    o_ref[...] = x_ref[...] + gf_ref[0] * _rms(f, postg_ref[...])


def _moe_combine(slots, ys, route, x, post_g, g_f, tiles_per_batch):
    N, D = x.shape
    tm = TM_FFN
    return pl.pallas_call(
        _moe_combine_kernel,
        grid=(N // tm,),
        in_specs=[
            pl.BlockSpec((1, 1, 2 * tm), lambda i: (i, 0, 0), memory_space=pltpu.SMEM),
            pl.BlockSpec(memory_space=pl.ANY),
            pl.BlockSpec((tm, LANE), lambda i: (i, 0)),
            pl.BlockSpec((tm, D), lambda i: (i, 0)),
            pl.BlockSpec((1, D), lambda i: (0, 0)),
            pl.BlockSpec((1, 1, D), lambda i: (i // tiles_per_batch, 0, 0)),
        ],
        out_specs=pl.BlockSpec((tm, D), lambda i: (i, 0)),
        out_shape=jax.ShapeDtypeStruct((N, D), F32),
        scratch_shapes=[pltpu.VMEM((2, tm, D // 2), jnp.uint32), pltpu.SemaphoreType.DMA(())],
        compiler_params=pltpu.CompilerParams(dimension_semantics=("arbitrary",), vmem_limit_bytes=VMEM_LIMIT),
        name="moe_combine",
    )(slots.reshape(N // tm, 1, 2 * tm), ys, route, x, post_g, g_f)


def _moe_plan(route, n_slots):
    t = MOE_TILE
    pair_expert = route[:, :2].astype(jnp.int32).reshape(-1)
    onehot = (pair_expert[:, None] == jnp.arange(N_EXPERTS, dtype=jnp.int32)[None, :]).astype(jnp.int32)
    csum = jnp.cumsum(onehot, axis=0)
    rank = jnp.take_along_axis(csum, pair_expert[:, None], axis=1)[:, 0] - 1
    padded = (csum[-1] + t - 1) // t * t
    ends = jnp.cumsum(padded)
    slot = (ends - padded)[pair_expert] + rank
    src = jnp.zeros((n_slots,), jnp.int32).at[slot].set(jnp.arange(slot.shape[0], dtype=jnp.int32) // 2)
    n_used = ends[-1:] // t
    tile_start = jnp.arange(n_slots // t, dtype=jnp.int32) * t
    tile_expert = jnp.sum((tile_start[:, None] >= ends[None, :]).astype(jnp.int32), axis=1)
    last_used = tile_expert[jnp.maximum(n_used[0] - 1, 0)]
    tile_expert = jnp.where(tile_start < ends[-1], tile_expert, last_used)
    return slot, src, tile_expert, n_used.astype(jnp.int32)


def _moe(h_packed, x, route, wg, wu, wd, post_g, g_f, tiles_per_batch):
    N = x.shape[0]
    n_slots = 2 * N + N_EXPERTS * MOE_TILE
    slot, src, tile_expert, n_used = _moe_plan(route, n_slots)
    hs = _moe_dispatch(src, h_packed, n_slots)
    ys = _moe_experts(tile_expert, n_used, hs, wg, wu, wd)
    return _moe_combine(slot, ys, route, x, post_g, g_f, tiles_per_batch)


def _rope_partner(w):
    half = QK_ROPE // 2
    return jnp.concatenate([-w[..., half:], w[..., :half]], axis=-1)


def _rope_tables(S):
    pos = jnp.arange(S, dtype=F32)
    inv = ROPE_THETA ** (-jnp.arange(0, QK_ROPE, 2, dtype=F32) / QK_ROPE)
    ang = pos[:, None] * inv[None, :]
    cos, sin = jnp.cos(ang), jnp.sin(ang)
    cos2 = jnp.concatenate([cos, cos], axis=-1)
    sin2 = jnp.concatenate([sin, sin], axis=-1)
    scale = (QK_NOPE + QK_ROPE) ** -0.5 * 1.4426950408889634
    pad = HEAD_PAD - QK_NOPE - QK_ROPE
    cosq = jnp.concatenate([jnp.ones((S, QK_NOPE), F32), cos2, jnp.zeros((S, pad), F32)], axis=-1) * scale
    sinq = jnp.concatenate([jnp.zeros((S, QK_NOPE), F32), sin2, jnp.zeros((S, pad), F32)], axis=-1) * scale
    tk = jnp.concatenate([cos2, sin2, jnp.zeros((S, LANE - 2 * QK_ROPE), F32)], axis=-1)
    return {"cosq": cosq.T, "sinq": sinq.T, "tk": tk}


def _layer_params(l, w_in, q_norm_g, w_uq, kv_norm_g, w_ukv, sgu_ln_g, sgu_ln_b, sgu_w, sgu_b,
                  conv_w, conv_b, lru_wa, lru_ba, lru_wx, lru_bx, lru_lambda):
    D = w_in.shape[1]
    wi = w_in[l]
    o_q, o_kv = 0, Q_LORA
    o_kr = o_kv + KV_LORA
    o_u = o_kr + QK_ROPE
    o_v = o_u + SGU_WIDTH
    o_x = o_v + SGU_WIDTH
    o_y = o_x + LRU_WIDTH
    w_kr = wi[:, o_kr:o_kr + QK_ROPE]
    w_in_p = jnp.concatenate([
        wi[:, o_q:o_q + Q_LORA], wi[:, o_kv:o_kv + KV_LORA], wi[:, o_u:o_u + SGU_WIDTH],
        wi[:, o_v:o_v + SGU_WIDTH], wi[:, o_x:o_x + LRU_WIDTH], wi[:, o_y:o_y + LRU_WIDTH],
        w_kr, _rope_partner(w_kr), jnp.zeros((D, LANE - 2 * QK_ROPE), F32)], axis=-1).astype(BF16)

    H = MLA_HEADS
    pad = HEAD_PAD - QK_NOPE - QK_ROPE
    wq = w_uq[l].reshape(Q_LORA, H, QK_NOPE + QK_ROPE)
    zq = jnp.zeros((Q_LORA, H, pad), F32)
    wq_main = jnp.concatenate([wq, zq], axis=-1).reshape(Q_LORA, H * HEAD_PAD)
    wq_part = jnp.concatenate([jnp.zeros((Q_LORA, H, QK_NOPE), F32), _rope_partner(wq[..., QK_NOPE:]), zq],
                              axis=-1).reshape(Q_LORA, H * HEAD_PAD)
    w_q = jnp.concatenate([wq_main, wq_part], axis=-1).T.astype(BF16)

    wkv = w_ukv[l].reshape(KV_LORA, H, QK_NOPE + V_DIM)
    wk_lat = jnp.concatenate([wkv[..., :QK_NOPE], jnp.zeros((KV_LORA, H, HEAD_PAD - QK_NOPE), F32)],
                             axis=-1).reshape(KV_LORA, H * HEAD_PAD)
    eye = jnp.eye(QK_ROPE, dtype=F32)
    place = jnp.concatenate([jnp.zeros((QK_ROPE, QK_NOPE), F32), eye, jnp.zeros((QK_ROPE, pad), F32)], axis=-1)
    place = jnp.tile(place, (1, H))
    w_k = jnp.concatenate([wk_lat, place, place, jnp.zeros((LANE - 2 * QK_ROPE, H * HEAD_PAD), F32)],
                          axis=0).astype(BF16)
    w_v = jnp.concatenate([wkv[..., QK_NOPE:], jnp.zeros((KV_LORA, H, HEAD_PAD - V_DIM), F32)],
                          axis=-1).reshape(KV_LORA, H * HEAD_PAD).T.astype(BF16)

    def block_diag(w):
        out = jnp.zeros((LRU_WIDTH, LRU_WIDTH), F32)
        for hd in range(LRU_HEADS):
            s = hd * LRU_HEAD_DIM
            out = out.at[s:s + LRU_HEAD_DIM, s:s + LRU_HEAD_DIM].set(w[hd])
        return out.astype(BF16)

    row = lambda a: a[l].reshape(1, -1)
    return {
        "w_in": w_in_p, "qg": row(q_norm_g), "w_q": w_q, "kvg": row(kv_norm_g), "w_k": w_k, "w_v": w_v,
        "ln_g": row(sgu_ln_g), "ln_b": row(sgu_ln_b),
        "w_s": sgu_w[l].reshape(SGU_GROUPS * SGU_BLOCK, SGU_BLOCK),
        "b_s": jnp.repeat(sgu_b[l].T, SGU_GROUP_DIM, axis=1),
        "conv_w": conv_w[l], "conv_b": row(conv_b), "wa": block_diag(lru_wa[l]), "ba": row(lru_ba),
        "wx": block_diag(lru_wx[l]), "bx": row(lru_bx), "lam": row(lru_lambda),
    }


def kernel(x, c, w_mod, b_mod, pre_mix_g, post_mix_g, w_in, q_norm_g, w_uq, kv_norm_g, w_ukv, sgu_ln_g,
           sgu_ln_b, sgu_w, sgu_b, conv_w, conv_b, lru_wa, lru_ba, lru_wx, lru_bx, lru_lambda, w_o,
           pre_ffn_g, post_ffn_g, ffn_w_gate, ffn_w_up, ffn_w_down, moe_router, moe_w_gate, moe_w_up,
           moe_w_down):
    B, S, D = x.shape
    L = w_mod.shape[0]
    N = B * S
    tabs = _rope_tables(S)
    mod = _modulation(c, w_mod, b_mod)
    xf = x.reshape(N, D)
    for l in range(L):
        sh_m, sc_m, g_m, sh_f, sc_f, g_f = [m.reshape(B, 1, D) for m in jnp.split(mod[l], 6, axis=-1)]
        p = _layer_params(l, w_in, q_norm_g, w_uq, kv_norm_g, w_ukv, sgu_ln_g, sgu_ln_b, sgu_w, sgu_b,
                          conv_w, conv_b, lru_wa, lru_ba, lru_wx, lru_bx, lru_lambda)
        q, k, v, ymix = _mix_in(xf.reshape(B, S, D), sh_m, sc_m, pre_mix_g[l].reshape(1, D), p, tabs)
        o = _attention(q, k, v)
        moe = l % 2 == 1
        router = None
        if moe:
            router = jnp.pad(moe_router[l // 2], ((0, 0), (0, LANE - N_EXPERTS)))
        res = _mix_out(o.reshape(N, -1), ymix.reshape(N, -1), xf, w_o[l].astype(BF16),
                       post_mix_g[l].reshape(1, D), g_m, pre_ffn_g[l].reshape(1, D), sc_f, sh_f, router,
                       S // TM_OUT)
        post_g = post_ffn_g[l].reshape(1, D)
        if moe:
            xf, h2, comb = res
            xf = _moe(h2, xf, comb, moe_w_gate[l // 2].astype(BF16), moe_w_up[l // 2].astype(BF16),
                      moe_w_down[l // 2].astype(BF16), post_g, g_f, S // TM_FFN)
        else:
            xf, h2 = res
            xf = _ffn(h2, xf, ffn_w_gate[l // 2].astype(BF16), ffn_w_up[l // 2].astype(BF16),
                      ffn_w_down[l // 2].astype(BF16), post_g, g_f, S // TM_FFN)
    return xf.reshape(B, S, D)
```

```python
import functools

import jax
import jax.numpy as jnp
from jax import lax
from jax.experimental import pallas as pl
from jax.experimental.pallas import tpu as pltpu

F32 = jnp.float32
BF16 = jnp.bfloat16
HIGHEST = lax.Precision.HIGHEST

MLA_HEADS = 8
QK_NOPE = 64
QK_ROPE = 32
V_DIM = 64
Q_LORA = 256
KV_LORA = 128
ROPE_THETA = 10000.0
CHUNK = 64
SGU_GROUPS = 4
SGU_GROUP_DIM = 64
SGU_WIDTH = SGU_GROUPS * SGU_GROUP_DIM
SGU_BLOCK = 128
LRU_HEADS = 4
LRU_HEAD_DIM = 64
LRU_WIDTH = LRU_HEADS * LRU_HEAD_DIM
CONV_W = 4
LRU_C = 8.0
N_EXPERTS = 8
NEG_INF = -1e30
RMS_EPS = 1e-6
LN_EPS = 1e-5

LANE = 128
SUBLANE = 8
HEAD_PAD = LANE

C_Q = 0
C_KV = C_Q + Q_LORA
C_U = C_KV + KV_LORA
C_V = C_U + SGU_WIDTH
C_X = C_V + SGU_WIDTH
C_Y = C_X + LRU_WIDTH
C_KR = C_Y + LRU_WIDTH
D_IN_PAD = C_KR + LANE

VMEM_LIMIT = 56 * 1024 * 1024

TM_MIX = 512
TQ = 512
TK = 512
ATTN_HEADS_PER_STEP = 4
TM_OUT = 512
TM_FFN = 512
MOE_TILE = 512
MOE_DMA_BLOCK = 2048


def _rms(x, g):
    return x * lax.rsqrt(jnp.mean(x * x, axis=-1, keepdims=True) + RMS_EPS) * g


def _gelu(x):
    return 0.5 * x * (1.0 + jnp.tanh(0.7978845608028654 * (x + 0.044715 * (x * x * x))))


def _sigmoid(x):
    return 1.0 / (1.0 + jnp.exp(-x))


def _silu(x):
    return x * _sigmoid(x)


def _bdot(a, b):
    return jnp.dot(a, b, preferred_element_type=F32)


def _mod_kernel(c_ref, w_ref, b_ref, o_ref):
    c = c_ref[...]
    o_ref[0] = jnp.dot(_silu(c), w_ref[0], preferred_element_type=F32, precision=HIGHEST) + b_ref[0]


def _modulation(c, w_mod, b_mod):
    L, D, D6 = w_mod.shape
    B = c.shape[0]
    return pl.pallas_call(
        _mod_kernel,
        grid=(L, D6 // D),
        in_specs=[
            pl.BlockSpec((B, D), lambda l, j: (0, 0)),
            pl.BlockSpec((1, D, D), lambda l, j: (l, 0, j)),
            pl.BlockSpec((1, 1, D), lambda l, j: (l, 0, j)),
        ],
        out_specs=pl.BlockSpec((1, B, D), lambda l, j: (l, 0, j)),
        out_shape=jax.ShapeDtypeStruct((L, B, D6), F32),
        compiler_params=pltpu.CompilerParams(vmem_limit_bytes=VMEM_LIMIT),
        name="modulation",
    )(c, w_mod, b_mod.reshape(L, 1, D6))


def _shift_rows(cur, prev8, k):
    rolled = pltpu.roll(cur, k, 0)
    fix = pltpu.roll(prev8, k, 0)
    row = lax.broadcasted_iota(jnp.int32, fix.shape, 0)
    top = jnp.where(row < k, fix, rolled[:SUBLANE])
    return jnp.concatenate([top, rolled[SUBLANE:]], axis=0)


def _mix_in_kernel(x_ref, sh_ref, sc_ref, preg_ref, win_ref, qg_ref, wq_ref, kvg_ref, wk_ref, wv_ref,
                   cosq_ref, sinq_ref, tk_ref, lng_ref, lnb_ref, ws_ref, bs_ref,
                   cw_ref, cb_ref, wa_ref, ba_ref, wx_ref, bx_ref, lam_ref,
                   q_ref, k_ref, v_ref, y_ref, prev_ref, hc_ref):
    tm = x_ref.shape[1]

    @pl.when(pl.program_id(1) == 0)
    def _():
        prev_ref[...] = jnp.zeros_like(prev_ref)
        hc_ref[...] = jnp.zeros_like(hc_ref)

    x = x_ref[0]
    h = _rms(x, preg_ref[...]) * (1.0 + sc_ref[0]) + sh_ref[0]
    z = _bdot(h.astype(BF16), win_ref[...])

    nt = (((1,), (1,)), ((), ()))
    qn = _rms(z[:, C_Q:C_Q + Q_LORA], qg_ref[...]).astype(BF16)
    qq = lax.dot_general(wq_ref[...], qn, nt, preferred_element_type=F32)
    cosq = cosq_ref[...]
    sinq = sinq_ref[...]
    qw = MLA_HEADS * HEAD_PAD
    for hd in range(MLA_HEADS):
        lo = hd * HEAD_PAD
        qh = qq[lo:lo + HEAD_PAD] * cosq + qq[qw + lo:qw + lo + HEAD_PAD] * sinq
        q_ref[0, lo:lo + HEAD_PAD, :] = qh.astype(BF16)
    kvn = _rms(z[:, C_KV:C_KV + KV_LORA], kvg_ref[...]).astype(BF16)
    kr = (z[:, C_KR:C_KR + LANE] * tk_ref[...]).astype(BF16)
    k_in = jnp.concatenate([kvn, kr], axis=-1)
    k_ref[0] = _bdot(k_in, wk_ref[...]).astype(BF16)
    vv = lax.dot_general(wv_ref[...], kvn, nt, preferred_element_type=F32)
    ones_row = lax.broadcasted_iota(jnp.int32, vv.shape, 0) % HEAD_PAD == V_DIM
    v_ref[0, 0] = jnp.where(ones_row, 1.0, vv).astype(BF16)

    u = _gelu(z[:, C_U:C_U + SGU_WIDTH])
    gv = _gelu(z[:, C_V:C_V + SGU_WIDTH])
    mu = jnp.mean(gv, axis=-1, keepdims=True)
    var = jnp.mean(jnp.square(gv - mu), axis=-1, keepdims=True)
    vn = ((gv - mu) * lax.rsqrt(var + LN_EPS) * lng_ref[...] + lnb_ref[...]).astype(BF16)
    ws = ws_ref[...]
    r_t = lax.broadcasted_iota(jnp.int32, ws.shape, 0) % SGU_BLOCK
    c_s = lax.broadcasted_iota(jnp.int32, ws.shape, 1)
    ws = jnp.where(c_s <= r_t, ws, 0.0).astype(BF16)
    grp = lax.broadcasted_iota(jnp.int32, (SGU_BLOCK, SGU_WIDTH), 1) // SGU_GROUP_DIM
    for blk in range(tm // SGU_BLOCK):
        r0 = blk * SGU_BLOCK
        res = _bdot(ws, vn[r0:r0 + SGU_BLOCK])
        mixed = bs_ref[...]
        for g in range(SGU_GROUPS):
            mixed = mixed + jnp.where(grp == g, res[g * SGU_BLOCK:(g + 1) * SGU_BLOCK], 0.0)
        y_ref[0, r0:r0 + SGU_BLOCK, 0:SGU_WIDTH] = (u[r0:r0 + SGU_BLOCK] * mixed).astype(BF16)

    zx = z[:, C_X:C_X + LRU_WIDTH]
    prev8 = prev_ref[...]
    cw = cw_ref[...]
    xc = cb_ref[...] + zx * cw[CONV_W - 1:CONV_W]
    for k in range(1, CONV_W):
        xc = xc + _shift_rows(zx, prev8, k) * cw[CONV_W - 1 - k:CONV_W - k]
    prev_ref[...] = zx[tm - SUBLANE:]
    xcb = xc.astype(BF16)
    r = _sigmoid(_bdot(xcb, wa_ref[...]) + ba_ref[...])
    ig = _sigmoid(_bdot(xcb, wx_ref[...]) + bx_ref[...])
    lam = lam_ref[...]
    softplus_neg_lam = jnp.maximum(-lam, 0.0) + jnp.log(1.0 + jnp.exp(-jnp.abs(lam)))
    log_a = -LRU_C * r * softplus_neg_lam
    a = jnp.exp(log_a)
    t = jnp.tanh(log_a)
    b = jnp.sqrt(-2.0 * t / (1.0 - t)) * (ig * xc)
    row = lax.broadcasted_iota(jnp.int32, a.shape, 0)
    k = 1
    while k < tm:
        valid = row >= k
        a_sh = pltpu.roll(a, k, 0)
        b_sh = pltpu.roll(b, k, 0)
        b = jnp.where(valid, b + a * b_sh, b)
        a = jnp.where(valid, a * a_sh, a)
        k *= 2
    hs = b + a * hc_ref[0:1]
    hc_ref[0:1] = hs[tm - 1:tm]
    y_ref[0, :, SGU_WIDTH:SGU_WIDTH + LRU_WIDTH] = (hs * _gelu(z[:, C_Y:C_Y + LRU_WIDTH])).astype(BF16)


def _mix_in(x, sh, sc, pre_g, p, tabs):
    B, S, D = x.shape
    tm = TM_MIX
    full = lambda a: pl.BlockSpec(a.shape, lambda b, s: (0,) * a.ndim)
    per_b = pl.BlockSpec((1, 1, D), lambda b, s: (b, 0, 0))
    tab = pl.BlockSpec((tm, LANE), lambda b, s: (s, 0))
    tab_t = pl.BlockSpec((HEAD_PAD, tm), lambda b, s: (0, s))
    assert tm == TK
    consts = [p["w_in"], p["qg"], p["w_q"], p["kvg"], p["w_k"], p["w_v"]]
    consts2 = [p["ln_g"], p["ln_b"], p["w_s"], p["b_s"], p["conv_w"], p["conv_b"], p["wa"], p["ba"],
               p["wx"], p["bx"], p["lam"]]
    qw = MLA_HEADS * HEAD_PAD
    return pl.pallas_call(
        _mix_in_kernel,
        grid=(B, S // tm),
        in_specs=[pl.BlockSpec((1, tm, D), lambda b, s: (b, s, 0)), per_b, per_b, full(pre_g)]
        + [full(a) for a in consts] + [tab_t, tab_t, tab] + [full(a) for a in consts2],
        out_specs=[
            pl.BlockSpec((1, qw, tm), lambda b, s: (b, 0, s)),
            pl.BlockSpec((1, tm, qw), lambda b, s: (b, s, 0)),
            pl.BlockSpec((1, 1, qw, tm), lambda b, s: (b, s, 0, 0)),
            pl.BlockSpec((1, tm, SGU_WIDTH + LRU_WIDTH), lambda b, s: (b, s, 0)),
        ],
        out_shape=[
            jax.ShapeDtypeStruct((B, qw, S), BF16),
            jax.ShapeDtypeStruct((B, S, qw), BF16),
            jax.ShapeDtypeStruct((B, S // tm, qw, tm), BF16),
            jax.ShapeDtypeStruct((B, S, SGU_WIDTH + LRU_WIDTH), BF16),
        ],
        scratch_shapes=[pltpu.VMEM((SUBLANE, LRU_WIDTH), F32), pltpu.VMEM((SUBLANE, LRU_WIDTH), F32)],
        compiler_params=pltpu.CompilerParams(
            dimension_semantics=("arbitrary", "arbitrary"), vmem_limit_bytes=VMEM_LIMIT),
        name="mix_in",
    )(x, sh, sc, pre_g, *consts, tabs["cosq"], tabs["sinq"], tabs["tk"], *consts2)


def _attn_kernel(q_ref, k_ref, v_ref, o_ref, m_ref, acc_ref, p_ref, a_ref):
    i = pl.program_id(2)
    heads = q_ref.shape[1] // HEAD_PAD
    feat = lambda hd: slice(hd * HEAD_PAD, (hd + 1) * HEAD_PAD)

    m_ref[...] = jnp.full(m_ref.shape, NEG_INF, F32)
    acc_ref[...] = jnp.zeros(acc_ref.shape, F32)
    p_ref[...] = jnp.zeros(p_ref.shape, BF16)
    a_ref[...] = jnp.ones(a_ref.shape, F32)

    def scores(t, hd):
        keys = pl.ds(pl.multiple_of(t * TK, TK), TK)
        return _bdot(k_ref[0, keys, feat(hd)], q_ref[0, feat(hd), :])

    def weights(s, hd):
        m = m_ref[hd]
        m_new = jnp.maximum(m, jnp.max(s, axis=0, keepdims=True))
        p_ref[hd] = jnp.exp2(s - m_new).astype(BF16)
        a_ref[hd] = jnp.exp2(m - m_new)
        m_ref[hd] = m_new

    def accumulate(t, hd):
        acc_ref[hd] = a_ref[hd] * acc_ref[hd] + _bdot(v_ref[0, t, feat(hd), :], p_ref[hd])

    def tile(t, masked):
        ss = [scores(t, hd) for hd in range(heads)]
        for hd in range(heads):
            s = ss[hd]
            if masked:
                kc = lax.broadcasted_iota(jnp.int32, s.shape, 0) // CHUNK
                qc = lax.broadcasted_iota(jnp.int32, s.shape, 1) // CHUNK
                s = jnp.where(kc <= qc, s, NEG_INF)
            weights(s, hd)
            accumulate(t, hd)

    def body(t, c):
        tile(t, False)
        return c

    lax.fori_loop(0, i, body, 0)
    tile(i, True)
    outs = []
    for hd in range(heads):
        acc = acc_ref[hd]
        outs.append((acc / acc[V_DIM:V_DIM + 1]).T[:, :V_DIM])
    o_ref[0] = jnp.concatenate(outs, axis=-1).astype(BF16)


def _attention(q, k, v):
    B, S, _ = k.shape
    hp = ATTN_HEADS_PER_STEP
    assert TQ == TK
    return pl.pallas_call(
        _attn_kernel,
        grid=(B, MLA_HEADS // hp, S // TQ),
        in_specs=[
            pl.BlockSpec((1, hp * HEAD_PAD, TQ), lambda b, h, i: (b, h, i)),
            pl.BlockSpec((1, S, hp * HEAD_PAD), lambda b, h, i: (b, 0, h)),
            pl.BlockSpec((1, S // TK, hp * HEAD_PAD, TK), lambda b, h, i: (b, 0, h, 0)),
        ],
        out_specs=pl.BlockSpec((1, TQ, hp * V_DIM), lambda b, h, i: (b, i, h)),
        out_shape=jax.ShapeDtypeStruct((B, S, MLA_HEADS * V_DIM), BF16),
        scratch_shapes=[
            pltpu.VMEM((hp, 1, TQ), F32),
            pltpu.VMEM((hp, HEAD_PAD, TQ), F32),
            pltpu.VMEM((hp, TK, TQ), BF16),
            pltpu.VMEM((hp, 1, TQ), F32),
        ],
        compiler_params=pltpu.CompilerParams(
            dimension_semantics=("arbitrary", "arbitrary", "arbitrary"), vmem_limit_bytes=VMEM_LIMIT),
        name="attention",
    )(q, k, v)


def _mix_out_kernel(o_ref, y_ref, x_ref, wo_ref, postg_ref, gm_ref, preg_ref, scf_ref, shf_ref,
                    xo_ref, h_ref):
    no = o_ref.shape[1]
    y = _bdot(o_ref[...], wo_ref[0:no]) + _bdot(y_ref[...], wo_ref[no:])
    x = x_ref[...] + gm_ref[0] * _rms(y, postg_ref[...])
    xo_ref[...] = x
    h_ref[...] = (_rms(x, preg_ref[...]) * (1.0 + scf_ref[0]) + shf_ref[0]).astype(BF16)


def _pack_rows(y):
    w = y.shape[1] // 2
    bits = lambda a: lax.bitcast_convert_type(a.astype(BF16).astype(F32), jnp.uint32)
    return (bits(y[:, w:]) & jnp.uint32(0xFFFF0000)) | (bits(y[:, :w]) >> 16)


def _unpack_rows(words):
    lo = lax.bitcast_convert_type(words << 16, F32)
    hi = lax.bitcast_convert_type(words & jnp.uint32(0xFFFF0000), F32)
    return lo, hi


def _mix_out_moe_kernel(o_ref, y_ref, x_ref, wo_ref, postg_ref, gm_ref, preg_ref, scf_ref, shf_ref,
                        router_ref, xo_ref, h_ref, route_ref):
    no = o_ref.shape[1]
    y = _bdot(o_ref[...], wo_ref[0:no]) + _bdot(y_ref[...], wo_ref[no:])
    x = x_ref[...] + gm_ref[0] * _rms(y, postg_ref[...])
    xo_ref[...] = x
    h = _rms(x, preg_ref[...]) * (1.0 + scf_ref[0]) + shf_ref[0]
    h_ref[...] = _pack_rows(h)
    logits = jnp.dot(h, router_ref[...], preferred_element_type=F32, precision=HIGHEST)
    lane = lax.broadcasted_iota(jnp.int32, logits.shape, 1)
    logits = jnp.where(lane < N_EXPERTS, logits, -jnp.inf)
    m1 = jnp.max(logits, axis=-1, keepdims=True)
    i1 = jnp.min(jnp.where(logits == m1, lane, LANE), axis=-1, keepdims=True)
    rest = jnp.where(lane == i1, -jnp.inf, logits)
    m2 = jnp.max(rest, axis=-1, keepdims=True)
    i2 = jnp.min(jnp.where(rest == m2, lane, LANE), axis=-1, keepdims=True)
    e = jnp.exp(m2 - m1)
    g1 = 1.0 / (1.0 + e)
    route_ref[...] = (jnp.where(lane == 0, i1.astype(F32), 0.0) + jnp.where(lane == 1, i2.astype(F32), 0.0)
                      + jnp.where(lane == 2, g1, 0.0) + jnp.where(lane == 3, e * g1, 0.0))


def _mix_out(o, y, x, w_o, post_g, g_m, pre_g, sc_f, sh_f, router, tiles_per_batch):
    N, D = x.shape
    tm = TM_OUT
    row = lambda w: pl.BlockSpec((tm, w), lambda i: (i, 0))
    full = lambda a: pl.BlockSpec(a.shape, lambda i: (0,) * a.ndim)
    per_b = pl.BlockSpec((1, 1, D), lambda i: (i // tiles_per_batch, 0, 0))
    in_specs = [row(o.shape[1]), row(y.shape[1]), row(D), full(w_o), full(post_g), per_b, full(pre_g), per_b, per_b]
    out_specs = [row(D), row(D)]
    out_shape = [jax.ShapeDtypeStruct((N, D), F32), jax.ShapeDtypeStruct((N, D), BF16)]
    args = [o, y, x, w_o, post_g, g_m, pre_g, sc_f, sh_f]
    body = _mix_out_kernel
    if router is not None:
        in_specs.append(full(router))
        out_specs = [row(D), row(D // 2), row(LANE)]
        out_shape = [jax.ShapeDtypeStruct((N, D), F32), jax.ShapeDtypeStruct((N, D // 2), jnp.uint32),
                     jax.ShapeDtypeStruct((N, LANE), F32)]
        args.append(router)
        body = _mix_out_moe_kernel
    return pl.pallas_call(
        body,
        grid=(N // tm,),
        in_specs=in_specs,
        out_specs=out_specs,
        out_shape=out_shape,
        compiler_params=pltpu.CompilerParams(dimension_semantics=("arbitrary",), vmem_limit_bytes=VMEM_LIMIT),
        name="mix_out",
    )(*args)


def _swiglu(h, wg, wu, wd):
    g = _bdot(h, wg)
    u = _bdot(h, wu)
    return _bdot((_silu(g) * u).astype(BF16), wd)


def _ffn_kernel(h_ref, x_ref, wg_ref, wu_ref, wd_ref, postg_ref, gf_ref, o_ref, acc_ref):
    j = pl.program_id(1)

    @pl.when(j == 0)
    def _():
        acc_ref[...] = jnp.zeros_like(acc_ref)

    acc_ref[...] += _swiglu(h_ref[...], wg_ref[...], wu_ref[...], wd_ref[...])

    @pl.when(j == pl.num_programs(1) - 1)
    def _():
        o_ref[...] = x_ref[...] + gf_ref[0] * _rms(acc_ref[...], postg_ref[...])


def _ffn(h, x, wg, wu, wd, post_g, g_f, tiles_per_batch):
    N, D = x.shape
    FF = wg.shape[1]
    tm = TM_FFN
    tf = FF // 2
    return pl.pallas_call(
        _ffn_kernel,
        grid=(N // tm, FF // tf),
        in_specs=[
            pl.BlockSpec((tm, D), lambda i, j: (i, 0)),
            pl.BlockSpec((tm, D), lambda i, j: (i, 0)),
            pl.BlockSpec((D, tf), lambda i, j: (0, j)),
            pl.BlockSpec((D, tf), lambda i, j: (0, j)),
            pl.BlockSpec((tf, D), lambda i, j: (j, 0)),
            pl.BlockSpec((1, D), lambda i, j: (0, 0)),
            pl.BlockSpec((1, 1, D), lambda i, j: (i // tiles_per_batch, 0, 0)),
        ],
        out_specs=pl.BlockSpec((tm, D), lambda i, j: (i, 0)),
        out_shape=jax.ShapeDtypeStruct((N, D), F32),
        scratch_shapes=[pltpu.VMEM((tm, D), F32)],
        compiler_params=pltpu.CompilerParams(
            dimension_semantics=("arbitrary", "arbitrary"), vmem_limit_bytes=VMEM_LIMIT),
        name="ffn_dense",
    )(h, x, wg, wu, wd, post_g, g_f)


def _row_copy(src, dst, src_row, dst_row, sem):
    return pltpu.make_async_copy(src.at[pl.ds(src_row, 1)], dst.at[pl.ds(dst_row, 1)], sem)


def _moe_dispatch_kernel(pad_lo_ref, pad_hi_ref, slot_ref, h_ref, hs_hbm, zero_ref, sem):
    tm = h_ref.shape[0]

    @pl.when(pl.program_id(0) == 0)
    def _():
        zero_ref[...] = jnp.zeros(zero_ref.shape, zero_ref.dtype)
        for e in range(N_EXPERTS):
            lo, hi = pad_lo_ref[e], pad_hi_ref[e]

            def start_pad(s, c):
                _row_copy(zero_ref, hs_hbm, 0, s, sem).start()
                return c

            def wait_pad(s, c):
                _row_copy(zero_ref, hs_hbm, 0, s, sem).wait()
                return c

            lax.fori_loop(lo, hi, start_pad, 0)
            lax.fori_loop(lo, hi, wait_pad, 0)

    def start(r, c):
        for k in range(2):
            _row_copy(h_ref, hs_hbm, r, slot_ref[0, 0, 2 * r + k], sem).start()
        return c

    lax.fori_loop(0, tm, start, 0, unroll=4)

    def wait(r, c):
        for k in range(2):
            _row_copy(h_ref, hs_hbm, r, 0, sem).wait()
        return c

    lax.fori_loop(0, tm, wait, 0, unroll=4)


def _moe_dispatch(pad_lo, pad_hi, slots, h_packed, n_slots):
    N, half = h_packed.shape
    tm = MOE_DMA_BLOCK
    return pl.pallas_call(
        _moe_dispatch_kernel,
        grid_spec=pltpu.PrefetchScalarGridSpec(
            num_scalar_prefetch=2,
            grid=(N // tm,),
            in_specs=[
                pl.BlockSpec((1, 1, 2 * tm), lambda i, lo, hi: (i, 0, 0), memory_space=pltpu.SMEM),
                pl.BlockSpec((tm, half), lambda i, lo, hi: (i, 0)),
            ],
            out_specs=pl.BlockSpec(memory_space=pl.ANY),
            scratch_shapes=[pltpu.VMEM((SUBLANE, half), jnp.uint32), pltpu.SemaphoreType.DMA(())],
        ),
        out_shape=jax.ShapeDtypeStruct((n_slots, half), jnp.uint32),
        compiler_params=pltpu.CompilerParams(dimension_semantics=("arbitrary",)),
        name="moe_dispatch",
    )(pad_lo, pad_hi, slots.reshape(N // tm, 1, 2 * tm), h_packed)


def _moe_expert_kernel(texp_ref, nused_ref, hs_ref, wg_ref, wu_ref, wd_ref, ys_ref):
    del texp_ref

    @pl.when(pl.program_id(0) < nused_ref[0])
    def _():
        lo, hi = _unpack_rows(hs_ref[...])
        lo, hi = lo.astype(BF16), hi.astype(BF16)
        half = lo.shape[1]
        g = _bdot(lo, wg_ref[0, :half]) + _bdot(hi, wg_ref[0, half:])
        u = _bdot(lo, wu_ref[0, :half]) + _bdot(hi, wu_ref[0, half:])
        ys_ref[...] = _pack_rows(_bdot((_silu(g) * u).astype(BF16), wd_ref[0]))


def _moe_experts(tile_expert, n_used, hs, wg, wu, wd):
    n_slots, half = hs.shape
    E, D, FF = wg.shape
    t = MOE_TILE
    tile = lambda i, te, nu: (jnp.minimum(i, nu[0] - 1), 0)
    return pl.pallas_call(
        _moe_expert_kernel,
        grid_spec=pltpu.PrefetchScalarGridSpec(
            num_scalar_prefetch=2,
            grid=(n_slots // t,),
            in_specs=[
                pl.BlockSpec((t, half), tile),
                pl.BlockSpec((1, D, FF), lambda i, te, nu: (te[i], 0, 0)),
                pl.BlockSpec((1, D, FF), lambda i, te, nu: (te[i], 0, 0)),
                pl.BlockSpec((1, FF, D), lambda i, te, nu: (te[i], 0, 0)),
            ],
            out_specs=pl.BlockSpec((t, half), tile),
        ),
        out_shape=jax.ShapeDtypeStruct((n_slots, half), jnp.uint32),
        compiler_params=pltpu.CompilerParams(dimension_semantics=("arbitrary",), vmem_limit_bytes=VMEM_LIMIT),
        name="moe_experts",
    )(tile_expert, n_used, hs, wg, wu, wd)


def _moe_combine_kernel(slot_ref, ys_hbm, route_ref, x_ref, postg_ref, gf_ref, o_ref, buf_ref, sem):
    tm = x_ref.shape[0]

    def start(r, c):
        for k in range(2):
            _row_copy(ys_hbm, buf_ref.at[k], slot_ref[0, 0, 2 * r + k], r, sem).start()
        return c

    lax.fori_loop(0, tm, start, 0, unroll=4)

    def wait(r, c):
        for k in range(2):
            _row_copy(ys_hbm, buf_ref.at[k], 0, r, sem).wait()
        return c

    lax.fori_loop(0, tm, wait, 0, unroll=4)
    route = route_ref[...]
    lane = lax.broadcasted_iota(jnp.int32, route.shape, 1)
    f = None
    for k in range(2):
        gate = jnp.sum(jnp.where(lane == 2 + k, route, 0.0), axis=-1, keepdims=True)
        y = jnp.concatenate(_unpack_rows(buf_ref[k]), axis=-1)
        f = gate * y if f is None else f + gate * y
    o_ref[...] = x_ref[...] + gf_ref[0] * _rms(f, postg_ref[...])


def _moe_combine(slots, ys, route, x, post_g, g_f, tiles_per_batch):
    N, D = x.shape
    tm = TM_FFN
    return pl.pallas_call(
        _moe_combine_kernel,
        grid=(N // tm,),
        in_specs=[
            pl.BlockSpec((1, 1, 2 * tm), lambda i: (i, 0, 0), memory_space=pltpu.SMEM),
            pl.BlockSpec(memory_space=pl.ANY),
            pl.BlockSpec((tm, LANE), lambda i: (i, 0)),
            pl.BlockSpec((tm, D), lambda i: (i, 0)),
            pl.BlockSpec((1, D), lambda i: (0, 0)),
            pl.BlockSpec((1, 1, D), lambda i: (i // tiles_per_batch, 0, 0)),
        ],
        out_specs=pl.BlockSpec((tm, D), lambda i: (i, 0)),
        out_shape=jax.ShapeDtypeStruct((N, D), F32),
        scratch_shapes=[pltpu.VMEM((2, tm, D // 2), jnp.uint32), pltpu.SemaphoreType.DMA(())],
        compiler_params=pltpu.CompilerParams(dimension_semantics=("arbitrary",), vmem_limit_bytes=VMEM_LIMIT),
        name="moe_combine",
    )(slots.reshape(N // tm, 1, 2 * tm), ys, route, x, post_g, g_f)


def _moe_plan(route, n_slots):
    t = MOE_TILE
    pair_expert = route[:, :2].astype(jnp.int32).reshape(-1)
    onehot = (pair_expert[:, None] == jnp.arange(N_EXPERTS, dtype=jnp.int32)[None, :]).astype(jnp.int32)
    csum = jnp.cumsum(onehot, axis=0)
    rank = jnp.take_along_axis(csum, pair_expert[:, None], axis=1)[:, 0] - 1
    count = csum[-1]
    padded = (count + t - 1) // t * t
    ends = jnp.cumsum(padded)
    starts = ends - padded
    slot = starts[pair_expert] + rank
    n_used = ends[-1:] // t
    tile_start = jnp.arange(n_slots // t, dtype=jnp.int32) * t
    tile_expert = jnp.sum((tile_start[:, None] >= ends[None, :]).astype(jnp.int32), axis=1)
    last_used = tile_expert[jnp.maximum(n_used[0] - 1, 0)]
    tile_expert = jnp.where(tile_start < ends[-1], tile_expert, last_used)
    return slot, starts + count, ends, tile_expert, n_used.astype(jnp.int32)


def _moe(h_packed, x, route, wg, wu, wd, post_g, g_f, tiles_per_batch):
    N = x.shape[0]
    n_slots = 2 * N + N_EXPERTS * MOE_TILE
    slot, pad_lo, pad_hi, tile_expert, n_used = _moe_plan(route, n_slots)
    hs = _moe_dispatch(pad_lo, pad_hi, slot, h_packed, n_slots)
    ys = _moe_experts(tile_expert, n_used, hs, wg, wu, wd)
    return _moe_combine(slot, ys, route, x, post_g, g_f, tiles_per_batch)


def _rope_partner(w):
    half = QK_ROPE // 2
    return jnp.concatenate([-w[..., half:], w[..., :half]], axis=-1)


def _rope_tables(S):
    pos = jnp.arange(S, dtype=F32)
    inv = ROPE_THETA ** (-jnp.arange(0, QK_ROPE, 2, dtype=F32) / QK_ROPE)
    ang = pos[:, None] * inv[None, :]
    cos, sin = jnp.cos(ang), jnp.sin(ang)
    cos2 = jnp.concatenate([cos, cos], axis=-1)
    sin2 = jnp.concatenate([sin, sin], axis=-1)
    scale = (QK_NOPE + QK_ROPE) ** -0.5 * 1.4426950408889634
    pad = HEAD_PAD - QK_NOPE - QK_ROPE
    cosq = jnp.concatenate([jnp.ones((S, QK_NOPE), F32), cos2, jnp.zeros((S, pad), F32)], axis=-1) * scale
    sinq = jnp.concatenate([jnp.zeros((S, QK_NOPE), F32), sin2, jnp.zeros((S, pad), F32)], axis=-1) * scale
    tk = jnp.concatenate([cos2, sin2, jnp.zeros((S, LANE - 2 * QK_ROPE), F32)], axis=-1)
    return {"cosq": cosq.T, "sinq": sinq.T, "tk": tk}


def _layer_params(l, w_in, q_norm_g, w_uq, kv_norm_g, w_ukv, sgu_ln_g, sgu_ln_b, sgu_w, sgu_b,
                  conv_w, conv_b, lru_wa, lru_ba, lru_wx, lru_bx, lru_lambda):
    D = w_in.shape[1]
    wi = w_in[l]
    o_q, o_kv = 0, Q_LORA
    o_kr = o_kv + KV_LORA
    o_u = o_kr + QK_ROPE
    o_v = o_u + SGU_WIDTH
    o_x = o_v + SGU_WIDTH
    o_y = o_x + LRU_WIDTH
    w_kr = wi[:, o_kr:o_kr + QK_ROPE]
    w_in_p = jnp.concatenate([
        wi[:, o_q:o_q + Q_LORA], wi[:, o_kv:o_kv + KV_LORA], wi[:, o_u:o_u + SGU_WIDTH],
        wi[:, o_v:o_v + SGU_WIDTH], wi[:, o_x:o_x + LRU_WIDTH], wi[:, o_y:o_y + LRU_WIDTH],
        w_kr, _rope_partner(w_kr), jnp.zeros((D, LANE - 2 * QK_ROPE), F32)], axis=-1).astype(BF16)

    H = MLA_HEADS
    pad = HEAD_PAD - QK_NOPE - QK_ROPE
    wq = w_uq[l].reshape(Q_LORA, H, QK_NOPE + QK_ROPE)
    zq = jnp.zeros((Q_LORA, H, pad), F32)
    wq_main = jnp.concatenate([wq, zq], axis=-1).reshape(Q_LORA, H * HEAD_PAD)
    wq_part = jnp.concatenate([jnp.zeros((Q_LORA, H, QK_NOPE), F32), _rope_partner(wq[..., QK_NOPE:]), zq],
                              axis=-1).reshape(Q_LORA, H * HEAD_PAD)
    w_q = jnp.concatenate([wq_main, wq_part], axis=-1).T.astype(BF16)

    wkv = w_ukv[l].reshape(KV_LORA, H, QK_NOPE + V_DIM)
    wk_lat = jnp.concatenate([wkv[..., :QK_NOPE], jnp.zeros((KV_LORA, H, HEAD_PAD - QK_NOPE), F32)],
                             axis=-1).reshape(KV_LORA, H * HEAD_PAD)
    eye = jnp.eye(QK_ROPE, dtype=F32)
    place = jnp.concatenate([jnp.zeros((QK_ROPE, QK_NOPE), F32), eye, jnp.zeros((QK_ROPE, pad), F32)], axis=-1)
    place = jnp.tile(place, (1, H))
    w_k = jnp.concatenate([wk_lat, place, place, jnp.zeros((LANE - 2 * QK_ROPE, H * HEAD_PAD), F32)],
                          axis=0).astype(BF16)
    w_v = jnp.concatenate([wkv[..., QK_NOPE:], jnp.zeros((KV_LORA, H, HEAD_PAD - V_DIM), F32)],
                          axis=-1).reshape(KV_LORA, H * HEAD_PAD).T.astype(BF16)

    def block_diag(w):
        out = jnp.zeros((LRU_WIDTH, LRU_WIDTH), F32)
        for hd in range(LRU_HEADS):
            s = hd * LRU_HEAD_DIM
            out = out.at[s:s + LRU_HEAD_DIM, s:s + LRU_HEAD_DIM].set(w[hd])
        return out.astype(BF16)

    row = lambda a: a[l].reshape(1, -1)
    return {
        "w_in": w_in_p, "qg": row(q_norm_g), "w_q": w_q, "kvg": row(kv_norm_g), "w_k": w_k, "w_v": w_v,
        "ln_g": row(sgu_ln_g), "ln_b": row(sgu_ln_b),
        "w_s": sgu_w[l].reshape(SGU_GROUPS * SGU_BLOCK, SGU_BLOCK),
        "b_s": jnp.repeat(sgu_b[l].T, SGU_GROUP_DIM, axis=1),
        "conv_w": conv_w[l], "conv_b": row(conv_b), "wa": block_diag(lru_wa[l]), "ba": row(lru_ba),
        "wx": block_diag(lru_wx[l]), "bx": row(lru_bx), "lam": row(lru_lambda),
    }


def kernel(x, c, w_mod, b_mod, pre_mix_g, post_mix_g, w_in, q_norm_g, w_uq, kv_norm_g, w_ukv, sgu_ln_g,
           sgu_ln_b, sgu_w, sgu_b, conv_w, conv_b, lru_wa, lru_ba, lru_wx, lru_bx, lru_lambda, w_o,
           pre_ffn_g, post_ffn_g, ffn_w_gate, ffn_w_up, ffn_w_down, moe_router, moe_w_gate, moe_w_up,
           moe_w_down):
    B, S, D = x.shape
    L = w_mod.shape[0]
    N = B * S
    tabs = _rope_tables(S)
    mod = _modulation(c, w_mod, b_mod)
    xf = x.reshape(N, D)
    for l in range(L):
        sh_m, sc_m, g_m, sh_f, sc_f, g_f = [m.reshape(B, 1, D) for m in jnp.split(mod[l], 6, axis=-1)]
        p = _layer_params(l, w_in, q_norm_g, w_uq, kv_norm_g, w_ukv, sgu_ln_g, sgu_ln_b, sgu_w, sgu_b,
                          conv_w, conv_b, lru_wa, lru_ba, lru_wx, lru_bx, lru_lambda)
        q, k, v, ymix = _mix_in(xf.reshape(B, S, D), sh_m, sc_m, pre_mix_g[l].reshape(1, D), p, tabs)
        o = _attention(q, k, v)
        moe = l % 2 == 1
        router = None
        if moe:
            router = jnp.pad(moe_router[l // 2], ((0, 0), (0, LANE - N_EXPERTS)))
        res = _mix_out(o.reshape(N, -1), ymix.reshape(N, -1), xf, w_o[l].astype(BF16),
                       post_mix_g[l].reshape(1, D), g_m, pre_ffn_g[l].reshape(1, D), sc_f, sh_f, router,
                       S // TM_OUT)
        post_g = post_ffn_g[l].reshape(1, D)
        if moe:
            xf, h2, comb = res
            xf = _moe(h2, xf, comb, moe_w_gate[l // 2].astype(BF16), moe_w_up[l // 2].astype(BF16),
                      moe_w_down[l // 2].astype(BF16), post_g, g_f, S // TM_FFN)
        else:
            xf, h2 = res
            xf = _ffn(h2, xf, ffn_w_gate[l // 2].astype(BF16), ffn_w_up[l // 2].astype(BF16),
                      ffn_w_down[l // 2].astype(BF16), post_g, g_f, S // TM_FFN)
    return xf.reshape(B, S, D)
```

```python
import functools

import jax
import jax.numpy as jnp
from jax import lax
from jax.experimental import pallas as pl
from jax.experimental.pallas import tpu as pltpu

F32 = jnp.float32
BF16 = jnp.bfloat16
HIGHEST = lax.Precision.HIGHEST

MLA_HEADS = 8
QK_NOPE = 64
QK_ROPE = 32
V_DIM = 64
Q_LORA = 256
KV_LORA = 128
ROPE_THETA = 10000.0
CHUNK = 64
SGU_GROUPS = 4
SGU_GROUP_DIM = 64
SGU_WIDTH = SGU_GROUPS * SGU_GROUP_DIM
SGU_BLOCK = 128
LRU_HEADS = 4
LRU_HEAD_DIM = 64
LRU_WIDTH = LRU_HEADS * LRU_HEAD_DIM
CONV_W = 4
LRU_C = 8.0
N_EXPERTS = 8
NEG_INF = -1e30
RMS_EPS = 1e-6
LN_EPS = 1e-5

LANE = 128
SUBLANE = 8
HEAD_PAD = LANE

C_Q = 0
C_KV = C_Q + Q_LORA
C_U = C_KV + KV_LORA
C_V = C_U + SGU_WIDTH
C_X = C_V + SGU_WIDTH
C_Y = C_X + LRU_WIDTH
C_KR = C_Y + LRU_WIDTH
D_IN_PAD = C_KR + LANE

VMEM_LIMIT = 56 * 1024 * 1024

TM_MIX = 512
TQ = 512
TK = 512
ATTN_HEADS_PER_STEP = 8
TM_OUT = 512
TM_FFN = 512
MOE_TILE = 512
MOE_DMA_BLOCK = 2048


def _rms(x, g):
    return x * lax.rsqrt(jnp.mean(x * x, axis=-1, keepdims=True) + RMS_EPS) * g


def _gelu(x):
    return 0.5 * x * (1.0 + jnp.tanh(0.7978845608028654 * (x + 0.044715 * (x * x * x))))


def _sigmoid(x):
    return 1.0 / (1.0 + jnp.exp(-x))


def _silu(x):
    return x * _sigmoid(x)


def _bdot(a, b):
    return jnp.dot(a, b, preferred_element_type=F32)


def _mod_kernel(c_ref, w_ref, b_ref, o_ref):
    c = c_ref[...]
    o_ref[0] = jnp.dot(_silu(c), w_ref[0], preferred_element_type=F32, precision=HIGHEST) + b_ref[0]


def _modulation(c, w_mod, b_mod):
    L, D, D6 = w_mod.shape
    B = c.shape[0]
    return pl.pallas_call(
        _mod_kernel,
        grid=(L, D6 // D),
        in_specs=[
            pl.BlockSpec((B, D), lambda l, j: (0, 0)),
            pl.BlockSpec((1, D, D), lambda l, j: (l, 0, j)),
            pl.BlockSpec((1, 1, D), lambda l, j: (l, 0, j)),
        ],
        out_specs=pl.BlockSpec((1, B, D), lambda l, j: (l, 0, j)),
        out_shape=jax.ShapeDtypeStruct((L, B, D6), F32),
        compiler_params=pltpu.CompilerParams(vmem_limit_bytes=VMEM_LIMIT),
        name="modulation",
    )(c, w_mod, b_mod.reshape(L, 1, D6))


def _shift_rows(cur, prev8, k):
    rolled = pltpu.roll(cur, k, 0)
    fix = pltpu.roll(prev8, k, 0)
    row = lax.broadcasted_iota(jnp.int32, fix.shape, 0)
    top = jnp.where(row < k, fix, rolled[:SUBLANE])
    return jnp.concatenate([top, rolled[SUBLANE:]], axis=0)


def _mix_in_kernel(x_ref, sh_ref, sc_ref, preg_ref, win_ref, qg_ref, wq_ref, kvg_ref, wk_ref, wv_ref,
                   cosq_ref, sinq_ref, tk_ref, lng_ref, lnb_ref, ws_ref, bs_ref,
                   cw_ref, cb_ref, wa_ref, ba_ref, wx_ref, bx_ref, lam_ref,
                   q_ref, k_ref, v_ref, y_ref, prev_ref, hc_ref):
    tm = x_ref.shape[1]

    @pl.when(pl.program_id(1) == 0)
    def _():
        prev_ref[...] = jnp.zeros_like(prev_ref)
        hc_ref[...] = jnp.zeros_like(hc_ref)

    x = x_ref[0]
    h = _rms(x, preg_ref[...]) * (1.0 + sc_ref[0]) + sh_ref[0]
    z = _bdot(h.astype(BF16), win_ref[...])

    nt = (((1,), (1,)), ((), ()))
    qn = _rms(z[:, C_Q:C_Q + Q_LORA], qg_ref[...]).astype(BF16)
    qq = lax.dot_general(wq_ref[...], qn, nt, preferred_element_type=F32)
    cosq = cosq_ref[...]
    sinq = sinq_ref[...]
    qw = MLA_HEADS * HEAD_PAD
    for hd in range(MLA_HEADS):
        lo = hd * HEAD_PAD
        qh = qq[lo:lo + HEAD_PAD] * cosq + qq[qw + lo:qw + lo + HEAD_PAD] * sinq
        q_ref[0, lo:lo + HEAD_PAD, :] = qh.astype(BF16)
    kvn = _rms(z[:, C_KV:C_KV + KV_LORA], kvg_ref[...]).astype(BF16)
    kr = (z[:, C_KR:C_KR + LANE] * tk_ref[...]).astype(BF16)
    k_in = jnp.concatenate([kvn, kr], axis=-1)
    k_ref[0] = _bdot(k_in, wk_ref[...]).astype(BF16)
    vv = lax.dot_general(wv_ref[...], kvn, nt, preferred_element_type=F32)
    ones_row = lax.broadcasted_iota(jnp.int32, vv.shape, 0) % HEAD_PAD == V_DIM
    v_ref[0, 0] = jnp.where(ones_row, 1.0, vv).astype(BF16)

    u = _gelu(z[:, C_U:C_U + SGU_WIDTH])
    gv = _gelu(z[:, C_V:C_V + SGU_WIDTH])
    mu = jnp.mean(gv, axis=-1, keepdims=True)
    var = jnp.mean(jnp.square(gv - mu), axis=-1, keepdims=True)
    vn = ((gv - mu) * lax.rsqrt(var + LN_EPS) * lng_ref[...] + lnb_ref[...]).astype(BF16)
    ws = ws_ref[...]
    r_t = lax.broadcasted_iota(jnp.int32, ws.shape, 0) % SGU_BLOCK
    c_s = lax.broadcasted_iota(jnp.int32, ws.shape, 1)
    ws = jnp.where(c_s <= r_t, ws, 0.0).astype(BF16)
    grp = lax.broadcasted_iota(jnp.int32, (SGU_BLOCK, SGU_WIDTH), 1) // SGU_GROUP_DIM
    for blk in range(tm // SGU_BLOCK):
        r0 = blk * SGU_BLOCK
        res = _bdot(ws, vn[r0:r0 + SGU_BLOCK])
        mixed = bs_ref[...]
        for g in range(SGU_GROUPS):
            mixed = mixed + jnp.where(grp == g, res[g * SGU_BLOCK:(g + 1) * SGU_BLOCK], 0.0)
        y_ref[0, r0:r0 + SGU_BLOCK, 0:SGU_WIDTH] = (u[r0:r0 + SGU_BLOCK] * mixed).astype(BF16)

    zx = z[:, C_X:C_X + LRU_WIDTH]
    prev8 = prev_ref[...]
    cw = cw_ref[...]
    xc = cb_ref[...] + zx * cw[CONV_W - 1:CONV_W]
    for k in range(1, CONV_W):
        xc = xc + _shift_rows(zx, prev8, k) * cw[CONV_W - 1 - k:CONV_W - k]
    prev_ref[...] = zx[tm - SUBLANE:]
    xcb = xc.astype(BF16)
    r = _sigmoid(_bdot(xcb, wa_ref[...]) + ba_ref[...])
    ig = _sigmoid(_bdot(xcb, wx_ref[...]) + bx_ref[...])
    lam = lam_ref[...]
    softplus_neg_lam = jnp.maximum(-lam, 0.0) + jnp.log(1.0 + jnp.exp(-jnp.abs(lam)))
    log_a = -LRU_C * r * softplus_neg_lam
    a = jnp.exp(log_a)
    t = jnp.tanh(log_a)
    b = jnp.sqrt(-2.0 * t / (1.0 - t)) * (ig * xc)
    row = lax.broadcasted_iota(jnp.int32, a.shape, 0)
    k = 1
    while k < tm:
        valid = row >= k
        a_sh = pltpu.roll(a, k, 0)
        b_sh = pltpu.roll(b, k, 0)
        b = jnp.where(valid, b + a * b_sh, b)
        a = jnp.where(valid, a * a_sh, a)
        k *= 2
    hs = b + a * hc_ref[0:1]
    hc_ref[0:1] = hs[tm - 1:tm]
    y_ref[0, :, SGU_WIDTH:SGU_WIDTH + LRU_WIDTH] = (hs * _gelu(z[:, C_Y:C_Y + LRU_WIDTH])).astype(BF16)


def _mix_in(x, sh, sc, pre_g, p, tabs):
    B, S, D = x.shape
    tm = TM_MIX
    full = lambda a: pl.BlockSpec(a.shape, lambda b, s: (0,) * a.ndim)
    per_b = pl.BlockSpec((1, 1, D), lambda b, s: (b, 0, 0))
    tab = pl.BlockSpec((tm, LANE), lambda b, s: (s, 0))
    tab_t = pl.BlockSpec((HEAD_PAD, tm), lambda b, s: (0, s))
    assert tm == TK
    consts = [p["w_in"], p["qg"], p["w_q"], p["kvg"], p["w_k"], p["w_v"]]
    consts2 = [p["ln_g"], p["ln_b"], p["w_s"], p["b_s"], p["conv_w"], p["conv_b"], p["wa"], p["ba"],
               p["wx"], p["bx"], p["lam"]]
    qw = MLA_HEADS * HEAD_PAD
    return pl.pallas_call(
        _mix_in_kernel,
        grid=(B, S // tm),
        in_specs=[pl.BlockSpec((1, tm, D), lambda b, s: (b, s, 0)), per_b, per_b, full(pre_g)]
        + [full(a) for a in consts] + [tab_t, tab_t, tab] + [full(a) for a in consts2],
        out_specs=[
            pl.BlockSpec((1, qw, tm), lambda b, s: (b, 0, s)),
            pl.BlockSpec((1, tm, qw), lambda b, s: (b, s, 0)),
            pl.BlockSpec((1, 1, qw, tm), lambda b, s: (b, s, 0, 0)),
            pl.BlockSpec((1, tm, SGU_WIDTH + LRU_WIDTH), lambda b, s: (b, s, 0)),
        ],
        out_shape=[
            jax.ShapeDtypeStruct((B, qw, S), BF16),
            jax.ShapeDtypeStruct((B, S, qw), BF16),
            jax.ShapeDtypeStruct((B, S // tm, qw, tm), BF16),
            jax.ShapeDtypeStruct((B, S, SGU_WIDTH + LRU_WIDTH), BF16),
        ],
        scratch_shapes=[pltpu.VMEM((SUBLANE, LRU_WIDTH), F32), pltpu.VMEM((SUBLANE, LRU_WIDTH), F32)],
        compiler_params=pltpu.CompilerParams(
            dimension_semantics=("arbitrary", "arbitrary"), vmem_limit_bytes=VMEM_LIMIT),
        name="mix_in",
    )(x, sh, sc, pre_g, *consts, tabs["cosq"], tabs["sinq"], tabs["tk"], *consts2)


def _attn_kernel(q_ref, k_ref, v_ref, o_ref, m_ref, acc_ref, p_ref, a_ref):
    i = pl.program_id(2)
    heads = q_ref.shape[1] // HEAD_PAD
    feat = lambda hd: slice(hd * HEAD_PAD, (hd + 1) * HEAD_PAD)

    m_ref[...] = jnp.full(m_ref.shape, NEG_INF, F32)
    acc_ref[...] = jnp.zeros(acc_ref.shape, F32)
    p_ref[...] = jnp.zeros(p_ref.shape, BF16)
    a_ref[...] = jnp.ones(a_ref.shape, F32)

    def scores(t, hd):
        keys = pl.ds(pl.multiple_of(t * TK, TK), TK)
        return _bdot(k_ref[0, keys, feat(hd)], q_ref[0, feat(hd), :])

    def weights(s, hd):
        m = m_ref[hd]
        m_new = jnp.maximum(m, jnp.max(s, axis=0, keepdims=True))
        p_ref[hd] = jnp.exp2(s - m_new).astype(BF16)
        a_ref[hd] = jnp.exp2(m - m_new)
        m_ref[hd] = m_new

    def accumulate(t, hd):
        acc_ref[hd] = a_ref[hd] * acc_ref[hd] + _bdot(v_ref[0, t, feat(hd), :], p_ref[hd])

    def tile(t, masked):
        ss = [scores(t, hd) for hd in range(heads)]
        for hd in range(heads):
            s = ss[hd]
            if masked:
                kc = lax.broadcasted_iota(jnp.int32, s.shape, 0) // CHUNK
                qc = lax.broadcasted_iota(jnp.int32, s.shape, 1) // CHUNK
                s = jnp.where(kc <= qc, s, NEG_INF)
            weights(s, hd)
            accumulate(t, hd)

    def body(t, c):
        tile(t, False)
        return c

    lax.fori_loop(0, i, body, 0)
    tile(i, True)
    outs = []
    for hd in range(heads):
        acc = acc_ref[hd]
        outs.append((acc / acc[V_DIM:V_DIM + 1]).T[:, :V_DIM])
    o_ref[0] = jnp.concatenate(outs, axis=-1).astype(BF16)


def _attention(q, k, v):
    B, S, _ = k.shape
    hp = ATTN_HEADS_PER_STEP
    assert TQ == TK
    return pl.pallas_call(
        _attn_kernel,
        grid=(B, MLA_HEADS // hp, S // TQ),
        in_specs=[
            pl.BlockSpec((1, hp * HEAD_PAD, TQ), lambda b, h, i: (b, h, i)),
            pl.BlockSpec((1, S, hp * HEAD_PAD), lambda b, h, i: (b, 0, h)),
            pl.BlockSpec((1, S // TK, hp * HEAD_PAD, TK), lambda b, h, i: (b, 0, h, 0)),
        ],
        out_specs=pl.BlockSpec((1, TQ, hp * V_DIM), lambda b, h, i: (b, i, h)),
        out_shape=jax.ShapeDtypeStruct((B, S, MLA_HEADS * V_DIM), BF16),
        scratch_shapes=[
            pltpu.VMEM((hp, 1, TQ), F32),
            pltpu.VMEM((hp, HEAD_PAD, TQ), F32),
            pltpu.VMEM((hp, TK, TQ), BF16),
            pltpu.VMEM((hp, 1, TQ), F32),
        ],
        compiler_params=pltpu.CompilerParams(
            dimension_semantics=("arbitrary", "arbitrary", "arbitrary"), vmem_limit_bytes=VMEM_LIMIT),
        name="attention",
    )(q, k, v)


def _mix_out_kernel(o_ref, y_ref, x_ref, wo_ref, postg_ref, gm_ref, preg_ref, scf_ref, shf_ref,
                    xo_ref, h_ref):
    no = o_ref.shape[1]
    y = _bdot(o_ref[...], wo_ref[0:no]) + _bdot(y_ref[...], wo_ref[no:])
    x = x_ref[...] + gm_ref[0] * _rms(y, postg_ref[...])
    xo_ref[...] = x
    h_ref[...] = (_rms(x, preg_ref[...]) * (1.0 + scf_ref[0]) + shf_ref[0]).astype(BF16)


def _pack_rows(y):
    w = y.shape[1] // 2
    bits = lambda a: lax.bitcast_convert_type(a.astype(BF16).astype(F32), jnp.uint32)
    return (bits(y[:, w:]) & jnp.uint32(0xFFFF0000)) | (bits(y[:, :w]) >> 16)


def _unpack_rows(words):
    lo = lax.bitcast_convert_type(words << 16, F32)
    hi = lax.bitcast_convert_type(words & jnp.uint32(0xFFFF0000), F32)
    return lo, hi


def _mix_out_moe_kernel(o_ref, y_ref, x_ref, wo_ref, postg_ref, gm_ref, preg_ref, scf_ref, shf_ref,
                        router_ref, xo_ref, h_ref, route_ref):
    no = o_ref.shape[1]
    y = _bdot(o_ref[...], wo_ref[0:no]) + _bdot(y_ref[...], wo_ref[no:])
    x = x_ref[...] + gm_ref[0] * _rms(y, postg_ref[...])
    xo_ref[...] = x
    h = _rms(x, preg_ref[...]) * (1.0 + scf_ref[0]) + shf_ref[0]
    h_ref[...] = _pack_rows(h)
    h_hi = h.astype(BF16)
    h_lo = (h - h_hi.astype(F32)).astype(BF16)
    t = _bdot(h_hi, router_ref[...])
    logits = t[:, :LANE] + t[:, LANE:] + _bdot(h_lo, router_ref[:, :LANE])
    lane = lax.broadcasted_iota(jnp.int32, logits.shape, 1)
    logits = jnp.where(lane < N_EXPERTS, logits, -jnp.inf)
    m1 = jnp.max(logits, axis=-1, keepdims=True)
    i1 = jnp.min(jnp.where(logits == m1, lane, LANE), axis=-1, keepdims=True)
    rest = jnp.where(lane == i1, -jnp.inf, logits)
    m2 = jnp.max(rest, axis=-1, keepdims=True)
    i2 = jnp.min(jnp.where(rest == m2, lane, LANE), axis=-1, keepdims=True)
    e = jnp.exp(m2 - m1)
    g1 = 1.0 / (1.0 + e)
    route_ref[...] = (jnp.where(lane == 0, i1.astype(F32), 0.0) + jnp.where(lane == 1, i2.astype(F32), 0.0)
                      + jnp.where(lane == 2, g1, 0.0) + jnp.where(lane == 3, e * g1, 0.0))


def _mix_out(o, y, x, w_o, post_g, g_m, pre_g, sc_f, sh_f, router, tiles_per_batch):
    N, D = x.shape
    tm = TM_OUT
    row = lambda w: pl.BlockSpec((tm, w), lambda i: (i, 0))
    full = lambda a: pl.BlockSpec(a.shape, lambda i: (0,) * a.ndim)
    per_b = pl.BlockSpec((1, 1, D), lambda i: (i // tiles_per_batch, 0, 0))
    in_specs = [row(o.shape[1]), row(y.shape[1]), row(D), full(w_o), full(post_g), per_b, full(pre_g), per_b, per_b]
    out_specs = [row(D), row(D)]
    out_shape = [jax.ShapeDtypeStruct((N, D), F32), jax.ShapeDtypeStruct((N, D), BF16)]
    args = [o, y, x, w_o, post_g, g_m, pre_g, sc_f, sh_f]
    body = _mix_out_kernel
    if router is not None:
        in_specs.append(full(router))
        out_specs = [row(D), row(D // 2), row(LANE)]
        out_shape = [jax.ShapeDtypeStruct((N, D), F32), jax.ShapeDtypeStruct((N, D // 2), jnp.uint32),
                     jax.ShapeDtypeStruct((N, LANE), F32)]
        args.append(router)
        body = _mix_out_moe_kernel
    return pl.pallas_call(
        body,
        grid=(N // tm,),
        in_specs=in_specs,
        out_specs=out_specs,
        out_shape=out_shape,
        compiler_params=pltpu.CompilerParams(dimension_semantics=("arbitrary",), vmem_limit_bytes=VMEM_LIMIT),
        name="mix_out",
    )(*args)


def _swiglu(h, wg, wu, wd):
    g = _bdot(h, wg)
    u = _bdot(h, wu)
    return _bdot((_silu(g) * u).astype(BF16), wd)


def _ffn_kernel(h_ref, x_ref, wg_ref, wu_ref, wd_ref, postg_ref, gf_ref, o_ref, acc_ref):
    j = pl.program_id(1)

    @pl.when(j == 0)
    def _():
        acc_ref[...] = jnp.zeros_like(acc_ref)

    acc_ref[...] += _swiglu(h_ref[...], wg_ref[...], wu_ref[...], wd_ref[...])

    @pl.when(j == pl.num_programs(1) - 1)
    def _():
        o_ref[...] = x_ref[...] + gf_ref[0] * _rms(acc_ref[...], postg_ref[...])


def _ffn(h, x, wg, wu, wd, post_g, g_f, tiles_per_batch):
    N, D = x.shape
    FF = wg.shape[1]
    tm = TM_FFN
    tf = FF // 2
    return pl.pallas_call(
        _ffn_kernel,
        grid=(N // tm, FF // tf),
        in_specs=[
            pl.BlockSpec((tm, D), lambda i, j: (i, 0)),
            pl.BlockSpec((tm, D), lambda i, j: (i, 0)),
            pl.BlockSpec((D, tf), lambda i, j: (0, j)),
            pl.BlockSpec((D, tf), lambda i, j: (0, j)),
            pl.BlockSpec((tf, D), lambda i, j: (j, 0)),
            pl.BlockSpec((1, D), lambda i, j: (0, 0)),
            pl.BlockSpec((1, 1, D), lambda i, j: (i // tiles_per_batch, 0, 0)),
        ],
        out_specs=pl.BlockSpec((tm, D), lambda i, j: (i, 0)),
        out_shape=jax.ShapeDtypeStruct((N, D), F32),
        scratch_shapes=[pltpu.VMEM((tm, D), F32)],
        compiler_params=pltpu.CompilerParams(
            dimension_semantics=("arbitrary", "arbitrary"), vmem_limit_bytes=VMEM_LIMIT),
        name="ffn_dense",
    )(h, x, wg, wu, wd, post_g, g_f)


def _row_copy(src, dst, src_row, dst_row, sem):
    return pltpu.make_async_copy(src.at[pl.ds(src_row, 1)], dst.at[pl.ds(dst_row, 1)], sem)


def _moe_dispatch_kernel(pad_lo_ref, pad_hi_ref, slot_ref, h_ref, hs_hbm, zero_ref, sem):
    tm = h_ref.shape[0]

    @pl.when(pl.program_id(0) == 0)
    def _():
        zero_ref[...] = jnp.zeros(zero_ref.shape, zero_ref.dtype)
        for e in range(pad_lo_ref.shape[0]):
            lo, hi = pad_lo_ref[e], pad_hi_ref[e]

            def start_pad(s, c):
                _row_copy(zero_ref, hs_hbm, 0, s, sem).start()
                return c

            def wait_pad(s, c):
                _row_copy(zero_ref, hs_hbm, 0, s, sem).wait()
                return c

            lax.fori_loop(lo, hi, start_pad, 0)
            lax.fori_loop(lo, hi, wait_pad, 0)

    def start(r, c):
        for k in range(2):
            _row_copy(h_ref, hs_hbm, r, slot_ref[0, 0, 2 * r + k], sem).start(priority=k)
        return c

    lax.fori_loop(0, tm, start, 0, unroll=4)

    def wait(r, c):
        for k in range(2):
            _row_copy(h_ref, hs_hbm, r, 0, sem).wait()
        return c

    lax.fori_loop(0, tm, wait, 0, unroll=4)


def _moe_dispatch(pad_lo, pad_hi, slots, h_packed, n_slots):
    N, half = h_packed.shape
    tm = MOE_DMA_BLOCK
    return pl.pallas_call(
        _moe_dispatch_kernel,
        grid_spec=pltpu.PrefetchScalarGridSpec(
            num_scalar_prefetch=2,
            grid=(N // tm,),
            in_specs=[
                pl.BlockSpec((1, 1, 2 * tm), lambda i, lo, hi: (i, 0, 0), memory_space=pltpu.SMEM),
                pl.BlockSpec((tm, half), lambda i, lo, hi: (i, 0)),
            ],
            out_specs=pl.BlockSpec(memory_space=pl.ANY),
            scratch_shapes=[pltpu.VMEM((SUBLANE, half), jnp.uint32), pltpu.SemaphoreType.DMA(())],
        ),
        out_shape=jax.ShapeDtypeStruct((n_slots, half), jnp.uint32),
        compiler_params=pltpu.CompilerParams(dimension_semantics=("arbitrary",)),
        name="moe_dispatch",
    )(pad_lo, pad_hi, slots.reshape(N // tm, 1, 2 * tm), h_packed)


def _moe_expert_kernel(texp_ref, nused_ref, hs_ref, wg_ref, wu_ref, wd_ref, ys_ref):
    del texp_ref

    @pl.when(pl.program_id(0) < nused_ref[0])
    def _():
        lo, hi = _unpack_rows(hs_ref[...])
        lo, hi = lo.astype(BF16), hi.astype(BF16)
        half = lo.shape[1]
        g = _bdot(lo, wg_ref[0, :half]) + _bdot(hi, wg_ref[0, half:])
        u = _bdot(lo, wu_ref[0, :half]) + _bdot(hi, wu_ref[0, half:])
        ys_ref[...] = _pack_rows(_bdot((_silu(g) * u).astype(BF16), wd_ref[0]))

    @pl.when(pl.program_id(0) >= nused_ref[0])
    def _():
        ys_ref[...] = jnp.zeros(ys_ref.shape, jnp.uint32)


def _moe_experts(tile_expert, n_used, hs, wg, wu, wd):
    n_slots, half = hs.shape
    E, D, FF = wg.shape
    t = MOE_TILE
    tile = lambda i, te, nu: (i, 0)
    return pl.pallas_call(
        _moe_expert_kernel,
        grid_spec=pltpu.PrefetchScalarGridSpec(
            num_scalar_prefetch=2,
            grid=(n_slots // t,),
            in_specs=[
                pl.BlockSpec((t, half), tile),
                pl.BlockSpec((1, D, FF), lambda i, te, nu: (te[i], 0, 0)),
                pl.BlockSpec((1, D, FF), lambda i, te, nu: (te[i], 0, 0)),
                pl.BlockSpec((1, FF, D), lambda i, te, nu: (te[i], 0, 0)),
            ],
            out_specs=pl.BlockSpec((t, half), tile),
        ),
        out_shape=jax.ShapeDtypeStruct((n_slots, half), jnp.uint32),
        compiler_params=pltpu.CompilerParams(dimension_semantics=("arbitrary",), vmem_limit_bytes=VMEM_LIMIT),
        name="moe_experts",
    )(tile_expert, n_used, hs, wg, wu, wd)


def _moe_combine_kernel(slot_ref, ys_hbm, route_ref, x_ref, postg_ref, gf_ref, o_ref, buf_ref, sem):
    tm = x_ref.shape[0]

    def start(r, c):
        for k in range(2):
            _row_copy(ys_hbm, buf_ref.at[k], slot_ref[0, 0, 2 * r + k], r, sem).start(priority=k)
        return c

    lax.fori_loop(0, tm, start, 0, unroll=4)

    def wait(r, c):
        for k in range(2):
            _row_copy(ys_hbm, buf_ref.at[k], 0, r, sem).wait()
        return c

    lax.fori_loop(0, tm, wait, 0, unroll=4)
    route = route_ref[...]
    lane = lax.broadcasted_iota(jnp.int32, route.shape, 1)
    f = None
    for k in range(2):
        gate = jnp.sum(jnp.where(lane == 2 + k, route, 0.0), axis=-1, keepdims=True)
        y = jnp.concatenate(_unpack_rows(buf_ref[k]), axis=-1)
        f = gate * y if f is None else f + gate * y
    o_ref[...] = x_ref[...] + gf_ref[0] * _rms(f, postg_ref[...])


def _moe_combine(slots, ys, route, x, post_g, g_f, tiles_per_batch):
    N, D = x.shape
    tm = TM_FFN
    return pl.pallas_call(
        _moe_combine_kernel,
        grid=(N // tm,),
        in_specs=[
            pl.BlockSpec((1, 1, 2 * tm), lambda i: (i, 0, 0), memory_space=pltpu.SMEM),
            pl.BlockSpec(memory_space=pl.ANY),
            pl.BlockSpec((tm, LANE), lambda i: (i, 0)),
            pl.BlockSpec((tm, D), lambda i: (i, 0)),
            pl.BlockSpec((1, D), lambda i: (0, 0)),
            pl.BlockSpec((1, 1, D), lambda i: (i // tiles_per_batch, 0, 0)),
        ],
        out_specs=pl.BlockSpec((tm, D), lambda i: (i, 0)),
        out_shape=jax.ShapeDtypeStruct((N, D), F32),
        scratch_shapes=[pltpu.VMEM((2, tm, D // 2), jnp.uint32), pltpu.SemaphoreType.DMA(())],
        compiler_params=pltpu.CompilerParams(dimension_semantics=("arbitrary",), vmem_limit_bytes=VMEM_LIMIT),
        name="moe_combine",
    )(slots.reshape(N // tm, 1, 2 * tm), ys, route, x, post_g, g_f)


def _moe_plan(route, n_slots):
    t = MOE_TILE
    pair_expert = route[:, :2].astype(jnp.int32).reshape(-1)
    onehot = (pair_expert[:, None] == jnp.arange(N_EXPERTS, dtype=jnp.int32)[None, :]).astype(jnp.int32)
    csum = jnp.cumsum(onehot, axis=0)
    rank = jnp.take_along_axis(csum, pair_expert[:, None], axis=1)[:, 0] - 1
    count = csum[-1]
    padded = (count + t - 1) // t * t
    ends = jnp.cumsum(padded)
    starts = ends - padded
    slot = starts[pair_expert] + rank
    n_used = ends[-1:] // t
    tile_start = jnp.arange(n_slots // t, dtype=jnp.int32) * t
    tile_expert = jnp.sum((tile_start[:, None] >= ends[None, :]).astype(jnp.int32), axis=1)
    last_used = tile_expert[jnp.maximum(n_used[0] - 1, 0)]
    tile_expert = jnp.where(tile_start < ends[-1], tile_expert, last_used)
    pad_lo = jnp.concatenate([starts + count, ends[-1:]]).astype(jnp.int32)
    pad_hi = jnp.concatenate([ends, jnp.full((1,), n_slots, ends.dtype)]).astype(jnp.int32)
    return slot, pad_lo, pad_hi, tile_expert, n_used.astype(jnp.int32)


def _moe(h_packed, x, route, wg, wu, wd, post_g, g_f, tiles_per_batch):
    N = x.shape[0]
    n_slots = 2 * N + N_EXPERTS * MOE_TILE
    slot, pad_lo, pad_hi, tile_expert, n_used = _moe_plan(route, n_slots)
    hs = _moe_dispatch(pad_lo, pad_hi, slot, h_packed, n_slots)
    ys = _moe_experts(tile_expert, n_used, hs, wg, wu, wd)
    return _moe_combine(slot, ys, route, x, post_g, g_f, tiles_per_batch)


def _rope_partner(w):
    half = QK_ROPE // 2
    return jnp.concatenate([-w[..., half:], w[..., :half]], axis=-1)


def _rope_tables(S):
    pos = jnp.arange(S, dtype=F32)
    inv = ROPE_THETA ** (-jnp.arange(0, QK_ROPE, 2, dtype=F32) / QK_ROPE)
    ang = pos[:, None] * inv[None, :]
    cos, sin = jnp.cos(ang), jnp.sin(ang)
    cos2 = jnp.concatenate([cos, cos], axis=-1)
    sin2 = jnp.concatenate([sin, sin], axis=-1)
    scale = (QK_NOPE + QK_ROPE) ** -0.5 * 1.4426950408889634
    pad = HEAD_PAD - QK_NOPE - QK_ROPE
    cosq = jnp.concatenate([jnp.ones((S, QK_NOPE), F32), cos2, jnp.zeros((S, pad), F32)], axis=-1) * scale
    sinq = jnp.concatenate([jnp.zeros((S, QK_NOPE), F32), sin2, jnp.zeros((S, pad), F32)], axis=-1) * scale
    tk = jnp.concatenate([cos2, sin2, jnp.zeros((S, LANE - 2 * QK_ROPE), F32)], axis=-1)
    return {"cosq": cosq.T, "sinq": sinq.T, "tk": tk}


def _layer_params(l, w_in, q_norm_g, w_uq, kv_norm_g, w_ukv, sgu_ln_g, sgu_ln_b, sgu_w, sgu_b,
                  conv_w, conv_b, lru_wa, lru_ba, lru_wx, lru_bx, lru_lambda):
    D = w_in.shape[1]
    wi = w_in[l]
    o_q, o_kv = 0, Q_LORA
    o_kr = o_kv + KV_LORA
    o_u = o_kr + QK_ROPE
    o_v = o_u + SGU_WIDTH
    o_x = o_v + SGU_WIDTH
    o_y = o_x + LRU_WIDTH
    w_kr = wi[:, o_kr:o_kr + QK_ROPE]
    w_in_p = jnp.concatenate([
        wi[:, o_q:o_q + Q_LORA], wi[:, o_kv:o_kv + KV_LORA], wi[:, o_u:o_u + SGU_WIDTH],
        wi[:, o_v:o_v + SGU_WIDTH], wi[:, o_x:o_x + LRU_WIDTH], wi[:, o_y:o_y + LRU_WIDTH],
        w_kr, _rope_partner(w_kr), jnp.zeros((D, LANE - 2 * QK_ROPE), F32)], axis=-1).astype(BF16)

    H = MLA_HEADS
    pad = HEAD_PAD - QK_NOPE - QK_ROPE
    wq = w_uq[l].reshape(Q_LORA, H, QK_NOPE + QK_ROPE)
    zq = jnp.zeros((Q_LORA, H, pad), F32)
    wq_main = jnp.concatenate([wq, zq], axis=-1).reshape(Q_LORA, H * HEAD_PAD)
    wq_part = jnp.concatenate([jnp.zeros((Q_LORA, H, QK_NOPE), F32), _rope_partner(wq[..., QK_NOPE:]), zq],
                              axis=-1).reshape(Q_LORA, H * HEAD_PAD)
    w_q = jnp.concatenate([wq_main, wq_part], axis=-1).T.astype(BF16)

    wkv = w_ukv[l].reshape(KV_LORA, H, QK_NOPE + V_DIM)
    wk_lat = jnp.concatenate([wkv[..., :QK_NOPE], jnp.zeros((KV_LORA, H, HEAD_PAD - QK_NOPE), F32)],
                             axis=-1).reshape(KV_LORA, H * HEAD_PAD)
    eye = jnp.eye(QK_ROPE, dtype=F32)
    place = jnp.concatenate([jnp.zeros((QK_ROPE, QK_NOPE), F32), eye, jnp.zeros((QK_ROPE, pad), F32)], axis=-1)
    place = jnp.tile(place, (1, H))
    w_k = jnp.concatenate([wk_lat, place, place, jnp.zeros((LANE - 2 * QK_ROPE, H * HEAD_PAD), F32)],
                          axis=0).astype(BF16)
    w_v = jnp.concatenate([wkv[..., QK_NOPE:], jnp.zeros((KV_LORA, H, HEAD_PAD - V_DIM), F32)],
                          axis=-1).reshape(KV_LORA, H * HEAD_PAD).T.astype(BF16)

    def block_diag(w):
        out = jnp.zeros((LRU_WIDTH, LRU_WIDTH), F32)
        for hd in range(LRU_HEADS):
            s = hd * LRU_HEAD_DIM
            out = out.at[s:s + LRU_HEAD_DIM, s:s + LRU_HEAD_DIM].set(w[hd])
        return out.astype(BF16)

    row = lambda a: a[l].reshape(1, -1)
    return {
        "w_in": w_in_p, "qg": row(q_norm_g), "w_q": w_q, "kvg": row(kv_norm_g), "w_k": w_k, "w_v": w_v,
        "ln_g": row(sgu_ln_g), "ln_b": row(sgu_ln_b),
        "w_s": sgu_w[l].reshape(SGU_GROUPS * SGU_BLOCK, SGU_BLOCK),
        "b_s": jnp.repeat(sgu_b[l].T, SGU_GROUP_DIM, axis=1),
        "conv_w": conv_w[l], "conv_b": row(conv_b), "wa": block_diag(lru_wa[l]), "ba": row(lru_ba),
        "wx": block_diag(lru_wx[l]), "bx": row(lru_bx), "lam": row(lru_lambda),
    }


def kernel(x, c, w_mod, b_mod, pre_mix_g, post_mix_g, w_in, q_norm_g, w_uq, kv_norm_g, w_ukv, sgu_ln_g,
           sgu_ln_b, sgu_w, sgu_b, conv_w, conv_b, lru_wa, lru_ba, lru_wx, lru_bx, lru_lambda, w_o,
           pre_ffn_g, post_ffn_g, ffn_w_gate, ffn_w_up, ffn_w_down, moe_router, moe_w_gate, moe_w_up,
           moe_w_down):
    B, S, D = x.shape
    L = w_mod.shape[0]
    N = B * S
    tabs = _rope_tables(S)
    mod = _modulation(c, w_mod, b_mod)
    xf = x.reshape(N, D)
    for l in range(L):
        sh_m, sc_m, g_m, sh_f, sc_f, g_f = [m.reshape(B, 1, D) for m in jnp.split(mod[l], 6, axis=-1)]
        p = _layer_params(l, w_in, q_norm_g, w_uq, kv_norm_g, w_ukv, sgu_ln_g, sgu_ln_b, sgu_w, sgu_b,
                          conv_w, conv_b, lru_wa, lru_ba, lru_wx, lru_bx, lru_lambda)
        q, k, v, ymix = _mix_in(xf.reshape(B, S, D), sh_m, sc_m, pre_mix_g[l].reshape(1, D), p, tabs)
        o = _attention(q, k, v)
        moe = l % 2 == 1
        router = None
        if moe:
            r = jnp.pad(moe_router[l // 2], ((0, 0), (0, LANE - N_EXPERTS)))
            r_hi = r.astype(BF16)
            router = jnp.concatenate([r_hi, (r - r_hi.astype(F32)).astype(BF16)], axis=-1)
        res = _mix_out(o.reshape(N, -1), ymix.reshape(N, -1), xf, w_o[l].astype(BF16),
                       post_mix_g[l].reshape(1, D), g_m, pre_ffn_g[l].reshape(1, D), sc_f, sh_f, router,
                       S // TM_OUT)
        post_g = post_ffn_g[l].reshape(1, D)
        if moe:
            xf, h2, comb = res
            xf = _moe(h2, xf, comb, moe_w_gate[l // 2].astype(BF16), moe_w_up[l // 2].astype(BF16),
                      moe_w_down[l // 2].astype(BF16), post_g, g_f, S // TM_FFN)
        else:
            xf, h2 = res
            xf = _ffn(h2, xf, ffn_w_gate[l // 2].astype(BF16), ffn_w_up[l // 2].astype(BF16),
                      ffn_w_down[l // 2].astype(BF16), post_g, g_f, S // TM_FFN)
    return xf.reshape(B, S, D)
```

```python
import functools

import jax
import jax.numpy as jnp
from jax import lax
from jax.experimental import pallas as pl
from jax.experimental.pallas import tpu as pltpu

F32 = jnp.float32
BF16 = jnp.bfloat16
HIGHEST = lax.Precision.HIGHEST

MLA_HEADS = 8
QK_NOPE = 64
QK_ROPE = 32
V_DIM = 64
Q_LORA = 256
KV_LORA = 128
ROPE_THETA = 10000.0
CHUNK = 64
SGU_GROUPS = 4
SGU_GROUP_DIM = 64
SGU_WIDTH = SGU_GROUPS * SGU_GROUP_DIM
SGU_BLOCK = 128
LRU_HEADS = 4
LRU_HEAD_DIM = 64
LRU_WIDTH = LRU_HEADS * LRU_HEAD_DIM
CONV_W = 4
LRU_C = 8.0
N_EXPERTS = 8
NEG_INF = -1e30
RMS_EPS = 1e-6
LN_EPS = 1e-5

LANE = 128
SUBLANE = 8
HEAD_PAD = LANE

C_Q = 0
C_KV = C_Q + Q_LORA
C_U = C_KV + KV_LORA
C_V = C_U + SGU_WIDTH
C_X = C_V + SGU_WIDTH
C_Y = C_X + LRU_WIDTH
C_KR = C_Y + LRU_WIDTH
D_IN_PAD = C_KR + LANE

VMEM_LIMIT = 56 * 1024 * 1024

TM_MIX = 512
TQ = 512
TK = 512
ATTN_HEADS_PER_STEP = 8
TM_OUT = 512
TM_FFN = 512
MOE_TILE = 512
MOE_DMA_BLOCK = 2048


def _rms(x, g):
    return x * lax.rsqrt(jnp.mean(x * x, axis=-1, keepdims=True) + RMS_EPS) * g


def _gelu(x):
    return 0.5 * x * (1.0 + jnp.tanh(0.7978845608028654 * (x + 0.044715 * (x * x * x))))


def _sigmoid(x):
    return 1.0 / (1.0 + jnp.exp(-x))


def _silu(x):
    return x * _sigmoid(x)


def _bdot(a, b):
    return jnp.dot(a, b, preferred_element_type=F32)


def _mod_kernel(c_ref, w_ref, b_ref, o_ref):
    c = c_ref[...]
    o_ref[0] = jnp.dot(_silu(c), w_ref[0], preferred_element_type=F32, precision=HIGHEST) + b_ref[0]


def _modulation(c, w_mod, b_mod):
    L, D, D6 = w_mod.shape
    B = c.shape[0]
    return pl.pallas_call(
        _mod_kernel,
        grid=(L, D6 // D),
        in_specs=[
            pl.BlockSpec((B, D), lambda l, j: (0, 0)),
            pl.BlockSpec((1, D, D), lambda l, j: (l, 0, j)),
            pl.BlockSpec((1, 1, D), lambda l, j: (l, 0, j)),
        ],
        out_specs=pl.BlockSpec((1, B, D), lambda l, j: (l, 0, j)),
        out_shape=jax.ShapeDtypeStruct((L, B, D6), F32),
        compiler_params=pltpu.CompilerParams(vmem_limit_bytes=VMEM_LIMIT),
        name="modulation",
    )(c, w_mod, b_mod.reshape(L, 1, D6))


def _shift_rows(cur, prev8, k):
    rolled = pltpu.roll(cur, k, 0)
    fix = pltpu.roll(prev8, k, 0)
    row = lax.broadcasted_iota(jnp.int32, fix.shape, 0)
    top = jnp.where(row < k, fix, rolled[:SUBLANE])
    return jnp.concatenate([top, rolled[SUBLANE:]], axis=0)


def _mix_in_kernel(x_ref, sh_ref, sc_ref, preg_ref, win_ref, qg_ref, wq_ref, kvg_ref, wk_ref, wv_ref,
                   cosq_ref, sinq_ref, tk_ref, lng_ref, lnb_ref, ws_ref, bs_ref,
                   cw_ref, cb_ref, wa_ref, ba_ref, wx_ref, bx_ref, lam_ref,
                   q_ref, k_ref, v_ref, y_ref, prev_ref, hc_ref):
    tm = x_ref.shape[1]

    @pl.when(pl.program_id(1) == 0)
    def _():
        prev_ref[...] = jnp.zeros_like(prev_ref)
        hc_ref[...] = jnp.zeros_like(hc_ref)

    x = x_ref[0]
    h = _rms(x, preg_ref[...]) * (1.0 + sc_ref[0]) + sh_ref[0]
    z = _bdot(h.astype(BF16), win_ref[...])

    nt = (((1,), (1,)), ((), ()))
    qn = _rms(z[:, C_Q:C_Q + Q_LORA], qg_ref[...]).astype(BF16)
    qq = lax.dot_general(wq_ref[...], qn, nt, preferred_element_type=F32)
    cosq = cosq_ref[...]
    sinq = sinq_ref[...]
    qw = MLA_HEADS * HEAD_PAD
    for hd in range(MLA_HEADS):
        lo = hd * HEAD_PAD
        qh = qq[lo:lo + HEAD_PAD] * cosq + qq[qw + lo:qw + lo + HEAD_PAD] * sinq
        q_ref[0, lo:lo + HEAD_PAD, :] = qh.astype(BF16)
    kvn = _rms(z[:, C_KV:C_KV + KV_LORA], kvg_ref[...]).astype(BF16)
    kr = (z[:, C_KR:C_KR + LANE] * tk_ref[...]).astype(BF16)
    k_in = jnp.concatenate([kvn, kr], axis=-1)
    k_ref[0] = _bdot(k_in, wk_ref[...]).astype(BF16)
    vv = lax.dot_general(wv_ref[...], kvn, nt, preferred_element_type=F32)
    ones_row = lax.broadcasted_iota(jnp.int32, vv.shape, 0) % HEAD_PAD == V_DIM
    v_ref[0, 0] = jnp.where(ones_row, 1.0, vv).astype(BF16)

    u = _gelu(z[:, C_U:C_U + SGU_WIDTH])
    gv = _gelu(z[:, C_V:C_V + SGU_WIDTH])
    mu = jnp.mean(gv, axis=-1, keepdims=True)
    var = jnp.mean(jnp.square(gv - mu), axis=-1, keepdims=True)
    vn = ((gv - mu) * lax.rsqrt(var + LN_EPS) * lng_ref[...] + lnb_ref[...]).astype(BF16)
    ws = ws_ref[...]
    r_t = lax.broadcasted_iota(jnp.int32, ws.shape, 0) % SGU_BLOCK
    c_s = lax.broadcasted_iota(jnp.int32, ws.shape, 1)
    ws = jnp.where(c_s <= r_t, ws, 0.0).astype(BF16)
    grp = lax.broadcasted_iota(jnp.int32, (SGU_BLOCK, SGU_WIDTH), 1) // SGU_GROUP_DIM
    for blk in range(tm // SGU_BLOCK):
        r0 = blk * SGU_BLOCK
        res = _bdot(ws, vn[r0:r0 + SGU_BLOCK])
        mixed = bs_ref[...]
        for g in range(SGU_GROUPS):
            mixed = mixed + jnp.where(grp == g, res[g * SGU_BLOCK:(g + 1) * SGU_BLOCK], 0.0)
        y_ref[0, r0:r0 + SGU_BLOCK, 0:SGU_WIDTH] = (u[r0:r0 + SGU_BLOCK] * mixed).astype(BF16)

    zx = z[:, C_X:C_X + LRU_WIDTH]
    prev8 = prev_ref[...]
    cw = cw_ref[...]
    xc = cb_ref[...] + zx * cw[CONV_W - 1:CONV_W]
    for k in range(1, CONV_W):
        xc = xc + _shift_rows(zx, prev8, k) * cw[CONV_W - 1 - k:CONV_W - k]
    prev_ref[...] = zx[tm - SUBLANE:]
    xcb = xc.astype(BF16)
    r = _sigmoid(_bdot(xcb, wa_ref[...]) + ba_ref[...])
    ig = _sigmoid(_bdot(xcb, wx_ref[...]) + bx_ref[...])
    lam = lam_ref[...]
    softplus_neg_lam = jnp.maximum(-lam, 0.0) + jnp.log(1.0 + jnp.exp(-jnp.abs(lam)))
    log_a = -LRU_C * r * softplus_neg_lam
    a = jnp.exp(log_a)
    t = jnp.tanh(log_a)
    b = jnp.sqrt(-2.0 * t / (1.0 - t)) * (ig * xc)
    row = lax.broadcasted_iota(jnp.int32, a.shape, 0)
    k = 1
    while k < tm:
        valid = row >= k
        a_sh = pltpu.roll(a, k, 0)
        b_sh = pltpu.roll(b, k, 0)
        b = jnp.where(valid, b + a * b_sh, b)
        a = jnp.where(valid, a * a_sh, a)
        k *= 2
    hs = b + a * hc_ref[0:1]
    hc_ref[0:1] = hs[tm - 1:tm]
    y_ref[0, :, SGU_WIDTH:SGU_WIDTH + LRU_WIDTH] = (hs * _gelu(z[:, C_Y:C_Y + LRU_WIDTH])).astype(BF16)


def _mix_in(x, sh, sc, pre_g, p, tabs):
    B, S, D = x.shape
    tm = TM_MIX
    full = lambda a: pl.BlockSpec(a.shape, lambda b, s: (0,) * a.ndim)
    per_b = pl.BlockSpec((1, 1, D), lambda b, s: (b, 0, 0))
    tab = pl.BlockSpec((tm, LANE), lambda b, s: (s, 0))
    tab_t = pl.BlockSpec((HEAD_PAD, tm), lambda b, s: (0, s))
    assert tm == TK
    consts = [p["w_in"], p["qg"], p["w_q"], p["kvg"], p["w_k"], p["w_v"]]
    consts2 = [p["ln_g"], p["ln_b"], p["w_s"], p["b_s"], p["conv_w"], p["conv_b"], p["wa"], p["ba"],
               p["wx"], p["bx"], p["lam"]]
    qw = MLA_HEADS * HEAD_PAD
    return pl.pallas_call(
        _mix_in_kernel,
        grid=(B, S // tm),
        in_specs=[pl.BlockSpec((1, tm, D), lambda b, s: (b, s, 0)), per_b, per_b, full(pre_g)]
        + [full(a) for a in consts] + [tab_t, tab_t, tab] + [full(a) for a in consts2],
        out_specs=[
            pl.BlockSpec((1, qw, tm), lambda b, s: (b, 0, s)),
            pl.BlockSpec((1, tm, qw), lambda b, s: (b, s, 0)),
            pl.BlockSpec((1, 1, qw, tm), lambda b, s: (b, s, 0, 0)),
            pl.BlockSpec((1, tm, SGU_WIDTH + LRU_WIDTH), lambda b, s: (b, s, 0)),
        ],
        out_shape=[
            jax.ShapeDtypeStruct((B, qw, S), BF16),
            jax.ShapeDtypeStruct((B, S, qw), BF16),
            jax.ShapeDtypeStruct((B, S // tm, qw, tm), BF16),
            jax.ShapeDtypeStruct((B, S, SGU_WIDTH + LRU_WIDTH), BF16),
        ],
        scratch_shapes=[pltpu.VMEM((SUBLANE, LRU_WIDTH), F32), pltpu.VMEM((SUBLANE, LRU_WIDTH), F32)],
        compiler_params=pltpu.CompilerParams(
            dimension_semantics=("arbitrary", "arbitrary"), vmem_limit_bytes=VMEM_LIMIT),
        name="mix_in",
    )(x, sh, sc, pre_g, *consts, tabs["cosq"], tabs["sinq"], tabs["tk"], *consts2)


def _attn_kernel(q_ref, k_ref, v_ref, o_ref, m_ref, acc_ref, p_ref, a_ref):
    i = pl.program_id(2)
    heads = q_ref.shape[1] // HEAD_PAD
    feat = lambda hd: slice(hd * HEAD_PAD, (hd + 1) * HEAD_PAD)

    m_ref[...] = jnp.full(m_ref.shape, NEG_INF, F32)
    acc_ref[...] = jnp.zeros(acc_ref.shape, F32)
    p_ref[...] = jnp.zeros(p_ref.shape, BF16)
    a_ref[...] = jnp.ones(a_ref.shape, F32)

    def scores(t, hd):
        keys = pl.ds(pl.multiple_of(t * TK, TK), TK)
        return _bdot(k_ref[0, keys, feat(hd)], q_ref[0, feat(hd), :])

    def weights(s, hd):
        m = m_ref[hd]
        m_new = jnp.maximum(m, jnp.max(s, axis=0, keepdims=True))
        p_ref[hd] = jnp.exp2(s - m_new).astype(BF16)
        a_ref[hd] = jnp.exp2(m - m_new)
        m_ref[hd] = m_new

    def accumulate(t, hd):
        acc_ref[hd] = a_ref[hd] * acc_ref[hd] + _bdot(v_ref[0, t, feat(hd), :], p_ref[hd])

    def tile(t, masked):
        ss = [scores(t, hd) for hd in range(heads)]
        for hd in range(heads):
            s = ss[hd]
            if masked:
                kc = lax.broadcasted_iota(jnp.int32, s.shape, 0) // CHUNK
                qc = lax.broadcasted_iota(jnp.int32, s.shape, 1) // CHUNK
                s = jnp.where(kc <= qc, s, NEG_INF)
            weights(s, hd)
            accumulate(t, hd)

    def body(t, c):
        tile(t, False)
        return c

    lax.fori_loop(0, i, body, 0)
    tile(i, True)
    outs = []
    for hd in range(heads):
        acc = acc_ref[hd]
        outs.append((acc / acc[V_DIM:V_DIM + 1]).T[:, :V_DIM])
    o_ref[0] = jnp.concatenate(outs, axis=-1).astype(BF16)


def _attention(q, k, v):
    B, S, _ = k.shape
    hp = ATTN_HEADS_PER_STEP
    assert TQ == TK
    return pl.pallas_call(
        _attn_kernel,
        grid=(B, MLA_HEADS // hp, S // TQ),
        in_specs=[
            pl.BlockSpec((1, hp * HEAD_PAD, TQ), lambda b, h, i: (b, h, i)),
            pl.BlockSpec((1, S, hp * HEAD_PAD), lambda b, h, i: (b, 0, h)),
            pl.BlockSpec((1, S // TK, hp * HEAD_PAD, TK), lambda b, h, i: (b, 0, h, 0)),
        ],
        out_specs=pl.BlockSpec((1, TQ, hp * V_DIM), lambda b, h, i: (b, i, h)),
        out_shape=jax.ShapeDtypeStruct((B, S, MLA_HEADS * V_DIM), BF16),
        scratch_shapes=[
            pltpu.VMEM((hp, 1, TQ), F32),
            pltpu.VMEM((hp, HEAD_PAD, TQ), F32),
            pltpu.VMEM((hp, TK, TQ), BF16),
            pltpu.VMEM((hp, 1, TQ), F32),
        ],
        compiler_params=pltpu.CompilerParams(
            dimension_semantics=("arbitrary", "arbitrary", "arbitrary"), vmem_limit_bytes=VMEM_LIMIT),
        name="attention",
    )(q, k, v)


def _mix_out_kernel(o_ref, y_ref, x_ref, wo_ref, postg_ref, gm_ref, preg_ref, scf_ref, shf_ref,
                    xo_ref, h_ref):
    no = o_ref.shape[1]
    y = _bdot(o_ref[...], wo_ref[0:no]) + _bdot(y_ref[...], wo_ref[no:])
    x = x_ref[...] + gm_ref[0] * _rms(y, postg_ref[...])
    xo_ref[...] = x
    h_ref[...] = (_rms(x, preg_ref[...]) * (1.0 + scf_ref[0]) + shf_ref[0]).astype(BF16)


def _store_token_tiles(ref, x):
    rows, d = x.shape
    assert d == SUBLANE * LANE
    for j in range(SUBLANE):
        ref[pl.ds(j, rows, stride=SUBLANE), :] = x[:, j * LANE:(j + 1) * LANE]


def _load_token_tiles(ref):
    rows = ref.shape[0] // SUBLANE
    return jnp.concatenate([ref[pl.ds(j, rows, stride=SUBLANE), :] for j in range(SUBLANE)], axis=-1)


def _token_copy(src, dst, src_token, dst_token, sem):
    def rows(t):
        start = t * SUBLANE
        return pl.ds(start if isinstance(t, int) else pl.multiple_of(start, SUBLANE), SUBLANE)

    return pltpu.make_async_copy(src.at[rows(src_token)], dst.at[rows(dst_token)], sem)


def _mix_out_moe_kernel(o_ref, y_ref, x_ref, wo_ref, postg_ref, gm_ref, preg_ref, scf_ref, shf_ref,
                        router_ref, xo_ref, h_ref, route_ref):
    no = o_ref.shape[1]
    y = _bdot(o_ref[...], wo_ref[0:no]) + _bdot(y_ref[...], wo_ref[no:])
    x = x_ref[...] + gm_ref[0] * _rms(y, postg_ref[...])
    xo_ref[...] = x
    h = _rms(x, preg_ref[...]) * (1.0 + scf_ref[0]) + shf_ref[0]
    _store_token_tiles(h_ref, h)
    h_hi = h.astype(BF16)
    h_lo = (h - h_hi.astype(F32)).astype(BF16)
    t = _bdot(h_hi, router_ref[...])
    logits = t[:, :LANE] + t[:, LANE:] + _bdot(h_lo, router_ref[:, :LANE])
    lane = lax.broadcasted_iota(jnp.int32, logits.shape, 1)
    logits = jnp.where(lane < N_EXPERTS, logits, -jnp.inf)
    m1 = jnp.max(logits, axis=-1, keepdims=True)
    i1 = jnp.min(jnp.where(logits == m1, lane, LANE), axis=-1, keepdims=True)
    rest = jnp.where(lane == i1, -jnp.inf, logits)
    m2 = jnp.max(rest, axis=-1, keepdims=True)
    i2 = jnp.min(jnp.where(rest == m2, lane, LANE), axis=-1, keepdims=True)
    e = jnp.exp(m2 - m1)
    g1 = 1.0 / (1.0 + e)
    route_ref[...] = (jnp.where(lane == 0, i1.astype(F32), 0.0) + jnp.where(lane == 1, i2.astype(F32), 0.0)
                      + jnp.where(lane == 2, g1, 0.0) + jnp.where(lane == 3, e * g1, 0.0))


def _mix_out(o, y, x, w_o, post_g, g_m, pre_g, sc_f, sh_f, router, tiles_per_batch):
    N, D = x.shape
    tm = TM_OUT
    row = lambda w: pl.BlockSpec((tm, w), lambda i: (i, 0))
    full = lambda a: pl.BlockSpec(a.shape, lambda i: (0,) * a.ndim)
    per_b = pl.BlockSpec((1, 1, D), lambda i: (i // tiles_per_batch, 0, 0))
    in_specs = [row(o.shape[1]), row(y.shape[1]), row(D), full(w_o), full(post_g), per_b, full(pre_g), per_b, per_b]
    out_specs = [row(D), row(D)]
    out_shape = [jax.ShapeDtypeStruct((N, D), F32), jax.ShapeDtypeStruct((N, D), BF16)]
    args = [o, y, x, w_o, post_g, g_m, pre_g, sc_f, sh_f]
    body = _mix_out_kernel
    if router is not None:
        in_specs.append(full(router))
        out_specs = [row(D), pl.BlockSpec((tm * SUBLANE, LANE), lambda i: (i, 0)), row(LANE)]
        out_shape = [jax.ShapeDtypeStruct((N, D), F32), jax.ShapeDtypeStruct((N * SUBLANE, LANE), F32),
                     jax.ShapeDtypeStruct((N, LANE), F32)]
        args.append(router)
        body = _mix_out_moe_kernel
    return pl.pallas_call(
        body,
        grid=(N // tm,),
        in_specs=in_specs,
        out_specs=out_specs,
        out_shape=out_shape,
        compiler_params=pltpu.CompilerParams(dimension_semantics=("arbitrary",), vmem_limit_bytes=VMEM_LIMIT),
        name="mix_out",
    )(*args)


def _swiglu(h, wg, wu, wd):
    g = _bdot(h, wg)
    u = _bdot(h, wu)
    return _bdot((_silu(g) * u).astype(BF16), wd)


def _ffn_kernel(h_ref, x_ref, wg_ref, wu_ref, wd_ref, postg_ref, gf_ref, o_ref, acc_ref):
    j = pl.program_id(1)

    @pl.when(j == 0)
    def _():
        acc_ref[...] = jnp.zeros_like(acc_ref)

    acc_ref[...] += _swiglu(h_ref[...], wg_ref[...], wu_ref[...], wd_ref[...])

    @pl.when(j == pl.num_programs(1) - 1)
    def _():
        o_ref[...] = x_ref[...] + gf_ref[0] * _rms(acc_ref[...], postg_ref[...])


def _ffn(h, x, wg, wu, wd, post_g, g_f, tiles_per_batch):
    N, D = x.shape
    FF = wg.shape[1]
    tm = TM_FFN
    tf = FF // 2
    return pl.pallas_call(
        _ffn_kernel,
        grid=(N // tm, FF // tf),
        in_specs=[
            pl.BlockSpec((tm, D), lambda i, j: (i, 0)),
            pl.BlockSpec((tm, D), lambda i, j: (i, 0)),
            pl.BlockSpec((D, tf), lambda i, j: (0, j)),
            pl.BlockSpec((D, tf), lambda i, j: (0, j)),
            pl.BlockSpec((tf, D), lambda i, j: (j, 0)),
            pl.BlockSpec((1, D), lambda i, j: (0, 0)),
            pl.BlockSpec((1, 1, D), lambda i, j: (i // tiles_per_batch, 0, 0)),
        ],
        out_specs=pl.BlockSpec((tm, D), lambda i, j: (i, 0)),
        out_shape=jax.ShapeDtypeStruct((N, D), F32),
        scratch_shapes=[pltpu.VMEM((tm, D), F32)],
        compiler_params=pltpu.CompilerParams(
            dimension_semantics=("arbitrary", "arbitrary"), vmem_limit_bytes=VMEM_LIMIT),
        name="ffn_dense",
    )(h, x, wg, wu, wd, post_g, g_f)


def _moe_dispatch_kernel(pad_lo_ref, pad_hi_ref, slot_ref, h_ref, hs_hbm, zero_ref, sem):
    tm = h_ref.shape[0] // SUBLANE

    @pl.when(pl.program_id(0) == 0)
    def _():
        zero_ref[...] = jnp.zeros(zero_ref.shape, zero_ref.dtype)
        for e in range(pad_lo_ref.shape[0]):
            lo, hi = pad_lo_ref[e], pad_hi_ref[e]

            def start_pad(s, c):
                _token_copy(zero_ref, hs_hbm, 0, s, sem).start()
                return c

            def wait_pad(s, c):
                _token_copy(zero_ref, hs_hbm, 0, s, sem).wait()
                return c

            lax.fori_loop(lo, hi, start_pad, 0)
            lax.fori_loop(lo, hi, wait_pad, 0)

    def start(r, c):
        for k in range(2):
            _token_copy(h_ref, hs_hbm, r, slot_ref[0, 0, 2 * r + k], sem).start(priority=k)
        return c

    lax.fori_loop(0, tm, start, 0, unroll=4)

    def wait(r, c):
        for k in range(2):
            _token_copy(h_ref, hs_hbm, r, 0, sem).wait()
        return c

    lax.fori_loop(0, tm, wait, 0, unroll=4)


def _moe_dispatch(pad_lo, pad_hi, slots, h_tiles, n_slots):
    N = h_tiles.shape[0] // SUBLANE
    tm = MOE_DMA_BLOCK
    return pl.pallas_call(
        _moe_dispatch_kernel,
        grid_spec=pltpu.PrefetchScalarGridSpec(
            num_scalar_prefetch=2,
            grid=(N // tm,),
            in_specs=[
                pl.BlockSpec((1, 1, 2 * tm), lambda i, lo, hi: (i, 0, 0), memory_space=pltpu.SMEM),
                pl.BlockSpec((tm * SUBLANE, LANE), lambda i, lo, hi: (i, 0)),
            ],
            out_specs=pl.BlockSpec(memory_space=pl.ANY),
            scratch_shapes=[pltpu.VMEM((SUBLANE, LANE), F32), pltpu.SemaphoreType.DMA(())],
        ),
        out_shape=jax.ShapeDtypeStruct((n_slots * SUBLANE, LANE), F32),
        compiler_params=pltpu.CompilerParams(dimension_semantics=("arbitrary",)),
        name="moe_dispatch",
    )(pad_lo, pad_hi, slots.reshape(N // tm, 1, 2 * tm), h_tiles)


def _moe_expert_kernel(texp_ref, nused_ref, hs_ref, wg_ref, wu_ref, wd_ref, ys_ref):
    del texp_ref

    @pl.when(pl.program_id(0) < nused_ref[0])
    def _():
        _store_token_tiles(ys_ref, _swiglu(_load_token_tiles(hs_ref).astype(BF16), wg_ref[0], wu_ref[0], wd_ref[0]))

    @pl.when(pl.program_id(0) >= nused_ref[0])
    def _():
        ys_ref[...] = jnp.zeros(ys_ref.shape, F32)


def _moe_experts(tile_expert, n_used, hs, wg, wu, wd):
    n_slots = hs.shape[0] // SUBLANE
    E, D, FF = wg.shape
    t = MOE_TILE
    tile = pl.BlockSpec((t * SUBLANE, LANE), lambda i, te, nu: (i, 0))
    return pl.pallas_call(
        _moe_expert_kernel,
        grid_spec=pltpu.PrefetchScalarGridSpec(
            num_scalar_prefetch=2,
            grid=(n_slots // t,),
            in_specs=[
                tile,
                pl.BlockSpec((1, D, FF), lambda i, te, nu: (te[i], 0, 0)),
                pl.BlockSpec((1, D, FF), lambda i, te, nu: (te[i], 0, 0)),
                pl.BlockSpec((1, FF, D), lambda i, te, nu: (te[i], 0, 0)),
            ],
            out_specs=tile,
        ),
        out_shape=jax.ShapeDtypeStruct(hs.shape, F32),
        compiler_params=pltpu.CompilerParams(dimension_semantics=("arbitrary",), vmem_limit_bytes=VMEM_LIMIT),
        name="moe_experts",
    )(tile_expert, n_used, hs, wg, wu, wd)


def _moe_combine_kernel(slot_ref, ys_hbm, route_ref, x_ref, postg_ref, gf_ref, o_ref, buf_ref, sem):
    tm = x_ref.shape[0]

    def start(r, c):
        for k in range(2):
            _token_copy(ys_hbm, buf_ref.at[k], slot_ref[0, 0, 2 * r + k], r, sem).start(priority=k)
        return c

    lax.fori_loop(0, tm, start, 0, unroll=4)

    def wait(r, c):
        for k in range(2):
            _token_copy(ys_hbm, buf_ref.at[k], 0, r, sem).wait()
        return c

    lax.fori_loop(0, tm, wait, 0, unroll=4)
    route = route_ref[...]
    lane = lax.broadcasted_iota(jnp.int32, route.shape, 1)
    f = None
    for k in range(2):
        gate = jnp.sum(jnp.where(lane == 2 + k, route, 0.0), axis=-1, keepdims=True)
        y = _load_token_tiles(buf_ref.at[k])
        f = gate * y if f is None else f + gate * y
    o_ref[...] = x_ref[...] + gf_ref[0] * _rms(f, postg_ref[...])


def _moe_combine(slots, ys, route, x, post_g, g_f, tiles_per_batch):
    N, D = x.shape
    tm = TM_FFN
    return pl.pallas_call(
        _moe_combine_kernel,
        grid=(N // tm,),
        in_specs=[
            pl.BlockSpec((1, 1, 2 * tm), lambda i: (i, 0, 0), memory_space=pltpu.SMEM),
            pl.BlockSpec(memory_space=pl.ANY),
            pl.BlockSpec((tm, LANE), lambda i: (i, 0)),
            pl.BlockSpec((tm, D), lambda i: (i, 0)),
            pl.BlockSpec((1, D), lambda i: (0, 0)),
            pl.BlockSpec((1, 1, D), lambda i: (i // tiles_per_batch, 0, 0)),
        ],
        out_specs=pl.BlockSpec((tm, D), lambda i: (i, 0)),
        out_shape=jax.ShapeDtypeStruct((N, D), F32),
        scratch_shapes=[pltpu.VMEM((2, tm * SUBLANE, LANE), F32), pltpu.SemaphoreType.DMA(())],
        compiler_params=pltpu.CompilerParams(dimension_semantics=("arbitrary",), vmem_limit_bytes=VMEM_LIMIT),
        name="moe_combine",
    )(slots.reshape(N // tm, 1, 2 * tm), ys, route, x, post_g, g_f)


def _moe_plan(route, n_slots):
    t = MOE_TILE
    pair_expert = route[:, :2].astype(jnp.int32).reshape(-1)
    onehot = (pair_expert[:, None] == jnp.arange(N_EXPERTS, dtype=jnp.int32)[None, :]).astype(jnp.int32)
    csum = jnp.cumsum(onehot, axis=0)
    rank = jnp.take_along_axis(csum, pair_expert[:, None], axis=1)[:, 0] - 1
    count = csum[-1]
    padded = (count + t - 1) // t * t
    ends = jnp.cumsum(padded)
    starts = ends - padded
    slot = starts[pair_expert] + rank
    n_used = ends[-1:] // t
    tile_start = jnp.arange(n_slots // t, dtype=jnp.int32) * t
    tile_expert = jnp.sum((tile_start[:, None] >= ends[None, :]).astype(jnp.int32), axis=1)
    last_used = tile_expert[jnp.maximum(n_used[0] - 1, 0)]
    tile_expert = jnp.where(tile_start < ends[-1], tile_expert, last_used)
    pad_lo = jnp.concatenate([starts + count, ends[-1:]]).astype(jnp.int32)
    pad_hi = jnp.concatenate([ends, jnp.full((1,), n_slots, ends.dtype)]).astype(jnp.int32)
    return slot, pad_lo, pad_hi, tile_expert, n_used.astype(jnp.int32)


def _moe(h_packed, x, route, wg, wu, wd, post_g, g_f, tiles_per_batch):
    N = x.shape[0]
    n_slots = 2 * N + N_EXPERTS * MOE_TILE
    slot, pad_lo, pad_hi, tile_expert, n_used = _moe_plan(route, n_slots)
    hs = _moe_dispatch(pad_lo, pad_hi, slot, h_packed, n_slots)
    ys = _moe_experts(tile_expert, n_used, hs, wg, wu, wd)
    return _moe_combine(slot, ys, route, x, post_g, g_f, tiles_per_batch)


def _rope_partner(w):
    half = QK_ROPE // 2
    return jnp.concatenate([-w[..., half:], w[..., :half]], axis=-1)


def _rope_tables(S):
    pos = jnp.arange(S, dtype=F32)
    inv = ROPE_THETA ** (-jnp.arange(0, QK_ROPE, 2, dtype=F32) / QK_ROPE)
    ang = pos[:, None] * inv[None, :]
    cos, sin = jnp.cos(ang), jnp.sin(ang)
    cos2 = jnp.concatenate([cos, cos], axis=-1)
    sin2 = jnp.concatenate([sin, sin], axis=-1)
    scale = (QK_NOPE + QK_ROPE) ** -0.5 * 1.4426950408889634
    pad = HEAD_PAD - QK_NOPE - QK_ROPE
    cosq = jnp.concatenate([jnp.ones((S, QK_NOPE), F32), cos2, jnp.zeros((S, pad), F32)], axis=-1) * scale
    sinq = jnp.concatenate([jnp.zeros((S, QK_NOPE), F32), sin2, jnp.zeros((S, pad), F32)], axis=-1) * scale
    tk = jnp.concatenate([cos2, sin2, jnp.zeros((S, LANE - 2 * QK_ROPE), F32)], axis=-1)
    return {"cosq": cosq.T, "sinq": sinq.T, "tk": tk}


def _layer_params(l, w_in, q_norm_g, w_uq, kv_norm_g, w_ukv, sgu_ln_g, sgu_ln_b, sgu_w, sgu_b,
                  conv_w, conv_b, lru_wa, lru_ba, lru_wx, lru_bx, lru_lambda):
    D = w_in.shape[1]
    wi = w_in[l]
    o_q, o_kv = 0, Q_LORA
    o_kr = o_kv + KV_LORA
    o_u = o_kr + QK_ROPE
    o_v = o_u + SGU_WIDTH
    o_x = o_v + SGU_WIDTH
    o_y = o_x + LRU_WIDTH
    w_kr = wi[:, o_kr:o_kr + QK_ROPE]
    w_in_p = jnp.concatenate([
        wi[:, o_q:o_q + Q_LORA], wi[:, o_kv:o_kv + KV_LORA], wi[:, o_u:o_u + SGU_WIDTH],
        wi[:, o_v:o_v + SGU_WIDTH], wi[:, o_x:o_x + LRU_WIDTH], wi[:, o_y:o_y + LRU_WIDTH],
        w_kr, _rope_partner(w_kr), jnp.zeros((D, LANE - 2 * QK_ROPE), F32)], axis=-1).astype(BF16)

    H = MLA_HEADS
    pad = HEAD_PAD - QK_NOPE - QK_ROPE
    wq = w_uq[l].reshape(Q_LORA, H, QK_NOPE + QK_ROPE)
    zq = jnp.zeros((Q_LORA, H, pad), F32)
    wq_main = jnp.concatenate([wq, zq], axis=-1).reshape(Q_LORA, H * HEAD_PAD)
    wq_part = jnp.concatenate([jnp.zeros((Q_LORA, H, QK_NOPE), F32), _rope_partner(wq[..., QK_NOPE:]), zq],
                              axis=-1).reshape(Q_LORA, H * HEAD_PAD)
    w_q = jnp.concatenate([wq_main, wq_part], axis=-1).T.astype(BF16)

    wkv = w_ukv[l].reshape(KV_LORA, H, QK_NOPE + V_DIM)
    wk_lat = jnp.concatenate([wkv[..., :QK_NOPE], jnp.zeros((KV_LORA, H, HEAD_PAD - QK_NOPE), F32)],
                             axis=-1).reshape(KV_LORA, H * HEAD_PAD)
    eye = jnp.eye(QK_ROPE, dtype=F32)
    place = jnp.concatenate([jnp.zeros((QK_ROPE, QK_NOPE), F32), eye, jnp.zeros((QK_ROPE, pad), F32)], axis=-1)
    place = jnp.tile(place, (1, H))
    w_k = jnp.concatenate([wk_lat, place, place, jnp.zeros((LANE - 2 * QK_ROPE, H * HEAD_PAD), F32)],
                          axis=0).astype(BF16)
    w_v = jnp.concatenate([wkv[..., QK_NOPE:], jnp.zeros((KV_LORA, H, HEAD_PAD - V_DIM), F32)],
                          axis=-1).reshape(KV_LORA, H * HEAD_PAD).T.astype(BF16)

    def block_diag(w):
        out = jnp.zeros((LRU_WIDTH, LRU_WIDTH), F32)
        for hd in range(LRU_HEADS):
            s = hd * LRU_HEAD_DIM
            out = out.at[s:s + LRU_HEAD_DIM, s:s + LRU_HEAD_DIM].set(w[hd])
        return out.astype(BF16)

    row = lambda a: a[l].reshape(1, -1)
    return {
        "w_in": w_in_p, "qg": row(q_norm_g), "w_q": w_q, "kvg": row(kv_norm_g), "w_k": w_k, "w_v": w_v,
        "ln_g": row(sgu_ln_g), "ln_b": row(sgu_ln_b),
        "w_s": sgu_w[l].reshape(SGU_GROUPS * SGU_BLOCK, SGU_BLOCK),
        "b_s": jnp.repeat(sgu_b[l].T, SGU_GROUP_DIM, axis=1),
        "conv_w": conv_w[l], "conv_b": row(conv_b), "wa": block_diag(lru_wa[l]), "ba": row(lru_ba),
        "wx": block_diag(lru_wx[l]), "bx": row(lru_bx), "lam": row(lru_lambda),
    }


def kernel(x, c, w_mod, b_mod, pre_mix_g, post_mix_g, w_in, q_norm_g, w_uq, kv_norm_g, w_ukv, sgu_ln_g,
           sgu_ln_b, sgu_w, sgu_b, conv_w, conv_b, lru_wa, lru_ba, lru_wx, lru_bx, lru_lambda, w_o,
           pre_ffn_g, post_ffn_g, ffn_w_gate, ffn_w_up, ffn_w_down, moe_router, moe_w_gate, moe_w_up,
           moe_w_down):
    B, S, D = x.shape
    L = w_mod.shape[0]
    N = B * S
    tabs = _rope_tables(S)
    mod = _modulation(c, w_mod, b_mod)
    xf = x.reshape(N, D)
    for l in range(L):
        sh_m, sc_m, g_m, sh_f, sc_f, g_f = [m.reshape(B, 1, D) for m in jnp.split(mod[l], 6, axis=-1)]
        p = _layer_params(l, w_in, q_norm_g, w_uq, kv_norm_g, w_ukv, sgu_ln_g, sgu_ln_b, sgu_w, sgu_b,
                          conv_w, conv_b, lru_wa, lru_ba, lru_wx, lru_bx, lru_lambda)
        q, k, v, ymix = _mix_in(xf.reshape(B, S, D), sh_m, sc_m, pre_mix_g[l].reshape(1, D), p, tabs)
        o = _attention(q, k, v)
        moe = l % 2 == 1
        router = None
        if moe:
            r = jnp.pad(moe_router[l // 2], ((0, 0), (0, LANE - N_EXPERTS)))
            r_hi = r.astype(BF16)
            router = jnp.concatenate([r_hi, (r - r_hi.astype(F32)).astype(BF16)], axis=-1)
        res = _mix_out(o.reshape(N, -1), ymix.reshape(N, -1), xf, w_o[l].astype(BF16),
                       post_mix_g[l].reshape(1, D), g_m, pre_ffn_g[l].reshape(1, D), sc_f, sh_f, router,
                       S // TM_OUT)
        post_g = post_ffn_g[l].reshape(1, D)
        if moe:
            xf, h2, comb = res
            xf = _moe(h2, xf, comb, moe_w_gate[l // 2].astype(BF16), moe_w_up[l // 2].astype(BF16),
                      moe_w_down[l // 2].astype(BF16), post_g, g_f, S // TM_FFN)
        else:
            xf, h2 = res
            xf = _ffn(h2, xf, ffn_w_gate[l // 2].astype(BF16), ffn_w_up[l // 2].astype(BF16),
                      ffn_w_down[l // 2].astype(BF16), post_g, g_f, S // TM_FFN)
    return xf.reshape(B, S, D)
```

```python
import jax
import jax.numpy as jnp
from jax import lax
from jax.experimental import pallas as pl
from jax.experimental.pallas import tpu as pltpu

F32 = jnp.float32
BF16 = jnp.bfloat16
HIGHEST = lax.Precision.HIGHEST

MLA_HEADS = 8
QK_NOPE = 64
QK_ROPE = 32
V_DIM = 64
Q_LORA = 256
KV_LORA = 128
ROPE_THETA = 10000.0
CHUNK = 64
SGU_GROUPS = 4
SGU_GROUP_DIM = 64
SGU_WIDTH = SGU_GROUPS * SGU_GROUP_DIM
SGU_BLOCK = 128
LRU_HEADS = 4
LRU_HEAD_DIM = 64
LRU_WIDTH = LRU_HEADS * LRU_HEAD_DIM
CONV_W = 4
LRU_C = 8.0
N_EXPERTS = 8
NEG_INF = -1e30
RMS_EPS = 1e-6
LN_EPS = 1e-5

LANE = 128
SUBLANE = 8
HEAD_PAD = LANE
Q_SCALE = (QK_NOPE + QK_ROPE) ** -0.5 * 1.4426950408889634

C_Q = 0
C_KV = C_Q + Q_LORA
C_U = C_KV + KV_LORA
C_V = C_U + SGU_WIDTH
C_X = C_V + SGU_WIDTH
C_Y = C_X + LRU_WIDTH
C_KR = C_Y + LRU_WIDTH
D_IN_PAD = C_KR + LANE

VMEM_LIMIT = 56 * 1024 * 1024

TM_MIX = 512
TQ = 512
TK = 512
ATTN_HEADS_PER_STEP = 8
TM_OUT = 512
TM_FFN = 512
MOE_TILE = 512
MOE_DMA_BLOCK = 2048


def _rms(x, g):
    return x * lax.rsqrt(jnp.mean(x * x, axis=-1, keepdims=True) + RMS_EPS) * g


def _gelu(x):
    return 0.5 * x * (1.0 + jnp.tanh(0.7978845608028654 * (x + 0.044715 * (x * x * x))))


def _sigmoid(x):
    return 1.0 / (1.0 + jnp.exp(-x))


def _silu(x):
    return x * _sigmoid(x)


def _bdot(a, b):
    return jnp.dot(a, b, preferred_element_type=F32)


def _layer_block(a, l):
    rest = a.shape[1:]
    return pl.BlockSpec((None,) + rest, lambda *_: (l,) + (0,) * len(rest))


def _mod_block(mod, l, j, batch_of):
    D = mod.shape[-1]
    return pl.BlockSpec((None, 1, 1, D), lambda *g: (l, 6 * batch_of(*g) + j, 0, 0))


def _mod_kernel(c_ref, w_ref, b_ref, o_ref):
    c = c_ref[...]
    o_ref[0] = jnp.dot(_silu(c), w_ref[0], preferred_element_type=F32, precision=HIGHEST) + b_ref[0]


def _modulation(c, w_mod, b_mod):
    L, D, D6 = w_mod.shape
    B = c.shape[0]
    return pl.pallas_call(
        _mod_kernel,
        grid=(L, D6 // D),
        in_specs=[
            pl.BlockSpec((B, D), lambda l, j: (0, 0)),
            pl.BlockSpec((1, D, D), lambda l, j: (l, 0, j)),
            pl.BlockSpec((1, 1, D), lambda l, j: (l, 0, j)),
        ],
        out_specs=pl.BlockSpec((1, B, D), lambda l, j: (l, 0, j)),
        out_shape=jax.ShapeDtypeStruct((L, B, D6), F32),
        compiler_params=pltpu.CompilerParams(vmem_limit_bytes=VMEM_LIMIT),
        name="modulation",
    )(c, w_mod, b_mod.reshape(L, 1, D6))


def _shift_rows(cur, prev8, k):
    rolled = pltpu.roll(cur, k, 0)
    fix = pltpu.roll(prev8, k, 0)
    row = lax.broadcasted_iota(jnp.int32, fix.shape, 0)
    top = jnp.where(row < k, fix, rolled[:SUBLANE])
    return jnp.concatenate([top, rolled[SUBLANE:]], axis=0)


def _mix_in_kernel(x_ref, sh_ref, sc_ref, preg_ref, win_ref, qg_ref, wq_ref, kvg_ref, wk_ref, wv_ref,
                   cosq_ref, sinq_ref, tk_ref, lng_ref, lnb_ref, ws_ref, bs_ref,
                   cw_ref, cb_ref, wa_ref, ba_ref, wx_ref, bx_ref, lam_ref,
                   q_ref, k_ref, v_ref, y_ref, prev_ref, hc_ref):
    tm = x_ref.shape[1]

    @pl.when(pl.program_id(1) == 0)
    def _():
        prev_ref[...] = jnp.zeros_like(prev_ref)
        hc_ref[...] = jnp.zeros_like(hc_ref)

    x = x_ref[0]
    h = _rms(x, preg_ref[...] * (1.0 + sc_ref[0])) + sh_ref[0]
    z = _bdot(h.astype(BF16), win_ref[...])

    nt = (((1,), (1,)), ((), ()))
    qn = _rms(z[:, C_Q:C_Q + Q_LORA], qg_ref[...]).astype(BF16)
    qq = lax.dot_general(wq_ref[...], qn, nt, preferred_element_type=F32)
    cosq = cosq_ref[...]
    sinq = sinq_ref[...]
    for hd in range(MLA_HEADS):
        lo = hd * HEAD_PAD
        r0, p0 = lo + QK_NOPE, lo + QK_NOPE + QK_ROPE
        q_ref[0, lo:r0, :] = (qq[lo:r0] * Q_SCALE).astype(BF16)
        q_ref[0, r0:p0, :] = (qq[r0:p0] * cosq + qq[p0:p0 + QK_ROPE] * sinq).astype(BF16)
        q_ref[0, p0:p0 + QK_ROPE, :] = qq[p0:p0 + QK_ROPE].astype(BF16)
    kvn = _rms(z[:, C_KV:C_KV + KV_LORA], kvg_ref[...]).astype(BF16)
    kr = (z[:, C_KR:C_KR + LANE] * tk_ref[...]).astype(BF16)
    k_in = jnp.concatenate([kvn, kr], axis=-1)
    k_ref[0] = _bdot(k_in, wk_ref[...]).astype(BF16)
    vv = lax.dot_general(wv_ref[...], kvn, nt, preferred_element_type=F32)
    ones_row = lax.broadcasted_iota(jnp.int32, vv.shape, 0) % HEAD_PAD == V_DIM
    vv = jnp.where(ones_row, 1.0, vv).astype(BF16)
    for c in range(tm // TK):
        v_ref[0, c] = vv[:, c * TK:(c + 1) * TK]

    u = _gelu(z[:, C_U:C_U + SGU_WIDTH])
    gv = _gelu(z[:, C_V:C_V + SGU_WIDTH])
    mu = jnp.mean(gv, axis=-1, keepdims=True)
    var = jnp.mean(jnp.square(gv - mu), axis=-1, keepdims=True)
    vn = ((gv - mu) * lax.rsqrt(var + LN_EPS) * lng_ref[...] + lnb_ref[...]).astype(BF16)
    ws = ws_ref[...]
    r_t = lax.broadcasted_iota(jnp.int32, ws.shape, 0) % SGU_BLOCK
    c_s = lax.broadcasted_iota(jnp.int32, ws.shape, 1)
    ws = jnp.where(c_s <= r_t, ws, 0.0).astype(BF16)
    grp = lax.broadcasted_iota(jnp.int32, (SGU_BLOCK, SGU_WIDTH), 1) // SGU_GROUP_DIM
    for blk in range(tm // SGU_BLOCK):
        r0 = blk * SGU_BLOCK
        res = _bdot(ws, vn[r0:r0 + SGU_BLOCK])
        mixed = bs_ref[...]
        for g in range(SGU_GROUPS):
            mixed = mixed + jnp.where(grp == g, res[g * SGU_BLOCK:(g + 1) * SGU_BLOCK], 0.0)
        y_ref[0, r0:r0 + SGU_BLOCK, 0:SGU_WIDTH] = (u[r0:r0 + SGU_BLOCK] * mixed).astype(BF16)

    zx = z[:, C_X:C_X + LRU_WIDTH]
    prev8 = prev_ref[...]
    cw = cw_ref[...]
    xc = cb_ref[...] + zx * cw[CONV_W - 1:CONV_W]
    for k in range(1, CONV_W):
        xc = xc + _shift_rows(zx, prev8, k) * cw[CONV_W - 1 - k:CONV_W - k]
    prev_ref[...] = zx[tm - SUBLANE:]
    xcb = xc.astype(BF16)
    r = _sigmoid(_bdot(xcb, wa_ref[...]) + ba_ref[...])
    ig = _sigmoid(_bdot(xcb, wx_ref[...]) + bx_ref[...])
    lam = lam_ref[...]
    softplus_neg_lam = jnp.maximum(-lam, 0.0) + jnp.log(1.0 + jnp.exp(-jnp.abs(lam)))
    log_a = -LRU_C * r * softplus_neg_lam
    a = jnp.exp(log_a)
    t = jnp.tanh(log_a)
    b = jnp.sqrt(-2.0 * t / (1.0 - t)) * (ig * xc)
    row = lax.broadcasted_iota(jnp.int32, a.shape, 0)
    k = 1
    while k < SUBLANE:
        valid = row >= k
        a_sh = pltpu.roll(a, k, 0)
        b_sh = pltpu.roll(b, k, 0)
        b = jnp.where(valid, b + a * b_sh, b)
        a = jnp.where(valid, a * a_sh, a)
        k *= 2
    while k < tm:
        b = jnp.concatenate([b[:k], b[k:] + a[k:] * b[:-k]], axis=0)
        a = jnp.concatenate([a[:k], a[k:] * a[:-k]], axis=0)
        k *= 2
    hs = b + a * hc_ref[0:1]
    hc_ref[0:1] = hs[tm - 1:tm]
    y_ref[0, :, SGU_WIDTH:SGU_WIDTH + LRU_WIDTH] = (hs * _gelu(z[:, C_Y:C_Y + LRU_WIDTH])).astype(BF16)


def _mix_in(x, mod, l, p, tabs):
    B, S, D = x.shape
    tm = TM_MIX
    batch = lambda b, s: b
    tab = pl.BlockSpec((tm, LANE), lambda b, s: (s, 0))
    tab_t = pl.BlockSpec((QK_ROPE, tm), lambda b, s: (0, s))
    assert tm % TK == 0
    consts = [p[k] for k in ("pre_mix_g", "w_in", "qg", "w_q", "kvg", "w_k", "w_v")]
    consts2 = [p[k] for k in ("ln_g", "ln_b", "w_s", "b_s", "conv_w", "conv_b", "wa", "ba", "wx", "bx", "lam")]
    qw = MLA_HEADS * HEAD_PAD
    return pl.pallas_call(
        _mix_in_kernel,
        grid=(B, S // tm),
        in_specs=[pl.BlockSpec((1, tm, D), lambda b, s: (b, s, 0)), _mod_block(mod, l, 0, batch),
                  _mod_block(mod, l, 1, batch)]
        + [_layer_block(a, l) for a in consts] + [tab_t, tab_t, tab] + [_layer_block(a, l) for a in consts2],
        out_specs=[
            pl.BlockSpec((1, qw, tm), lambda b, s: (b, 0, s)),
            pl.BlockSpec((1, tm, qw), lambda b, s: (b, s, 0)),
            pl.BlockSpec((1, tm // TK, qw, TK), lambda b, s: (b, s, 0, 0)),
            pl.BlockSpec((1, tm, SGU_WIDTH + LRU_WIDTH), lambda b, s: (b, s, 0)),
        ],
        out_shape=[
            jax.ShapeDtypeStruct((B, qw, S), BF16),
            jax.ShapeDtypeStruct((B, S, qw), BF16),
            jax.ShapeDtypeStruct((B, S // TK, qw, TK), BF16),
            jax.ShapeDtypeStruct((B, S, SGU_WIDTH + LRU_WIDTH), BF16),
        ],
        scratch_shapes=[pltpu.VMEM((SUBLANE, LRU_WIDTH), F32), pltpu.VMEM((SUBLANE, LRU_WIDTH), F32)],
        compiler_params=pltpu.CompilerParams(
            dimension_semantics=("arbitrary", "arbitrary"), vmem_limit_bytes=VMEM_LIMIT),
        name="mix_in",
    )(x, mod, mod, *consts, tabs["cosq"], tabs["sinq"], tabs["tk"], *consts2)


def _attn_kernel(q_ref, k_ref, v_ref, o_ref, m_ref, acc_ref):
    i = pl.program_id(2)
    heads = q_ref.shape[1] // HEAD_PAD
    feat = lambda hd: slice(hd * HEAD_PAD, (hd + 1) * HEAD_PAD)

    m_ref[...] = jnp.full(m_ref.shape, NEG_INF, F32)
    acc_ref[...] = jnp.zeros(acc_ref.shape, F32)

    def scores(t, hd):
        keys = pl.ds(pl.multiple_of(t * TK, TK), TK)
        return _bdot(k_ref[0, keys, feat(hd)], q_ref[0, feat(hd), :])

    def tile(t, masked):
        ss = [scores(t, hd) for hd in range(heads)]
        for hd in range(heads):
            s = ss[hd]
            if masked:
                kc = lax.broadcasted_iota(jnp.int32, s.shape, 0) // CHUNK
                qc = lax.broadcasted_iota(jnp.int32, s.shape, 1) // CHUNK
                s = jnp.where(kc <= qc, s, NEG_INF)
            m = m_ref[hd]
            m_new = jnp.maximum(m, jnp.max(s, axis=0, keepdims=True))
            p = jnp.exp2(s - m_new).astype(BF16)
            acc_ref[hd] = jnp.exp2(m - m_new) * acc_ref[hd] + _bdot(v_ref[0, t, feat(hd), :], p)
            m_ref[hd] = m_new

    def body(t, c):
        tile(t, False)
        return c

    lax.fori_loop(0, i, body, 0)
    tile(i, True)
    outs = []
    for hd in range(heads):
        acc = acc_ref[hd]
        outs.append((acc / acc[V_DIM:V_DIM + 1]).T[:, :V_DIM])
    o_ref[0] = jnp.concatenate(outs, axis=-1).astype(BF16)


def _attention(q, k, v):
    B, S, _ = k.shape
    hp = ATTN_HEADS_PER_STEP
    assert TQ == TK
    return pl.pallas_call(
        _attn_kernel,
        grid=(B, MLA_HEADS // hp, S // TQ),
        in_specs=[
            pl.BlockSpec((1, hp * HEAD_PAD, TQ), lambda b, h, i: (b, h, i)),
            pl.BlockSpec((1, S, hp * HEAD_PAD), lambda b, h, i: (b, 0, h)),
            pl.BlockSpec((1, S // TK, hp * HEAD_PAD, TK), lambda b, h, i: (b, 0, h, 0)),
        ],
        out_specs=pl.BlockSpec((1, TQ, hp * V_DIM), lambda b, h, i: (b, i, h)),
        out_shape=jax.ShapeDtypeStruct((B, S, MLA_HEADS * V_DIM), BF16),
        scratch_shapes=[
            pltpu.VMEM((hp, 1, TQ), F32),
            pltpu.VMEM((hp, HEAD_PAD, TQ), F32),
        ],
        compiler_params=pltpu.CompilerParams(
            dimension_semantics=("arbitrary", "arbitrary", "arbitrary"), vmem_limit_bytes=VMEM_LIMIT),
        name="attention",
    )(q, k, v)


def _mix_out_kernel(o_ref, y_ref, x_ref, wo_ref, postg_ref, gm_ref, preg_ref, scf_ref, shf_ref,
                    xo_ref, h_ref):
    no = o_ref.shape[1]
    y = _bdot(o_ref[...], wo_ref[0:no]) + _bdot(y_ref[...], wo_ref[no:])
    x = x_ref[...] + _rms(y, gm_ref[0] * postg_ref[...])
    xo_ref[...] = x
    h_ref[...] = (_rms(x, preg_ref[...] * (1.0 + scf_ref[0])) + shf_ref[0]).astype(BF16)


def _store_token_tiles(ref, x):
    rows, d = x.shape
    assert d == SUBLANE * LANE
    for j in range(SUBLANE):
        ref[pl.ds(j, rows, stride=SUBLANE), :] = x[:, j * LANE:(j + 1) * LANE]


def _load_token_tiles(ref):
    rows = ref.shape[0] // SUBLANE
    return jnp.concatenate([ref[pl.ds(j, rows, stride=SUBLANE), :] for j in range(SUBLANE)], axis=-1)


def _token_copy(src, dst, src_token, dst_token, sem):
    def rows(t):
        start = t * SUBLANE
        return pl.ds(start if isinstance(t, int) else pl.multiple_of(start, SUBLANE), SUBLANE)

    return pltpu.make_async_copy(src.at[rows(src_token)], dst.at[rows(dst_token)], sem)


def _mix_out_moe_kernel(o_ref, y_ref, x_ref, wo_ref, postg_ref, gm_ref, preg_ref, scf_ref, shf_ref,
                        router_ref, xo_ref, h_ref, route_ref):
    no = o_ref.shape[1]
    y = _bdot(o_ref[...], wo_ref[0:no]) + _bdot(y_ref[...], wo_ref[no:])
    x = x_ref[...] + _rms(y, gm_ref[0] * postg_ref[...])
    xo_ref[...] = x
    h = _rms(x, preg_ref[...] * (1.0 + scf_ref[0])) + shf_ref[0]
    _store_token_tiles(h_ref, h)
    h_hi = h.astype(BF16)
    h_lo = (h - h_hi.astype(F32)).astype(BF16)
    t = _bdot(h_hi, router_ref[...])
    logits = t[:, :LANE] + t[:, LANE:] + _bdot(h_lo, router_ref[:, :LANE])
    lane = lax.broadcasted_iota(jnp.int32, logits.shape, 1)
    logits = jnp.where(lane < N_EXPERTS, logits, -jnp.inf)
    m1 = jnp.max(logits, axis=-1, keepdims=True)
    i1 = jnp.min(jnp.where(logits == m1, lane, LANE), axis=-1, keepdims=True)
    rest = jnp.where(lane == i1, -jnp.inf, logits)
    m2 = jnp.max(rest, axis=-1, keepdims=True)
    i2 = jnp.min(jnp.where(rest == m2, lane, LANE), axis=-1, keepdims=True)
    e = jnp.exp(m2 - m1)
    g1 = 1.0 / (1.0 + e)
    route_ref[...] = (jnp.where(lane == 0, i1.astype(F32), 0.0) + jnp.where(lane == 1, i2.astype(F32), 0.0)
                      + jnp.where(lane == 2, g1, 0.0) + jnp.where(lane == 3, e * g1, 0.0))


def _mix_out(o, y, x, mod, l, p, router, tiles_per_batch):
    N, D = x.shape
    tm = TM_OUT
    row = lambda w: pl.BlockSpec((tm, w), lambda i: (i, 0))
    batch = lambda i: i // tiles_per_batch
    in_specs = [row(o.shape[1]), row(y.shape[1]), row(D), _layer_block(p["w_o"], l),
                _layer_block(p["post_mix_g"], l), _mod_block(mod, l, 2, batch), _layer_block(p["pre_ffn_g"], l),
                _mod_block(mod, l, 4, batch), _mod_block(mod, l, 3, batch)]
    out_specs = [row(D), row(D)]
    out_shape = [jax.ShapeDtypeStruct((N, D), F32), jax.ShapeDtypeStruct((N, D), BF16)]
    args = [o, y, x, p["w_o"], p["post_mix_g"], mod, p["pre_ffn_g"], mod, mod]
    body = _mix_out_kernel
    if router is not None:
        in_specs.append(_layer_block(router, l // 2))
        out_specs = [row(D), pl.BlockSpec((tm * SUBLANE, LANE), lambda i: (i, 0)), row(LANE)]
        out_shape = [jax.ShapeDtypeStruct((N, D), F32), jax.ShapeDtypeStruct((N * SUBLANE, LANE), F32),
                     jax.ShapeDtypeStruct((N, LANE), F32)]
        args.append(router)
        body = _mix_out_moe_kernel
    return pl.pallas_call(
        body,
        grid=(N // tm,),
        in_specs=in_specs,
        out_specs=out_specs,
        out_shape=out_shape,
        compiler_params=pltpu.CompilerParams(dimension_semantics=("arbitrary",), vmem_limit_bytes=VMEM_LIMIT),
        name="mix_out",
    )(*args)


def _swiglu(h, wg, wu, wd):
    g = _bdot(h, wg)
    u = _bdot(h, wu)
    return _bdot((_silu(g) * u).astype(BF16), wd)


def _ffn_kernel(h_ref, x_ref, wg_ref, wu_ref, wd_ref, postg_ref, gf_ref, o_ref, acc_ref):
    j = pl.program_id(1)

    @pl.when(j == 0)
    def _():
        acc_ref[...] = jnp.zeros_like(acc_ref)

    acc_ref[...] += _swiglu(h_ref[...], wg_ref[...], wu_ref[...], wd_ref[...])

    @pl.when(j == pl.num_programs(1) - 1)
    def _():
        o_ref[...] = x_ref[...] + _rms(acc_ref[...], gf_ref[0] * postg_ref[...])


def _ffn(h, x, mod, l, p, tiles_per_batch):
    N, D = x.shape
    wg, wu, wd = p["ffn_w_gate"], p["ffn_w_up"], p["ffn_w_down"]
    FF = wg.shape[2]
    tm = TM_FFN
    tf = FF // 2
    lf = l // 2
    return pl.pallas_call(
        _ffn_kernel,
        grid=(N // tm, FF // tf),
        in_specs=[
            pl.BlockSpec((tm, D), lambda i, j: (i, 0)),
            pl.BlockSpec((tm, D), lambda i, j: (i, 0)),
            pl.BlockSpec((None, D, tf), lambda i, j: (lf, 0, j)),
            pl.BlockSpec((None, D, tf), lambda i, j: (lf, 0, j)),
            pl.BlockSpec((None, tf, D), lambda i, j: (lf, j, 0)),
            _layer_block(p["post_ffn_g"], l),
            _mod_block(mod, l, 5, lambda i, j: i // tiles_per_batch),
        ],
        out_specs=pl.BlockSpec((tm, D), lambda i, j: (i, 0)),
        out_shape=jax.ShapeDtypeStruct((N, D), F32),
        scratch_shapes=[pltpu.VMEM((tm, D), F32)],
        compiler_params=pltpu.CompilerParams(
            dimension_semantics=("arbitrary", "arbitrary"), vmem_limit_bytes=VMEM_LIMIT),
        name="ffn_dense",
    )(h, x, wg, wu, wd, p["post_ffn_g"], mod)


def _moe_dispatch_kernel(pad_lo_ref, pad_hi_ref, slot_ref, h_ref, hs_hbm, zero_ref, sem):
    tm = h_ref.shape[0] // SUBLANE

    @pl.when(pl.program_id(0) == 0)
    def _():
        zero_ref[...] = jnp.zeros(zero_ref.shape, zero_ref.dtype)
        for e in range(pad_lo_ref.shape[0]):
            lo, hi = pad_lo_ref[e], pad_hi_ref[e]

            def start_pad(s, c):
                _token_copy(zero_ref, hs_hbm, 0, s, sem).start()
                return c

            def wait_pad(s, c):
                _token_copy(zero_ref, hs_hbm, 0, s, sem).wait()
                return c

            lax.fori_loop(lo, hi, start_pad, 0)
            lax.fori_loop(lo, hi, wait_pad, 0)

    def start(r, c):
        for k in range(2):
            _token_copy(h_ref, hs_hbm, r, slot_ref[0, 0, 2 * r + k], sem).start(priority=k)
        return c

    lax.fori_loop(0, tm, start, 0, unroll=4)

    def wait(r, c):
        for k in range(2):
            _token_copy(h_ref, hs_hbm, r, 0, sem).wait()
        return c

    lax.fori_loop(0, tm, wait, 0, unroll=4)


def _moe_dispatch(pad_lo, pad_hi, slots, h_tiles, n_slots):
    N = h_tiles.shape[0] // SUBLANE
    tm = MOE_DMA_BLOCK
    return pl.pallas_call(
        _moe_dispatch_kernel,
        grid_spec=pltpu.PrefetchScalarGridSpec(
            num_scalar_prefetch=2,
            grid=(N // tm,),
            in_specs=[
                pl.BlockSpec((1, 1, 2 * tm), lambda i, lo, hi: (i, 0, 0), memory_space=pltpu.SMEM),
                pl.BlockSpec((tm * SUBLANE, LANE), lambda i, lo, hi: (i, 0)),
            ],
            out_specs=pl.BlockSpec(memory_space=pl.ANY),
            scratch_shapes=[pltpu.VMEM((SUBLANE, LANE), F32), pltpu.SemaphoreType.DMA(())],
        ),
        out_shape=jax.ShapeDtypeStruct((n_slots * SUBLANE, LANE), F32),
        compiler_params=pltpu.CompilerParams(dimension_semantics=("arbitrary",)),
        name="moe_dispatch",
    )(pad_lo, pad_hi, slots.reshape(N // tm, 1, 2 * tm), h_tiles)


def _moe_expert_kernel(texp_ref, nused_ref, hs_ref, wg_ref, wu_ref, wd_ref, ys_ref):
    del texp_ref

    @pl.when(pl.program_id(0) < nused_ref[0])
    def _():
        x = _load_token_tiles(hs_ref).astype(BF16)
        _store_token_tiles(ys_ref, _swiglu(x, wg_ref[...], wu_ref[...], wd_ref[...]))

    @pl.when(pl.program_id(0) >= nused_ref[0])
    def _():
        ys_ref[...] = jnp.zeros(ys_ref.shape, F32)


def _moe_experts(tile_expert, n_used, hs, wg, wu, wd, lm):
    n_slots = hs.shape[0] // SUBLANE
    _, E, D, FF = wg.shape
    t = MOE_TILE
    tile = pl.BlockSpec((t * SUBLANE, LANE), lambda i, te, nu: (i, 0))
    return pl.pallas_call(
        _moe_expert_kernel,
        grid_spec=pltpu.PrefetchScalarGridSpec(
            num_scalar_prefetch=2,
            grid=(n_slots // t,),
            in_specs=[
                tile,
                pl.BlockSpec((None, None, D, FF), lambda i, te, nu: (lm, te[i], 0, 0)),
                pl.BlockSpec((None, None, D, FF), lambda i, te, nu: (lm, te[i], 0, 0)),
                pl.BlockSpec((None, None, FF, D), lambda i, te, nu: (lm, te[i], 0, 0)),
            ],
            out_specs=tile,
        ),
        out_shape=jax.ShapeDtypeStruct(hs.shape, F32),
        compiler_params=pltpu.CompilerParams(dimension_semantics=("arbitrary",), vmem_limit_bytes=VMEM_LIMIT),
        name="moe_experts",
    )(tile_expert, n_used, hs, wg, wu, wd)


def _moe_combine_kernel(slot_ref, ys_hbm, route_ref, x_ref, postg_ref, gf_ref, o_ref, buf_ref, sem):
    tm = x_ref.shape[0]

    def start(r, c):
        for k in range(2):
            _token_copy(ys_hbm, buf_ref.at[k], slot_ref[0, 0, 2 * r + k], r, sem).start(priority=k)
        return c

    lax.fori_loop(0, tm, start, 0, unroll=4)

    def wait(r, c):
        for k in range(2):
            _token_copy(ys_hbm, buf_ref.at[k], 0, r, sem).wait()
        return c

    lax.fori_loop(0, tm, wait, 0, unroll=4)
    route = route_ref[...]
    lane = lax.broadcasted_iota(jnp.int32, route.shape, 1)
    f = None
    for k in range(2):
        gate = jnp.sum(jnp.where(lane == 2 + k, route, 0.0), axis=-1, keepdims=True)
        y = _load_token_tiles(buf_ref.at[k])
        f = gate * y if f is None else f + gate * y
    o_ref[...] = x_ref[...] + _rms(f, gf_ref[0] * postg_ref[...])


def _moe_combine(slots, ys, route, x, mod, l, p, tiles_per_batch):
    N, D = x.shape
    tm = TM_FFN
    return pl.pallas_call(
        _moe_combine_kernel,
        grid=(N // tm,),
        in_specs=[
            pl.BlockSpec((1, 1, 2 * tm), lambda i: (i, 0, 0), memory_space=pltpu.SMEM),
            pl.BlockSpec(memory_space=pl.ANY),
            pl.BlockSpec((tm, LANE), lambda i: (i, 0)),
            pl.BlockSpec((tm, D), lambda i: (i, 0)),
            _layer_block(p["post_ffn_g"], l),
            _mod_block(mod, l, 5, lambda i: i // tiles_per_batch),
        ],
        out_specs=pl.BlockSpec((tm, D), lambda i: (i, 0)),
        out_shape=jax.ShapeDtypeStruct((N, D), F32),
        scratch_shapes=[pltpu.VMEM((2, tm * SUBLANE, LANE), F32), pltpu.SemaphoreType.DMA(())],
        compiler_params=pltpu.CompilerParams(dimension_semantics=("arbitrary",), vmem_limit_bytes=VMEM_LIMIT),
        name="moe_combine",
    )(slots.reshape(N // tm, 1, 2 * tm), ys, route, x, p["post_ffn_g"], mod)


def _moe_plan(route, n_slots):
    t = MOE_TILE
    pair_expert = route[:, :2].astype(jnp.int32).reshape(-1)
    onehot = (pair_expert[:, None] == jnp.arange(N_EXPERTS, dtype=jnp.int32)[None, :]).astype(jnp.int32)
    csum = jnp.cumsum(onehot, axis=0)
    rank = jnp.take_along_axis(csum, pair_expert[:, None], axis=1)[:, 0] - 1
    count = csum[-1]
    padded = (count + t - 1) // t * t
    ends = jnp.cumsum(padded)
    starts = ends - padded
    slot = starts[pair_expert] + rank
    n_used = ends[-1:] // t
    tile_start = jnp.arange(n_slots // t, dtype=jnp.int32) * t
    tile_expert = jnp.sum((tile_start[:, None] >= ends[None, :]).astype(jnp.int32), axis=1)
    last_used = tile_expert[jnp.maximum(n_used[0] - 1, 0)]
    tile_expert = jnp.where(tile_start < ends[-1], tile_expert, last_used)
    pad_lo = jnp.concatenate([starts + count, ends[-1:]]).astype(jnp.int32)
    pad_hi = jnp.concatenate([ends, jnp.full((1,), n_slots, ends.dtype)]).astype(jnp.int32)
    return slot, pad_lo, pad_hi, tile_expert, n_used.astype(jnp.int32)


def _moe(h_tiles, x, route, mod, l, p, tiles_per_batch):
    N = x.shape[0]
    n_slots = 2 * N + N_EXPERTS * MOE_TILE
    slot, pad_lo, pad_hi, tile_expert, n_used = _moe_plan(route, n_slots)
    hs = _moe_dispatch(pad_lo, pad_hi, slot, h_tiles, n_slots)
    ys = _moe_experts(tile_expert, n_used, hs, p["moe_w_gate"], p["moe_w_up"], p["moe_w_down"], l // 2)
    return _moe_combine(slot, ys, route, x, mod, l, p, tiles_per_batch)


def _rope_partner(w):
    half = QK_ROPE // 2
    return jnp.concatenate([-w[..., half:], w[..., :half]], axis=-1)


def _rope_tables(S):
    pos = jnp.arange(S, dtype=F32)
    inv = ROPE_THETA ** (-jnp.arange(0, QK_ROPE, 2, dtype=F32) / QK_ROPE)
    ang = pos[:, None] * inv[None, :]
    cos, sin = jnp.cos(ang), jnp.sin(ang)
    cos2 = jnp.concatenate([cos, cos], axis=-1)
    sin2 = jnp.concatenate([sin, sin], axis=-1)
    tk = jnp.concatenate([cos2, sin2, jnp.zeros((S, LANE - 2 * QK_ROPE), F32)], axis=-1)
    return {"cosq": cos2.T * Q_SCALE, "sinq": sin2.T * Q_SCALE, "tk": tk}


def _prepare_params(w_in, q_norm_g, w_uq, kv_norm_g, w_ukv, sgu_ln_g, sgu_ln_b, sgu_w, sgu_b, conv_w, conv_b,
                    lru_wa, lru_ba, lru_wx, lru_bx, lru_lambda, pre_mix_g, post_mix_g, w_o, pre_ffn_g, post_ffn_g,
                    ffn_w_gate, ffn_w_up, ffn_w_down, moe_router, moe_w_gate, moe_w_up, moe_w_down):
    L, D, _ = w_in.shape
    o_q, o_kv = 0, Q_LORA
    o_kr = o_kv + KV_LORA
    o_u = o_kr + QK_ROPE
    o_v = o_u + SGU_WIDTH
    o_x = o_v + SGU_WIDTH
    o_y = o_x + LRU_WIDTH
    w_kr = w_in[..., o_kr:o_kr + QK_ROPE]
    w_in_p = jnp.concatenate([
        w_in[..., o_q:o_q + Q_LORA], w_in[..., o_kv:o_kv + KV_LORA], w_in[..., o_u:o_u + SGU_WIDTH],
        w_in[..., o_v:o_v + SGU_WIDTH], w_in[..., o_x:o_x + LRU_WIDTH], w_in[..., o_y:o_y + LRU_WIDTH],
        w_kr, _rope_partner(w_kr), jnp.zeros((L, D, LANE - 2 * QK_ROPE), F32)], axis=-1).astype(BF16)

    H = MLA_HEADS
    pad = HEAD_PAD - QK_NOPE - QK_ROPE
    wq = w_uq.reshape(L, Q_LORA, H, QK_NOPE + QK_ROPE)
    assert pad == QK_ROPE
    w_q = jnp.concatenate([wq, _rope_partner(wq[..., QK_NOPE:])], axis=-1).reshape(L, Q_LORA, H * HEAD_PAD)
    w_q = jnp.swapaxes(w_q, 1, 2).astype(BF16)

    wkv = w_ukv.reshape(L, KV_LORA, H, QK_NOPE + V_DIM)
    wk_lat = jnp.concatenate([wkv[..., :QK_NOPE], jnp.zeros((L, KV_LORA, H, HEAD_PAD - QK_NOPE), F32)],
                             axis=-1).reshape(L, KV_LORA, H * HEAD_PAD)
    eye = jnp.eye(QK_ROPE, dtype=F32)
    place = jnp.concatenate([jnp.zeros((QK_ROPE, QK_NOPE), F32), eye, jnp.zeros((QK_ROPE, pad), F32)], axis=-1)
    place = jnp.tile(place, (2, H))
    tail = jnp.concatenate([place, jnp.zeros((LANE - 2 * QK_ROPE, H * HEAD_PAD), F32)], axis=0)
    w_k = jnp.concatenate([wk_lat, jnp.broadcast_to(tail, (L,) + tail.shape)], axis=1).astype(BF16)
    w_v = jnp.swapaxes(
        jnp.concatenate([wkv[..., QK_NOPE:], jnp.zeros((L, KV_LORA, H, HEAD_PAD - V_DIM), F32)],
                        axis=-1).reshape(L, KV_LORA, H * HEAD_PAD), 1, 2).astype(BF16)

    def block_diag(w):
        sel = jnp.eye(LRU_HEADS, dtype=F32)
        return jnp.einsum("lhij,hg->lhigj", w, sel).reshape(L, LRU_WIDTH, LRU_WIDTH).astype(BF16)

    row = lambda a: a.reshape(a.shape[0], 1, -1)
    r = jnp.pad(moe_router, ((0, 0), (0, 0), (0, LANE - N_EXPERTS)))
    r_hi = r.astype(BF16)
    return {
        "pre_mix_g": row(pre_mix_g), "w_in": w_in_p, "qg": row(q_norm_g), "w_q": w_q, "kvg": row(kv_norm_g),
        "w_k": w_k, "w_v": w_v, "ln_g": row(sgu_ln_g), "ln_b": row(sgu_ln_b),
        "w_s": sgu_w.reshape(L, SGU_GROUPS * SGU_BLOCK, SGU_BLOCK),
        "b_s": jnp.repeat(jnp.swapaxes(sgu_b, 1, 2), SGU_GROUP_DIM, axis=2),
        "conv_w": conv_w, "conv_b": row(conv_b), "wa": block_diag(lru_wa), "ba": row(lru_ba),
        "wx": block_diag(lru_wx), "bx": row(lru_bx), "lam": row(lru_lambda),
        "w_o": w_o.astype(BF16), "post_mix_g": row(post_mix_g), "pre_ffn_g": row(pre_ffn_g),
        "post_ffn_g": row(post_ffn_g),
        "ffn_w_gate": ffn_w_gate.astype(BF16), "ffn_w_up": ffn_w_up.astype(BF16),
        "ffn_w_down": ffn_w_down.astype(BF16),
        "router": jnp.concatenate([r_hi, (r - r_hi.astype(F32)).astype(BF16)], axis=-1),
        "moe_w_gate": moe_w_gate.astype(BF16), "moe_w_up": moe_w_up.astype(BF16),
        "moe_w_down": moe_w_down.astype(BF16),
    }


def kernel(x, c, w_mod, b_mod, pre_mix_g, post_mix_g, w_in, q_norm_g, w_uq, kv_norm_g, w_ukv, sgu_ln_g,
           sgu_ln_b, sgu_w, sgu_b, conv_w, conv_b, lru_wa, lru_ba, lru_wx, lru_bx, lru_lambda, w_o,
           pre_ffn_g, post_ffn_g, ffn_w_gate, ffn_w_up, ffn_w_down, moe_router, moe_w_gate, moe_w_up,
           moe_w_down):
    B, S, D = x.shape
    L = w_mod.shape[0]
    N = B * S
    tabs = _rope_tables(S)
    p = _prepare_params(w_in, q_norm_g, w_uq, kv_norm_g, w_ukv, sgu_ln_g, sgu_ln_b, sgu_w, sgu_b, conv_w, conv_b,
                        lru_wa, lru_ba, lru_wx, lru_bx, lru_lambda, pre_mix_g, post_mix_g, w_o, pre_ffn_g,
                        post_ffn_g, ffn_w_gate, ffn_w_up, ffn_w_down, moe_router, moe_w_gate, moe_w_up, moe_w_down)
    mod = _modulation(c, w_mod, b_mod).reshape(L, B * 6, 1, D)
    xf = x.reshape(N, D)
    for l in range(L):
        q, k, v, ymix = _mix_in(xf.reshape(B, S, D), mod, l, p, tabs)
        o = _attention(q, k, v)
        moe = l % 2 == 1
        res = _mix_out(o.reshape(N, -1), ymix.reshape(N, -1), xf, mod, l, p, p["router"] if moe else None,
                       S // TM_OUT)
        if moe:
            xf, h_tiles, route = res
            xf = _moe(h_tiles, xf, route, mod, l, p, S // TM_FFN)
        else:
            xf, h2 = res
            xf = _ffn(h2, xf, mod, l, p, S // TM_FFN)
    return xf.reshape(B, S, D)
```

```python
import jax
import jax.numpy as jnp
from jax import lax
from jax.experimental import pallas as pl
from jax.experimental.pallas import tpu as pltpu

F32 = jnp.float32
BF16 = jnp.bfloat16
HIGHEST = lax.Precision.HIGHEST

MLA_HEADS = 8
QK_NOPE = 64
QK_ROPE = 32
V_DIM = 64
Q_LORA = 256
KV_LORA = 128
ROPE_THETA = 10000.0
CHUNK = 64
SGU_GROUPS = 4
SGU_GROUP_DIM = 64
SGU_WIDTH = SGU_GROUPS * SGU_GROUP_DIM
SGU_BLOCK = 128
LRU_HEADS = 4
LRU_HEAD_DIM = 64
LRU_WIDTH = LRU_HEADS * LRU_HEAD_DIM
CONV_W = 4
LRU_C = 8.0
N_EXPERTS = 8
NEG_INF = -1e30
RMS_EPS = 1e-6
LN_EPS = 1e-5

LANE = 128
SUBLANE = 8
HEAD_PAD = LANE
Q_SCALE = (QK_NOPE + QK_ROPE) ** -0.5 * 1.4426950408889634

C_Q = 0
C_KV = C_Q + Q_LORA
C_U = C_KV + KV_LORA
C_V = C_U + SGU_WIDTH
C_X = C_V + SGU_WIDTH
C_Y = C_X + LRU_WIDTH
C_KR = C_Y + LRU_WIDTH
D_IN_PAD = C_KR + LANE

VMEM_LIMIT = 56 * 1024 * 1024

TM_MIX = 512
TQ = 512
TK = 512
ATTN_HEADS_PER_STEP = 8
TM_OUT = 512
TM_FFN = 512
MOE_TILE = 512
MOE_DMA_BLOCK = 2048


def _rms(x, g):
    return x * lax.rsqrt(jnp.mean(x * x, axis=-1, keepdims=True) + RMS_EPS) * g


def _gelu(x):
    return 0.5 * x * (1.0 + jnp.tanh(0.7978845608028654 * (x + 0.044715 * (x * x * x))))


def _sigmoid(x):
    return 1.0 / (1.0 + jnp.exp(-x))


def _silu(x):
    return x * _sigmoid(x)


def _bdot(a, b):
    return jnp.dot(a, b, preferred_element_type=F32)


def _layer_block(a, l):
    rest = a.shape[1:]
    return pl.BlockSpec((None,) + rest, lambda *_: (l,) + (0,) * len(rest))


def _mod_block(mod, l, j, batch_of):
    D = mod.shape[-1]
    return pl.BlockSpec((None, 1, 1, D), lambda *g: (l, 6 * batch_of(*g) + j, 0, 0))


def _mod_kernel(c_ref, w_ref, b_ref, o_ref):
    c = c_ref[...]
    o_ref[0] = jnp.dot(_silu(c), w_ref[0], preferred_element_type=F32, precision=HIGHEST) + b_ref[0]


def _modulation(c, w_mod, b_mod):
    L, D, D6 = w_mod.shape
    B = c.shape[0]
    return pl.pallas_call(
        _mod_kernel,
        grid=(L, D6 // D),
        in_specs=[
            pl.BlockSpec((B, D), lambda l, j: (0, 0)),
            pl.BlockSpec((1, D, D), lambda l, j: (l, 0, j)),
            pl.BlockSpec((1, 1, D), lambda l, j: (l, 0, j)),
        ],
        out_specs=pl.BlockSpec((1, B, D), lambda l, j: (l, 0, j)),
        out_shape=jax.ShapeDtypeStruct((L, B, D6), F32),
        compiler_params=pltpu.CompilerParams(vmem_limit_bytes=VMEM_LIMIT),
        name="modulation",
    )(c, w_mod, b_mod.reshape(L, 1, D6))


def _shift_rows(cur, prev8, k):
    rolled = pltpu.roll(cur, k, 0)
    fix = pltpu.roll(prev8, k, 0)
    row = lax.broadcasted_iota(jnp.int32, fix.shape, 0)
    top = jnp.where(row < k, fix, rolled[:SUBLANE])
    return jnp.concatenate([top, rolled[SUBLANE:]], axis=0)


def _mix_in_kernel(x_ref, sh_ref, sc_ref, preg_ref, win_ref, qg_ref, wq_ref, kvg_ref, wk_ref, wv_ref,
                   cosq_ref, sinq_ref, tk_ref, lng_ref, lnb_ref, ws_ref, bs_ref,
                   cw_ref, cb_ref, wa_ref, ba_ref, wx_ref, bx_ref, lam_ref,
                   q_ref, k_ref, v_ref, y_ref, prev_ref, hc_ref):
    tm = x_ref.shape[1]

    @pl.when(pl.program_id(1) == 0)
    def _():
        prev_ref[...] = jnp.zeros_like(prev_ref)
        hc_ref[...] = jnp.zeros_like(hc_ref)

    x = x_ref[0]
    h = _rms(x, preg_ref[...] * (1.0 + sc_ref[0])) + sh_ref[0]
    z = _bdot(h.astype(BF16), win_ref[...])

    nt = (((1,), (1,)), ((), ()))
    qn = _rms(z[:, C_Q:C_Q + Q_LORA], qg_ref[...]).astype(BF16)
    qq = lax.dot_general(wq_ref[...], qn, nt, preferred_element_type=F32)
    cosq = cosq_ref[...]
    sinq = sinq_ref[...]
    for hd in range(MLA_HEADS):
        lo = hd * HEAD_PAD
        r0, p0 = lo + QK_NOPE, lo + QK_NOPE + QK_ROPE
        q_ref[0, lo:r0, :] = (qq[lo:r0] * Q_SCALE).astype(BF16)
        q_ref[0, r0:p0, :] = (qq[r0:p0] * cosq + qq[p0:p0 + QK_ROPE] * sinq).astype(BF16)
        q_ref[0, p0:p0 + QK_ROPE, :] = qq[p0:p0 + QK_ROPE].astype(BF16)
    kvn = _rms(z[:, C_KV:C_KV + KV_LORA], kvg_ref[...]).astype(BF16)
    kr = (z[:, C_KR:C_KR + LANE] * tk_ref[...]).astype(BF16)
    k_in = jnp.concatenate([kvn, kr], axis=-1)
    k_ref[0] = _bdot(k_in, wk_ref[...]).astype(BF16)
    vv = lax.dot_general(wv_ref[...], kvn, nt, preferred_element_type=F32)
    ones_row = lax.broadcasted_iota(jnp.int32, vv.shape, 0) % HEAD_PAD == V_DIM
    vv = jnp.where(ones_row, 1.0, vv).astype(BF16)
    for c in range(tm // TK):
        v_ref[0, c] = vv[:, c * TK:(c + 1) * TK]

    u = _gelu(z[:, C_U:C_U + SGU_WIDTH])
    gv = _gelu(z[:, C_V:C_V + SGU_WIDTH])
    mu = jnp.mean(gv, axis=-1, keepdims=True)
    var = jnp.mean(jnp.square(gv - mu), axis=-1, keepdims=True)
    vn = ((gv - mu) * lax.rsqrt(var + LN_EPS) * lng_ref[...] + lnb_ref[...]).astype(BF16)
    ws = ws_ref[...]
    r_t = lax.broadcasted_iota(jnp.int32, ws.shape, 0) % SGU_BLOCK
    c_s = lax.broadcasted_iota(jnp.int32, ws.shape, 1)
    ws = jnp.where(c_s <= r_t, ws, 0.0).astype(BF16)
    grp = lax.broadcasted_iota(jnp.int32, (SGU_BLOCK, SGU_WIDTH), 1) // SGU_GROUP_DIM
    for blk in range(tm // SGU_BLOCK):
        r0 = blk * SGU_BLOCK
        res = _bdot(ws, vn[r0:r0 + SGU_BLOCK])
        mixed = bs_ref[...]
        for g in range(SGU_GROUPS):
            mixed = mixed + jnp.where(grp == g, res[g * SGU_BLOCK:(g + 1) * SGU_BLOCK], 0.0)
        y_ref[0, r0:r0 + SGU_BLOCK, 0:SGU_WIDTH] = (u[r0:r0 + SGU_BLOCK] * mixed).astype(BF16)

    zx = z[:, C_X:C_X + LRU_WIDTH]
    prev8 = prev_ref[...]
    cw = cw_ref[...]
    xc = cb_ref[...] + zx * cw[CONV_W - 1:CONV_W]
    for k in range(1, CONV_W):
        xc = xc + _shift_rows(zx, prev8, k) * cw[CONV_W - 1 - k:CONV_W - k]
    prev_ref[...] = zx[tm - SUBLANE:]
    xcb = xc.astype(BF16)
    r = _sigmoid(_bdot(xcb, wa_ref[...]) + ba_ref[...])
    ig = _sigmoid(_bdot(xcb, wx_ref[...]) + bx_ref[...])
    lam = lam_ref[...]
    softplus_neg_lam = jnp.maximum(-lam, 0.0) + jnp.log(1.0 + jnp.exp(-jnp.abs(lam)))
    log_a = -LRU_C * r * softplus_neg_lam
    a = jnp.exp(log_a)
    t = jnp.tanh(log_a)
    b = jnp.sqrt(-2.0 * t / (1.0 - t)) * (ig * xc)
    row = lax.broadcasted_iota(jnp.int32, a.shape, 0)
    k = 1
    while k < SUBLANE:
        valid = row >= k
        a_sh = pltpu.roll(a, k, 0)
        b_sh = pltpu.roll(b, k, 0)
        b = jnp.where(valid, b + a * b_sh, b)
        a = jnp.where(valid, a * a_sh, a)
        k *= 2
    while k < tm:
        b = jnp.concatenate([b[:k], b[k:] + a[k:] * b[:-k]], axis=0)
        a = jnp.concatenate([a[:k], a[k:] * a[:-k]], axis=0)
        k *= 2
    hs = b + a * hc_ref[0:1]
    hc_ref[0:1] = hs[tm - 1:tm]
    y_ref[0, :, SGU_WIDTH:SGU_WIDTH + LRU_WIDTH] = (hs * _gelu(z[:, C_Y:C_Y + LRU_WIDTH])).astype(BF16)


def _mix_in(x, mod, l, p, tabs):
    B, S, D = x.shape
    tm = TM_MIX
    batch = lambda b, s: b
    tab = pl.BlockSpec((tm, LANE), lambda b, s: (s, 0))
    tab_t = pl.BlockSpec((QK_ROPE, tm), lambda b, s: (0, s))
    assert tm % TK == 0
    consts = [p[k] for k in ("pre_mix_g", "w_in", "qg", "w_q", "kvg", "w_k", "w_v")]
    consts2 = [p[k] for k in ("ln_g", "ln_b", "w_s", "b_s", "conv_w", "conv_b", "wa", "ba", "wx", "bx", "lam")]
    qw = MLA_HEADS * HEAD_PAD
    return pl.pallas_call(
        _mix_in_kernel,
        grid=(B, S // tm),
        in_specs=[pl.BlockSpec((1, tm, D), lambda b, s: (b, s, 0)), _mod_block(mod, l, 0, batch),
                  _mod_block(mod, l, 1, batch)]
        + [_layer_block(a, l) for a in consts] + [tab_t, tab_t, tab] + [_layer_block(a, l) for a in consts2],
        out_specs=[
            pl.BlockSpec((1, qw, tm), lambda b, s: (b, 0, s)),
            pl.BlockSpec((1, tm, qw), lambda b, s: (b, s, 0)),
            pl.BlockSpec((1, tm // TK, qw, TK), lambda b, s: (b, s, 0, 0)),
            pl.BlockSpec((1, tm, SGU_WIDTH + LRU_WIDTH), lambda b, s: (b, s, 0)),
        ],
        out_shape=[
            jax.ShapeDtypeStruct((B, qw, S), BF16),
            jax.ShapeDtypeStruct((B, S, qw), BF16),
            jax.ShapeDtypeStruct((B, S // TK, qw, TK), BF16),
            jax.ShapeDtypeStruct((B, S, SGU_WIDTH + LRU_WIDTH), BF16),
        ],
        scratch_shapes=[pltpu.VMEM((SUBLANE, LRU_WIDTH), F32), pltpu.VMEM((SUBLANE, LRU_WIDTH), F32)],
        compiler_params=pltpu.CompilerParams(
            dimension_semantics=("arbitrary", "arbitrary"), vmem_limit_bytes=VMEM_LIMIT),
        name="mix_in",
    )(x, mod, mod, *consts, tabs["cosq"], tabs["sinq"], tabs["tk"], *consts2)


def _attn_kernel(q_ref, k_ref, v_ref, o_ref, m_ref, acc_ref):
    i = pl.program_id(2)
    heads = q_ref.shape[1] // HEAD_PAD
    feat = lambda hd: slice(hd * HEAD_PAD, (hd + 1) * HEAD_PAD)

    m_ref[...] = jnp.full(m_ref.shape, NEG_INF, F32)
    acc_ref[...] = jnp.zeros(acc_ref.shape, F32)

    def advance(hd, s, v, lanes):
        m = m_ref[hd, :, lanes]
        m_new = jnp.maximum(m, jnp.max(s, axis=0, keepdims=True))
        p = jnp.exp2(s - m_new).astype(BF16)
        acc_ref[hd, :, lanes] = jnp.exp2(m - m_new) * acc_ref[hd, :, lanes] + _bdot(v, p)
        m_ref[hd, :, lanes] = m_new

    def step(t, key_lo, n_keys, query_lo, masked):
        keys = pl.ds(pl.multiple_of(t * TK + key_lo, n_keys), n_keys)
        lanes = slice(query_lo, TQ)
        ss = [_bdot(k_ref[0, keys, feat(hd)], q_ref[0, feat(hd), lanes]) for hd in range(heads)]
        if masked:
            kc = (lax.broadcasted_iota(jnp.int32, ss[0].shape, 0) + key_lo) // CHUNK
            qc = (lax.broadcasted_iota(jnp.int32, ss[0].shape, 1) + query_lo) // CHUNK
            ss = [jnp.where(kc <= qc, s, NEG_INF) for s in ss]
        for hd in range(heads):
            advance(hd, ss[hd], v_ref[0, t, feat(hd), key_lo:key_lo + n_keys], lanes)

    def body(t, c):
        step(t, 0, TK, 0, False)
        return c

    lax.fori_loop(0, i, body, 0)
    step(i, 0, TK // 2, 0, True)
    step(i, TK // 2, TK // 2, TQ // 2, True)
    outs = []
    for hd in range(heads):
        acc = acc_ref[hd]
        outs.append((acc / acc[V_DIM:V_DIM + 1]).T[:, :V_DIM])
    o_ref[0] = jnp.concatenate(outs, axis=-1).astype(BF16)


def _attention(q, k, v):
    B, S, _ = k.shape
    hp = ATTN_HEADS_PER_STEP
    assert TQ == TK
    return pl.pallas_call(
        _attn_kernel,
        grid=(B, MLA_HEADS // hp, S // TQ),
        in_specs=[
            pl.BlockSpec((1, hp * HEAD_PAD, TQ), lambda b, h, i: (b, h, i)),
            pl.BlockSpec((1, S, hp * HEAD_PAD), lambda b, h, i: (b, 0, h)),
            pl.BlockSpec((1, S // TK, hp * HEAD_PAD, TK), lambda b, h, i: (b, 0, h, 0)),
        ],
        out_specs=pl.BlockSpec((1, TQ, hp * V_DIM), lambda b, h, i: (b, i, h)),
        out_shape=jax.ShapeDtypeStruct((B, S, MLA_HEADS * V_DIM), BF16),
        scratch_shapes=[
            pltpu.VMEM((hp, 1, TQ), F32),
            pltpu.VMEM((hp, HEAD_PAD, TQ), F32),
        ],
        compiler_params=pltpu.CompilerParams(
            dimension_semantics=("arbitrary", "arbitrary", "arbitrary"), vmem_limit_bytes=VMEM_LIMIT),
        name="attention",
    )(q, k, v)


def _mix_out_kernel(o_ref, y_ref, x_ref, wo_ref, postg_ref, gm_ref, preg_ref, scf_ref, shf_ref,
                    xo_ref, h_ref):
    no = o_ref.shape[1]
    y = _bdot(o_ref[...], wo_ref[0:no]) + _bdot(y_ref[...], wo_ref[no:])
    x = x_ref[...] + _rms(y, gm_ref[0] * postg_ref[...])
    xo_ref[...] = x
    h_ref[...] = (_rms(x, preg_ref[...] * (1.0 + scf_ref[0])) + shf_ref[0]).astype(BF16)


def _store_token_tiles(ref, x):
    rows, d = x.shape
    assert d == SUBLANE * LANE
    for j in range(SUBLANE):
        ref[pl.ds(j, rows, stride=SUBLANE), :] = x[:, j * LANE:(j + 1) * LANE]


def _load_token_tiles(ref):
    rows = ref.shape[0] // SUBLANE
    return jnp.concatenate([ref[pl.ds(j, rows, stride=SUBLANE), :] for j in range(SUBLANE)], axis=-1)


def _token_copy(src, dst, src_token, dst_token, sem):
    def rows(t):
        start = t * SUBLANE
        return pl.ds(start if isinstance(t, int) else pl.multiple_of(start, SUBLANE), SUBLANE)

    return pltpu.make_async_copy(src.at[rows(src_token)], dst.at[rows(dst_token)], sem)


def _mix_out_moe_kernel(o_ref, y_ref, x_ref, wo_ref, postg_ref, gm_ref, preg_ref, scf_ref, shf_ref,
                        router_ref, xo_ref, h_ref, route_ref):
    no = o_ref.shape[1]
    y = _bdot(o_ref[...], wo_ref[0:no]) + _bdot(y_ref[...], wo_ref[no:])
    x = x_ref[...] + _rms(y, gm_ref[0] * postg_ref[...])
    xo_ref[...] = x
    h = _rms(x, preg_ref[...] * (1.0 + scf_ref[0])) + shf_ref[0]
    _store_token_tiles(h_ref, h)
    h_hi = h.astype(BF16)
    h_lo = (h - h_hi.astype(F32)).astype(BF16)
    t = _bdot(h_hi, router_ref[...])
    logits = t[:, :LANE] + t[:, LANE:] + _bdot(h_lo, router_ref[:, :LANE])
    lane = lax.broadcasted_iota(jnp.int32, logits.shape, 1)
    logits = jnp.where(lane < N_EXPERTS, logits, -jnp.inf)
    m1 = jnp.max(logits, axis=-1, keepdims=True)
    i1 = jnp.min(jnp.where(logits == m1, lane, LANE), axis=-1, keepdims=True)
    rest = jnp.where(lane == i1, -jnp.inf, logits)
    m2 = jnp.max(rest, axis=-1, keepdims=True)
    i2 = jnp.min(jnp.where(rest == m2, lane, LANE), axis=-1, keepdims=True)
    e = jnp.exp(m2 - m1)
    g1 = 1.0 / (1.0 + e)
    route_ref[...] = (jnp.where(lane == 0, i1.astype(F32), 0.0) + jnp.where(lane == 1, i2.astype(F32), 0.0)
                      + jnp.where(lane == 2, g1, 0.0) + jnp.where(lane == 3, e * g1, 0.0))


def _mix_out(o, y, x, mod, l, p, router, tiles_per_batch):
    N, D = x.shape
    tm = TM_OUT
    row = lambda w: pl.BlockSpec((tm, w), lambda i: (i, 0))
    batch = lambda i: i // tiles_per_batch
    in_specs = [row(o.shape[1]), row(y.shape[1]), row(D), _layer_block(p["w_o"], l),
                _layer_block(p["post_mix_g"], l), _mod_block(mod, l, 2, batch), _layer_block(p["pre_ffn_g"], l),
                _mod_block(mod, l, 4, batch), _mod_block(mod, l, 3, batch)]
    out_specs = [row(D), row(D)]
    out_shape = [jax.ShapeDtypeStruct((N, D), F32), jax.ShapeDtypeStruct((N, D), BF16)]
    args = [o, y, x, p["w_o"], p["post_mix_g"], mod, p["pre_ffn_g"], mod, mod]
    body = _mix_out_kernel
    if router is not None:
        in_specs.append(_layer_block(router, l // 2))
        out_specs = [row(D), pl.BlockSpec((tm * SUBLANE, LANE), lambda i: (i, 0)), row(LANE)]
        out_shape = [jax.ShapeDtypeStruct((N, D), F32), jax.ShapeDtypeStruct((N * SUBLANE, LANE), F32),
                     jax.ShapeDtypeStruct((N, LANE), F32)]
        args.append(router)
        body = _mix_out_moe_kernel
    return pl.pallas_call(
        body,
        grid=(N // tm,),
        in_specs=in_specs,
        out_specs=out_specs,
        out_shape=out_shape,
        compiler_params=pltpu.CompilerParams(dimension_semantics=("arbitrary",), vmem_limit_bytes=VMEM_LIMIT),
        name="mix_out",
    )(*args)


def _swiglu(h, wg, wu, wd):
    g = _bdot(h, wg)
    u = _bdot(h, wu)
    return _bdot((_silu(g) * u).astype(BF16), wd)


def _ffn_kernel(h_ref, x_ref, wg_ref, wu_ref, wd_ref, postg_ref, gf_ref, o_ref, acc_ref):
    j = pl.program_id(1)

    @pl.when(j == 0)
    def _():
        acc_ref[...] = jnp.zeros_like(acc_ref)

    acc_ref[...] += _swiglu(h_ref[...], wg_ref[...], wu_ref[...], wd_ref[...])

    @pl.when(j == pl.num_programs(1) - 1)
    def _():
        o_ref[...] = x_ref[...] + _rms(acc_ref[...], gf_ref[0] * postg_ref[...])


def _ffn(h, x, mod, l, p, tiles_per_batch):
    N, D = x.shape
    wg, wu, wd = p["ffn_w_gate"], p["ffn_w_up"], p["ffn_w_down"]
    FF = wg.shape[2]
    tm = TM_FFN
    tf = FF // 2
    lf = l // 2
    return pl.pallas_call(
        _ffn_kernel,
        grid=(N // tm, FF // tf),
        in_specs=[
            pl.BlockSpec((tm, D), lambda i, j: (i, 0)),
            pl.BlockSpec((tm, D), lambda i, j: (i, 0)),
            pl.BlockSpec((None, D, tf), lambda i, j: (lf, 0, j)),
            pl.BlockSpec((None, D, tf), lambda i, j: (lf, 0, j)),
            pl.BlockSpec((None, tf, D), lambda i, j: (lf, j, 0)),
            _layer_block(p["post_ffn_g"], l),
            _mod_block(mod, l, 5, lambda i, j: i // tiles_per_batch),
        ],
        out_specs=pl.BlockSpec((tm, D), lambda i, j: (i, 0)),
        out_shape=jax.ShapeDtypeStruct((N, D), F32),
        scratch_shapes=[pltpu.VMEM((tm, D), F32)],
        compiler_params=pltpu.CompilerParams(
            dimension_semantics=("arbitrary", "arbitrary"), vmem_limit_bytes=VMEM_LIMIT),
        name="ffn_dense",
    )(h, x, wg, wu, wd, p["post_ffn_g"], mod)


def _moe_dispatch_kernel(pad_lo_ref, pad_hi_ref, slot_ref, h_ref, hs_hbm, zero_ref, sem):
    tm = h_ref.shape[0] // SUBLANE

    @pl.when(pl.program_id(0) == 0)
    def _():
        zero_ref[...] = jnp.zeros(zero_ref.shape, zero_ref.dtype)
        for e in range(pad_lo_ref.shape[0]):
            lo, hi = pad_lo_ref[e], pad_hi_ref[e]

            def start_pad(s, c):
                _token_copy(zero_ref, hs_hbm, 0, s, sem).start()
                return c

            def wait_pad(s, c):
                _token_copy(zero_ref, hs_hbm, 0, s, sem).wait()
                return c

            lax.fori_loop(lo, hi, start_pad, 0)
            lax.fori_loop(lo, hi, wait_pad, 0)

    def start(r, c):
        for k in range(2):
            _token_copy(h_ref, hs_hbm, r, slot_ref[0, 0, 2 * r + k], sem).start(priority=k)
        return c

    lax.fori_loop(0, tm, start, 0, unroll=4)

    def wait(r, c):
        for k in range(2):
            _token_copy(h_ref, hs_hbm, r, 0, sem).wait()
        return c

    lax.fori_loop(0, tm, wait, 0, unroll=4)


def _moe_dispatch(pad_lo, pad_hi, slots, h_tiles, n_slots):
    N = h_tiles.shape[0] // SUBLANE
    tm = MOE_DMA_BLOCK
    return pl.pallas_call(
        _moe_dispatch_kernel,
        grid_spec=pltpu.PrefetchScalarGridSpec(
            num_scalar_prefetch=2,
            grid=(N // tm,),
            in_specs=[
                pl.BlockSpec((1, 1, 2 * tm), lambda i, lo, hi: (i, 0, 0), memory_space=pltpu.SMEM),
                pl.BlockSpec((tm * SUBLANE, LANE), lambda i, lo, hi: (i, 0)),
            ],
            out_specs=pl.BlockSpec(memory_space=pl.ANY),
            scratch_shapes=[pltpu.VMEM((SUBLANE, LANE), F32), pltpu.SemaphoreType.DMA(())],
        ),
        out_shape=jax.ShapeDtypeStruct((n_slots * SUBLANE, LANE), F32),
        compiler_params=pltpu.CompilerParams(dimension_semantics=("arbitrary",)),
        name="moe_dispatch",
    )(pad_lo, pad_hi, slots.reshape(N // tm, 1, 2 * tm), h_tiles)


def _moe_expert_kernel(texp_ref, nused_ref, hs_ref, wg_ref, wu_ref, wd_ref, ys_ref):
    del texp_ref

    @pl.when(pl.program_id(0) < nused_ref[0])
    def _():
        x = _load_token_tiles(hs_ref).astype(BF16)
        cast = lambda w_ref: w_ref[...].astype(BF16)
        _store_token_tiles(ys_ref, _swiglu(x, cast(wg_ref), cast(wu_ref), cast(wd_ref)))

    @pl.when(pl.program_id(0) >= nused_ref[0])
    def _():
        ys_ref[...] = jnp.zeros(ys_ref.shape, F32)


def _moe_experts(tile_expert, n_used, hs, wg, wu, wd, lm):
    n_slots = hs.shape[0] // SUBLANE
    _, E, D, FF = wg.shape
    t = MOE_TILE
    tile = pl.BlockSpec((t * SUBLANE, LANE), lambda i, te, nu: (i, 0))
    return pl.pallas_call(
        _moe_expert_kernel,
        grid_spec=pltpu.PrefetchScalarGridSpec(
            num_scalar_prefetch=2,
            grid=(n_slots // t,),
            in_specs=[
                tile,
                pl.BlockSpec((None, None, D, FF), lambda i, te, nu: (lm, te[i], 0, 0)),
                pl.BlockSpec((None, None, D, FF), lambda i, te, nu: (lm, te[i], 0, 0)),
                pl.BlockSpec((None, None, FF, D), lambda i, te, nu: (lm, te[i], 0, 0)),
            ],
            out_specs=tile,
        ),
        out_shape=jax.ShapeDtypeStruct(hs.shape, F32),
        compiler_params=pltpu.CompilerParams(dimension_semantics=("arbitrary",), vmem_limit_bytes=VMEM_LIMIT),
        name="moe_experts",
    )(tile_expert, n_used, hs, wg, wu, wd)


def _moe_combine_kernel(slot_ref, next_slot_ref, ys_hbm, route_ref, x_ref, postg_ref, gf_ref, o_ref, buf_ref, sem):
    tm = x_ref.shape[0]
    i = pl.program_id(0)
    cur = lax.rem(i, 2)

    def gather(slots, half):
        def start(r, c):
            for k in range(2):
                _token_copy(ys_hbm, buf_ref.at[half, k], slots[0, 0, 2 * r + k], r, sem.at[half]).start(priority=k)
            return c

        lax.fori_loop(0, tm, start, 0, unroll=4)

    @pl.when(i == 0)
    def _():
        gather(slot_ref, 0)

    @pl.when(i + 1 < pl.num_programs(0))
    def _():
        gather(next_slot_ref, 1 - cur)

    def wait(r, c):
        for k in range(2):
            _token_copy(ys_hbm, buf_ref.at[cur, k], 0, r, sem.at[cur]).wait()
        return c

    lax.fori_loop(0, tm, wait, 0, unroll=4)
    route = route_ref[...]
    lane = lax.broadcasted_iota(jnp.int32, route.shape, 1)
    f = None
    for k in range(2):
        gate = jnp.sum(jnp.where(lane == 2 + k, route, 0.0), axis=-1, keepdims=True)
        y = _load_token_tiles(buf_ref.at[cur, k])
        f = gate * y if f is None else f + gate * y
    o_ref[...] = x_ref[...] + _rms(f, gf_ref[0] * postg_ref[...])


def _moe_combine(slots, ys, route, x, mod, l, p, tiles_per_batch):
    N, D = x.shape
    tm = TM_FFN
    n = N // tm
    slots = slots.reshape(n, 1, 2 * tm)
    return pl.pallas_call(
        _moe_combine_kernel,
        grid=(n,),
        in_specs=[
            pl.BlockSpec((1, 1, 2 * tm), lambda i: (i, 0, 0), memory_space=pltpu.SMEM),
            pl.BlockSpec((1, 1, 2 * tm), lambda i: (jnp.minimum(i + 1, n - 1), 0, 0), memory_space=pltpu.SMEM),
            pl.BlockSpec(memory_space=pl.ANY),
            pl.BlockSpec((tm, LANE), lambda i: (i, 0)),
            pl.BlockSpec((tm, D), lambda i: (i, 0)),
            _layer_block(p["post_ffn_g"], l),
            _mod_block(mod, l, 5, lambda i: i // tiles_per_batch),
        ],
        out_specs=pl.BlockSpec((tm, D), lambda i: (i, 0)),
        out_shape=jax.ShapeDtypeStruct((N, D), F32),
        scratch_shapes=[pltpu.VMEM((2, 2, tm * SUBLANE, LANE), F32), pltpu.SemaphoreType.DMA((2,))],
        compiler_params=pltpu.CompilerParams(dimension_semantics=("arbitrary",), vmem_limit_bytes=VMEM_LIMIT),
        name="moe_combine",
    )(slots, slots, ys, route, x, p["post_ffn_g"], mod)


def _moe_plan(route, n_slots):
    t = MOE_TILE
    pair_expert = route[:, :2].astype(jnp.int32).reshape(-1)
    onehot = (pair_expert[:, None] == jnp.arange(N_EXPERTS, dtype=jnp.int32)[None, :]).astype(jnp.int32)
    csum = jnp.cumsum(onehot, axis=0)
    count = csum[-1]
    padded = (count + t - 1) // t * t
    ends = jnp.cumsum(padded)
    starts = ends - padded
    slot = jnp.sum(onehot * (csum - 1 + starts[None, :]), axis=1)
    n_used = ends[-1:] // t
    tile_start = jnp.arange(n_slots // t, dtype=jnp.int32) * t
    tile_expert = jnp.sum((tile_start[:, None] >= ends[None, :]).astype(jnp.int32), axis=1)
    last_used = tile_expert[jnp.maximum(n_used[0] - 1, 0)]
    tile_expert = jnp.where(tile_start < ends[-1], tile_expert, last_used)
    pad_lo = jnp.concatenate([starts + count, ends[-1:]]).astype(jnp.int32)
    pad_hi = jnp.concatenate([ends, jnp.full((1,), n_slots, ends.dtype)]).astype(jnp.int32)
    return slot, pad_lo, pad_hi, tile_expert, n_used.astype(jnp.int32)


def _moe(h_tiles, x, route, mod, l, p, tiles_per_batch):
    N = x.shape[0]
    n_slots = 2 * N + N_EXPERTS * MOE_TILE
    slot, pad_lo, pad_hi, tile_expert, n_used = _moe_plan(route, n_slots)
    hs = _moe_dispatch(pad_lo, pad_hi, slot, h_tiles, n_slots)
    ys = _moe_experts(tile_expert, n_used, hs, p["moe_w_gate"], p["moe_w_up"], p["moe_w_down"], l // 2)
    return _moe_combine(slot, ys, route, x, mod, l, p, tiles_per_batch)


def _rope_partner(w):
    half = QK_ROPE // 2
    return jnp.concatenate([-w[..., half:], w[..., :half]], axis=-1)


def _rope_tables(S):
    pos = jnp.arange(S, dtype=F32)
    inv = ROPE_THETA ** (-jnp.arange(0, QK_ROPE, 2, dtype=F32) / QK_ROPE)
    ang = pos[:, None] * inv[None, :]
    cos, sin = jnp.cos(ang), jnp.sin(ang)
    cos2 = jnp.concatenate([cos, cos], axis=-1)
    sin2 = jnp.concatenate([sin, sin], axis=-1)
    tk = jnp.concatenate([cos2, sin2, jnp.zeros((S, LANE - 2 * QK_ROPE), F32)], axis=-1)
    return {"cosq": cos2.T * Q_SCALE, "sinq": sin2.T * Q_SCALE, "tk": tk}


def _prepare_params(w_in, q_norm_g, w_uq, kv_norm_g, w_ukv, sgu_ln_g, sgu_ln_b, sgu_w, sgu_b, conv_w, conv_b,
                    lru_wa, lru_ba, lru_wx, lru_bx, lru_lambda, pre_mix_g, post_mix_g, w_o, pre_ffn_g, post_ffn_g,
                    ffn_w_gate, ffn_w_up, ffn_w_down, moe_router, moe_w_gate, moe_w_up, moe_w_down):
    L, D, _ = w_in.shape
    o_q, o_kv = 0, Q_LORA
    o_kr = o_kv + KV_LORA
    o_u = o_kr + QK_ROPE
    o_v = o_u + SGU_WIDTH
    o_x = o_v + SGU_WIDTH
    o_y = o_x + LRU_WIDTH
    w_kr = w_in[..., o_kr:o_kr + QK_ROPE]
    w_in_p = jnp.concatenate([
        w_in[..., o_q:o_q + Q_LORA], w_in[..., o_kv:o_kv + KV_LORA], w_in[..., o_u:o_u + SGU_WIDTH],
        w_in[..., o_v:o_v + SGU_WIDTH], w_in[..., o_x:o_x + LRU_WIDTH], w_in[..., o_y:o_y + LRU_WIDTH],
        w_kr, _rope_partner(w_kr), jnp.zeros((L, D, LANE - 2 * QK_ROPE), F32)], axis=-1).astype(BF16)

    H = MLA_HEADS
    pad = HEAD_PAD - QK_NOPE - QK_ROPE
    wq = w_uq.reshape(L, Q_LORA, H, QK_NOPE + QK_ROPE)
    assert pad == QK_ROPE
    w_q = jnp.concatenate([wq, _rope_partner(wq[..., QK_NOPE:])], axis=-1).reshape(L, Q_LORA, H * HEAD_PAD)
    w_q = jnp.swapaxes(w_q, 1, 2).astype(BF16)

    wkv = w_ukv.reshape(L, KV_LORA, H, QK_NOPE + V_DIM)
    wk_lat = jnp.concatenate([wkv[..., :QK_NOPE], jnp.zeros((L, KV_LORA, H, HEAD_PAD - QK_NOPE), F32)],
                             axis=-1).reshape(L, KV_LORA, H * HEAD_PAD)
    eye = jnp.eye(QK_ROPE, dtype=F32)
    place = jnp.concatenate([jnp.zeros((QK_ROPE, QK_NOPE), F32), eye, jnp.zeros((QK_ROPE, pad), F32)], axis=-1)
    place = jnp.tile(place, (2, H))
    tail = jnp.concatenate([place, jnp.zeros((LANE - 2 * QK_ROPE, H * HEAD_PAD), F32)], axis=0)
    w_k = jnp.concatenate([wk_lat, jnp.broadcast_to(tail, (L,) + tail.shape)], axis=1).astype(BF16)
    w_v = jnp.swapaxes(
        jnp.concatenate([wkv[..., QK_NOPE:], jnp.zeros((L, KV_LORA, H, HEAD_PAD - V_DIM), F32)],
                        axis=-1).reshape(L, KV_LORA, H * HEAD_PAD), 1, 2).astype(BF16)

    def block_diag(w):
        sel = jnp.eye(LRU_HEADS, dtype=F32)
        return jnp.einsum("lhij,hg->lhigj", w, sel).reshape(L, LRU_WIDTH, LRU_WIDTH).astype(BF16)

    row = lambda a: a.reshape(a.shape[0], 1, -1)
    r = jnp.pad(moe_router, ((0, 0), (0, 0), (0, LANE - N_EXPERTS)))
    r_hi = r.astype(BF16)
    return {
        "pre_mix_g": row(pre_mix_g), "w_in": w_in_p, "qg": row(q_norm_g), "w_q": w_q, "kvg": row(kv_norm_g),
        "w_k": w_k, "w_v": w_v, "ln_g": row(sgu_ln_g), "ln_b": row(sgu_ln_b),
        "w_s": sgu_w.reshape(L, SGU_GROUPS * SGU_BLOCK, SGU_BLOCK),
        "b_s": jnp.repeat(jnp.swapaxes(sgu_b, 1, 2), SGU_GROUP_DIM, axis=2),
        "conv_w": conv_w, "conv_b": row(conv_b), "wa": block_diag(lru_wa), "ba": row(lru_ba),
        "wx": block_diag(lru_wx), "bx": row(lru_bx), "lam": row(lru_lambda),
        "w_o": w_o.astype(BF16), "post_mix_g": row(post_mix_g), "pre_ffn_g": row(pre_ffn_g),
        "post_ffn_g": row(post_ffn_g),
        "ffn_w_gate": ffn_w_gate.astype(BF16), "ffn_w_up": ffn_w_up.astype(BF16),
        "ffn_w_down": ffn_w_down.astype(BF16),
        "router": jnp.concatenate([r_hi, (r - r_hi.astype(F32)).astype(BF16)], axis=-1),
        "moe_w_gate": moe_w_gate, "moe_w_up": moe_w_up, "moe_w_down": moe_w_down,
    }


def kernel(x, c, w_mod, b_mod, pre_mix_g, post_mix_g, w_in, q_norm_g, w_uq, kv_norm_g, w_ukv, sgu_ln_g,
           sgu_ln_b, sgu_w, sgu_b, conv_w, conv_b, lru_wa, lru_ba, lru_wx, lru_bx, lru_lambda, w_o,
           pre_ffn_g, post_ffn_g, ffn_w_gate, ffn_w_up, ffn_w_down, moe_router, moe_w_gate, moe_w_up,
           moe_w_down):
    B, S, D = x.shape
    L = w_mod.shape[0]
    N = B * S
    tabs = _rope_tables(S)
    p = _prepare_params(w_in, q_norm_g, w_uq, kv_norm_g, w_ukv, sgu_ln_g, sgu_ln_b, sgu_w, sgu_b, conv_w, conv_b,
                        lru_wa, lru_ba, lru_wx, lru_bx, lru_lambda, pre_mix_g, post_mix_g, w_o, pre_ffn_g,
                        post_ffn_g, ffn_w_gate, ffn_w_up, ffn_w_down, moe_router, moe_w_gate, moe_w_up, moe_w_down)
    mod = _modulation(c, w_mod, b_mod).reshape(L, B * 6, 1, D)
    xf = x.reshape(N, D)
    for l in range(L):
        q, k, v, ymix = _mix_in(xf.reshape(B, S, D), mod, l, p, tabs)
        o = _attention(q, k, v)
        moe = l % 2 == 1
        res = _mix_out(o.reshape(N, -1), ymix.reshape(N, -1), xf, mod, l, p, p["router"] if moe else None,
                       S // TM_OUT)
        if moe:
            xf, h_tiles, route = res
            xf = _moe(h_tiles, xf, route, mod, l, p, S // TM_FFN)
        else:
            xf, h2 = res
            xf = _ffn(h2, xf, mod, l, p, S // TM_FFN)
    return xf.reshape(B, S, D)
```

```python
import jax
import jax.numpy as jnp
from jax import lax
from jax.experimental import pallas as pl
from jax.experimental.pallas import tpu as pltpu

F32 = jnp.float32
BF16 = jnp.bfloat16

MLA_HEADS = 8
QK_NOPE = 64
QK_ROPE = 32
V_DIM = 64
Q_LORA = 256
KV_LORA = 128
ROPE_THETA = 10000.0
CHUNK = 64
SGU_GROUPS = 4
SGU_GROUP_DIM = 64
SGU_WIDTH = SGU_GROUPS * SGU_GROUP_DIM
SGU_BLOCK = 128
LRU_HEADS = 4
LRU_HEAD_DIM = 64
LRU_WIDTH = LRU_HEADS * LRU_HEAD_DIM
CONV_W = 4
LRU_C = 8.0
N_EXPERTS = 8
NEG_INF = -1e30
RMS_EPS = 1e-6
LN_EPS = 1e-5

LANE = 128
SUBLANE = 8
HEAD_PAD = LANE
V_ROWS = 80
Q_SCALE = (QK_NOPE + QK_ROPE) ** -0.5 * 1.4426950408889634

C_Q = 0
C_KV = C_Q + Q_LORA
C_U = C_KV + KV_LORA
C_V = C_U + SGU_WIDTH
C_X = C_V + SGU_WIDTH
C_Y = C_X + LRU_WIDTH
C_KR = C_Y + LRU_WIDTH
D_IN_PAD = C_KR + LANE

VMEM_LIMIT = 56 * 1024 * 1024

TM_MIX = 512
TQ = 512
TK = 512
ATTN_HEADS_PER_STEP = 8
AHEAD = 3
TM_OUT = 512
TM_FFN = 512
MOE_TILE = 512
MOE_DMA_BLOCK = 2048


def _rms(x, g):
    return x * lax.rsqrt(jnp.mean(x * x, axis=-1, keepdims=True) + RMS_EPS) * g


def _gelu(x):
    k = -2.0 * 0.7978845608028654 * 1.4426950408889634
    return x / (1.0 + jnp.exp2(x * (k + (k * 0.044715) * (x * x))))


def _sigmoid(x):
    return 1.0 / (1.0 + jnp.exp(-x))


def _silu(x):
    return x * _sigmoid(x)


def _bdot(a, b):
    return jnp.dot(a, b, preferred_element_type=F32)


def _layer_block(a, l):
    rest = a.shape[1:]
    return pl.BlockSpec((None,) + rest, lambda *_: (l,) + (0,) * len(rest))


def _mod_block(mod, l, j, batch_of):
    D = mod.shape[-1]
    return pl.BlockSpec((None, 1, 1, D), lambda *g: (l, 6 * batch_of(*g) + j, 0, 0))


def _split_bf16(a):
    hi = a.astype(BF16)
    return hi, (a - hi.astype(F32)).astype(BF16)


def _mod_kernel(c_ref, w_ref, b_ref, o_ref):
    c = c_ref[...]
    c_hi, c_lo = _split_bf16(_silu(c))
    w_hi, w_lo = _split_bf16(w_ref[0])
    o_ref[0] = _bdot(c_hi, w_hi) + (_bdot(c_hi, w_lo) + _bdot(c_lo, w_hi)) + b_ref[0]


def _modulation(c, w_mod, b_mod):
    L, D, D6 = w_mod.shape
    B = c.shape[0]
    return pl.pallas_call(
        _mod_kernel,
        grid=(L, D6 // D),
        in_specs=[
            pl.BlockSpec((B, D), lambda l, j: (0, 0)),
            pl.BlockSpec((1, D, D), lambda l, j: (l, 0, j)),
            pl.BlockSpec((1, 1, D), lambda l, j: (l, 0, j)),
        ],
        out_specs=pl.BlockSpec((1, B, D), lambda l, j: (l, 0, j)),
        out_shape=jax.ShapeDtypeStruct((L, B, D6), F32),
        compiler_params=pltpu.CompilerParams(vmem_limit_bytes=VMEM_LIMIT),
        name="modulation",
    )(c, w_mod, b_mod.reshape(L, 1, D6))


def _shift_rows(cur, prev8, k):
    rolled = pltpu.roll(cur, k, 0)
    fix = pltpu.roll(prev8, k, 0)
    row = lax.broadcasted_iota(jnp.int32, fix.shape, 0)
    top = jnp.where(row < k, fix, rolled[:SUBLANE])
    return jnp.concatenate([top, rolled[SUBLANE:]], axis=0)


def _mix_in_kernel(x_ref, sh_ref, sc_ref, preg_ref, win_ref, qg_ref, wq_ref, kvg_ref, wk_ref, wv_ref,
                   cosq_ref, sinq_ref, tk_ref, lng_ref, lnb_ref, ws_ref, bs_ref,
                   cw_ref, cb_ref, wa_ref, ba_ref, wx_ref, bx_ref, lam_ref,
                   q_ref, k_ref, v_ref, y_ref, prev_ref, hc_ref):
    tm = x_ref.shape[1]

    @pl.when(pl.program_id(1) == 0)
    def _():
        prev_ref[...] = jnp.zeros_like(prev_ref)
        hc_ref[...] = jnp.zeros_like(hc_ref)

    x = x_ref[0]
    h = _rms(x, preg_ref[...] * (1.0 + sc_ref[0])) + sh_ref[0]
    z = _bdot(h.astype(BF16), win_ref[...])

    nt = (((1,), (1,)), ((), ()))
    qn = _rms(z[:, C_Q:C_Q + Q_LORA], qg_ref[...]).astype(BF16)
    qq = lax.dot_general(wq_ref[...], qn, nt, preferred_element_type=F32)
    cosq = cosq_ref[...]
    sinq = sinq_ref[...]
    for hd in range(MLA_HEADS):
        lo = hd * HEAD_PAD
        r0, p0 = lo + QK_NOPE, lo + QK_NOPE + QK_ROPE
        q_ref[0, lo:r0, :] = (qq[lo:r0] * Q_SCALE).astype(BF16)
        q_ref[0, r0:p0, :] = (qq[r0:p0] * cosq + qq[p0:p0 + QK_ROPE] * sinq).astype(BF16)
        q_ref[0, p0:p0 + QK_ROPE, :] = qq[p0:p0 + QK_ROPE].astype(BF16)
    kvn = _rms(z[:, C_KV:C_KV + KV_LORA], kvg_ref[...]).astype(BF16)
    kr = (z[:, C_KR:C_KR + LANE] * tk_ref[...]).astype(BF16)
    k_in = jnp.concatenate([kvn, kr], axis=-1)
    k_ref[0] = _bdot(k_in, wk_ref[...]).astype(BF16)
    vv = lax.dot_general(wv_ref[...], kvn, nt, preferred_element_type=F32)
    ones_row = lax.broadcasted_iota(jnp.int32, vv.shape, 0) % V_ROWS == V_DIM
    vv = jnp.where(ones_row, 1.0, vv).astype(BF16)
    for c in range(tm // TK):
        v_ref[0, c] = vv[:, c * TK:(c + 1) * TK]

    u = _gelu(z[:, C_U:C_U + SGU_WIDTH])
    gv = _gelu(z[:, C_V:C_V + SGU_WIDTH])
    mu = jnp.mean(gv, axis=-1, keepdims=True)
    var = jnp.mean(jnp.square(gv - mu), axis=-1, keepdims=True)
    vn = ((gv - mu) * lax.rsqrt(var + LN_EPS) * lng_ref[...] + lnb_ref[...]).astype(BF16)
    ws = ws_ref[...]
    r_t = lax.broadcasted_iota(jnp.int32, ws.shape, 0) % SGU_BLOCK
    c_s = lax.broadcasted_iota(jnp.int32, ws.shape, 1)
    ws = jnp.where(c_s <= r_t, ws, 0.0).astype(BF16)
    grp = lax.broadcasted_iota(jnp.int32, (SGU_BLOCK, SGU_WIDTH), 1) // SGU_GROUP_DIM
    for blk in range(tm // SGU_BLOCK):
        r0 = blk * SGU_BLOCK
        res = _bdot(ws, vn[r0:r0 + SGU_BLOCK])
        mixed = bs_ref[...]
        for g in range(SGU_GROUPS):
            mixed = mixed + jnp.where(grp == g, res[g * SGU_BLOCK:(g + 1) * SGU_BLOCK], 0.0)
        y_ref[0, r0:r0 + SGU_BLOCK, 0:SGU_WIDTH] = (u[r0:r0 + SGU_BLOCK] * mixed).astype(BF16)

    zx = z[:, C_X:C_X + LRU_WIDTH]
    prev8 = prev_ref[...]
    cw = cw_ref[...]
    xc = cb_ref[...] + zx * cw[CONV_W - 1:CONV_W]
    for k in range(1, CONV_W):
        xc = xc + _shift_rows(zx, prev8, k) * cw[CONV_W - 1 - k:CONV_W - k]
    prev_ref[...] = zx[tm - SUBLANE:]
    xcb = xc.astype(BF16)
    r = _sigmoid(_bdot(xcb, wa_ref[...]) + ba_ref[...])
    ig = _sigmoid(_bdot(xcb, wx_ref[...]) + bx_ref[...])
    lam = lam_ref[...]
    softplus_neg_lam = jnp.maximum(-lam, 0.0) + jnp.log(1.0 + jnp.exp(-jnp.abs(lam)))
    log_a = -LRU_C * r * softplus_neg_lam
    a = jnp.exp(log_a)
    t = jnp.tanh(log_a)
    b = jnp.sqrt(-2.0 * t / (1.0 - t)) * (ig * xc)
    row = lax.broadcasted_iota(jnp.int32, a.shape, 0)
    k = 1
    while k < SUBLANE:
        valid = row >= k
        a_sh = pltpu.roll(a, k, 0)
        b_sh = pltpu.roll(b, k, 0)
        b = jnp.where(valid, b + a * b_sh, b)
        a = jnp.where(valid, a * a_sh, a)
        k *= 2
    while k < tm:
        b = jnp.concatenate([b[:k], b[k:] + a[k:] * b[:-k]], axis=0)
        a = jnp.concatenate([a[:k], a[k:] * a[:-k]], axis=0)
        k *= 2
    hs = b + a * hc_ref[0:1]
    hc_ref[0:1] = hs[tm - 1:tm]
    y_ref[0, :, SGU_WIDTH:SGU_WIDTH + LRU_WIDTH] = (hs * _gelu(z[:, C_Y:C_Y + LRU_WIDTH])).astype(BF16)


def _mix_in(x, mod, l, p, tabs):
    B, S, D = x.shape
    tm = TM_MIX
    batch = lambda b, s: b
    tab = pl.BlockSpec((tm, LANE), lambda b, s: (s, 0))
    tab_t = pl.BlockSpec((QK_ROPE, tm), lambda b, s: (0, s))
    assert tm % TK == 0
    consts = [p[k] for k in ("pre_mix_g", "w_in", "qg", "w_q", "kvg", "w_k", "w_v")]
    consts2 = [p[k] for k in ("ln_g", "ln_b", "w_s", "b_s", "conv_w", "conv_b", "wa", "ba", "wx", "bx", "lam")]
    qw = MLA_HEADS * HEAD_PAD
    return pl.pallas_call(
        _mix_in_kernel,
        grid=(B, S // tm),
        in_specs=[pl.BlockSpec((1, tm, D), lambda b, s: (b, s, 0)), _mod_block(mod, l, 0, batch),
                  _mod_block(mod, l, 1, batch)]
        + [_layer_block(a, l) for a in consts] + [tab_t, tab_t, tab] + [_layer_block(a, l) for a in consts2],
        out_specs=[
            pl.BlockSpec((1, qw, tm), lambda b, s: (b, 0, s)),
            pl.BlockSpec((1, tm, qw), lambda b, s: (b, s, 0)),
            pl.BlockSpec((1, tm // TK, MLA_HEADS * V_ROWS, TK), lambda b, s: (b, s, 0, 0)),
            pl.BlockSpec((1, tm, SGU_WIDTH + LRU_WIDTH), lambda b, s: (b, s, 0)),
        ],
        out_shape=[
            jax.ShapeDtypeStruct((B, qw, S), BF16),
            jax.ShapeDtypeStruct((B, S, qw), BF16),
            jax.ShapeDtypeStruct((B, S // TK, MLA_HEADS * V_ROWS, TK), BF16),
            jax.ShapeDtypeStruct((B, S, SGU_WIDTH + LRU_WIDTH), BF16),
        ],
        scratch_shapes=[pltpu.VMEM((SUBLANE, LRU_WIDTH), F32), pltpu.VMEM((SUBLANE, LRU_WIDTH), F32)],
        compiler_params=pltpu.CompilerParams(
            dimension_semantics=("arbitrary", "arbitrary"), vmem_limit_bytes=VMEM_LIMIT),
        name="mix_in",
    )(x, mod, mod, *consts, tabs["cosq"], tabs["sinq"], tabs["tk"], *consts2)


def _attn_kernel(q_ref, k_ref, v_ref, o_ref, m_ref, acc_ref):
    i = pl.program_id(2)
    heads = q_ref.shape[1] // HEAD_PAD
    feat = lambda hd: slice(hd * HEAD_PAD, (hd + 1) * HEAD_PAD)

    m_ref[...] = jnp.full(m_ref.shape, NEG_INF, F32)
    acc_ref[...] = jnp.zeros(acc_ref.shape, F32)

    def advance(hd, s, v, lanes):
        m = m_ref[hd, :, lanes]
        m_new = jnp.maximum(m, jnp.max(s, axis=0, keepdims=True))
        p = jnp.exp2(s - m_new).astype(BF16)
        acc_ref[hd, 0:V_ROWS, lanes] = jnp.exp2(m - m_new) * acc_ref[hd, 0:V_ROWS, lanes] + _bdot(v, p)
        m_ref[hd, :, lanes] = m_new

    def step(t, key_lo, n_keys, query_lo, masked):
        keys = pl.ds(pl.multiple_of(t * TK + key_lo, n_keys), n_keys)
        lanes = slice(query_lo, TQ)

        def scores(hd):
            s = _bdot(k_ref[0, keys, feat(hd)], q_ref[0, feat(hd), lanes])
            if masked:
                kc = (lax.broadcasted_iota(jnp.int32, s.shape, 0) + key_lo) // CHUNK
                qc = (lax.broadcasted_iota(jnp.int32, s.shape, 1) + query_lo) // CHUNK
                s = jnp.where(kc <= qc, s, NEG_INF)
            return s

        ss = {hd: scores(hd) for hd in range(min(AHEAD, heads))}
        for hd in range(heads):
            if hd + AHEAD < heads:
                ss[hd + AHEAD] = scores(hd + AHEAD)
            advance(hd, ss.pop(hd), v_ref[0, t, hd * V_ROWS:(hd + 1) * V_ROWS, key_lo:key_lo + n_keys], lanes)

    def body(t, c):
        step(t, 0, TK, 0, False)
        return c

    lax.fori_loop(0, i, body, 0)
    step(i, 0, TK // 2, 0, True)
    step(i, TK // 2, TK // 2, TQ // 2, True)
    outs = []
    for hd in range(heads):
        acc = acc_ref[hd]
        outs.append((acc / acc[V_DIM:V_DIM + 1]).T[:, :V_DIM])
    o_ref[0] = jnp.concatenate(outs, axis=-1).astype(BF16)


def _attention(q, k, v):
    B, S, _ = k.shape
    hp = ATTN_HEADS_PER_STEP
    assert TQ == TK
    return pl.pallas_call(
        _attn_kernel,
        grid=(B, MLA_HEADS // hp, S // TQ),
        in_specs=[
            pl.BlockSpec((1, hp * HEAD_PAD, TQ), lambda b, h, i: (b, h, i)),
            pl.BlockSpec((1, S, hp * HEAD_PAD), lambda b, h, i: (b, 0, h)),
            pl.BlockSpec((1, S // TK, hp * V_ROWS, TK), lambda b, h, i: (b, 0, h, 0)),
        ],
        out_specs=pl.BlockSpec((1, TQ, hp * V_DIM), lambda b, h, i: (b, i, h)),
        out_shape=jax.ShapeDtypeStruct((B, S, MLA_HEADS * V_DIM), BF16),
        scratch_shapes=[
            pltpu.VMEM((hp, 1, TQ), F32),
            pltpu.VMEM((hp, HEAD_PAD, TQ), F32),
        ],
        compiler_params=pltpu.CompilerParams(
            dimension_semantics=("arbitrary", "arbitrary", "arbitrary"), vmem_limit_bytes=VMEM_LIMIT),
        name="attention",
    )(q, k, v)


def _mix_out_kernel(o_ref, y_ref, x_ref, wo_ref, postg_ref, gm_ref, preg_ref, scf_ref, shf_ref,
                    xo_ref, h_ref):
    no = o_ref.shape[1]
    y = _bdot(o_ref[...], wo_ref[0:no]) + _bdot(y_ref[...], wo_ref[no:])
    x = x_ref[...] + _rms(y, gm_ref[0] * postg_ref[...])
    xo_ref[...] = x
    h_ref[...] = (_rms(x, preg_ref[...] * (1.0 + scf_ref[0])) + shf_ref[0]).astype(BF16)


def _store_token_tiles(ref, x):
    rows, d = x.shape
    assert d == SUBLANE * LANE
    for j in range(SUBLANE):
        ref[pl.ds(j, rows, stride=SUBLANE), :] = x[:, j * LANE:(j + 1) * LANE]


def _load_token_tiles(ref):
    rows = ref.shape[0] // SUBLANE
    return jnp.concatenate([ref[pl.ds(j, rows, stride=SUBLANE), :] for j in range(SUBLANE)], axis=-1)


def _token_copy(src, dst, src_token, dst_token, sem):
    def rows(t):
        start = t * SUBLANE
        return pl.ds(start if isinstance(t, int) else pl.multiple_of(start, SUBLANE), SUBLANE)

    return pltpu.make_async_copy(src.at[rows(src_token)], dst.at[rows(dst_token)], sem)


def _mix_out_moe_kernel(o_ref, y_ref, x_ref, wo_ref, postg_ref, gm_ref, preg_ref, scf_ref, shf_ref,
                        router_ref, xo_ref, h_ref, route_ref):
    no = o_ref.shape[1]
    y = _bdot(o_ref[...], wo_ref[0:no]) + _bdot(y_ref[...], wo_ref[no:])
    x = x_ref[...] + _rms(y, gm_ref[0] * postg_ref[...])
    xo_ref[...] = x
    h = _rms(x, preg_ref[...] * (1.0 + scf_ref[0])) + shf_ref[0]
    _store_token_tiles(h_ref, h)
    h_hi, h_lo = _split_bf16(h)
    t = _bdot(h_hi, router_ref[...])
    logits = t[:, :LANE] + t[:, LANE:] + _bdot(h_lo, router_ref[:, :LANE])
    lane = lax.broadcasted_iota(jnp.int32, logits.shape, 1)
    logits = jnp.where(lane < N_EXPERTS, logits, -jnp.inf)
    m1 = jnp.max(logits, axis=-1, keepdims=True)
    i1 = jnp.min(jnp.where(logits == m1, lane, LANE), axis=-1, keepdims=True)
    rest = jnp.where(lane == i1, -jnp.inf, logits)
    m2 = jnp.max(rest, axis=-1, keepdims=True)
    i2 = jnp.min(jnp.where(rest == m2, lane, LANE), axis=-1, keepdims=True)
    e = jnp.exp(m2 - m1)
    g1 = 1.0 / (1.0 + e)
    route_ref[...] = (jnp.where(lane == 0, i1.astype(F32), 0.0) + jnp.where(lane == 1, i2.astype(F32), 0.0)
                      + jnp.where(lane == 2, g1, 0.0) + jnp.where(lane == 3, e * g1, 0.0))


def _mix_out(o, y, x, mod, l, p, router, tiles_per_batch):
    N, D = x.shape
    tm = TM_OUT
    row = lambda w: pl.BlockSpec((tm, w), lambda i: (i, 0))
    batch = lambda i: i // tiles_per_batch
    in_specs = [row(o.shape[1]), row(y.shape[1]), row(D), _layer_block(p["w_o"], l),
                _layer_block(p["post_mix_g"], l), _mod_block(mod, l, 2, batch), _layer_block(p["pre_ffn_g"], l),
                _mod_block(mod, l, 4, batch), _mod_block(mod, l, 3, batch)]
    out_specs = [row(D), row(D)]
    out_shape = [jax.ShapeDtypeStruct((N, D), F32), jax.ShapeDtypeStruct((N, D), BF16)]
    args = [o, y, x, p["w_o"], p["post_mix_g"], mod, p["pre_ffn_g"], mod, mod]
    body = _mix_out_kernel
    if router is not None:
        in_specs.append(_layer_block(router, l // 2))
        out_specs = [row(D), pl.BlockSpec((tm * SUBLANE, LANE), lambda i: (i, 0)), row(LANE)]
        out_shape = [jax.ShapeDtypeStruct((N, D), F32), jax.ShapeDtypeStruct((N * SUBLANE, LANE), F32),
                     jax.ShapeDtypeStruct((N, LANE), F32)]
        args.append(router)
        body = _mix_out_moe_kernel
    return pl.pallas_call(
        body,
        grid=(N // tm,),
        in_specs=in_specs,
        out_specs=out_specs,
        out_shape=out_shape,
        compiler_params=pltpu.CompilerParams(dimension_semantics=("arbitrary",), vmem_limit_bytes=VMEM_LIMIT),
        name="mix_out",
    )(*args)


def _swiglu(h, wg, wu, wd):
    g = _bdot(h, wg)
    u = _bdot(h, wu)
    return _bdot((_silu(g) * u).astype(BF16), wd)


def _ffn_kernel(h_ref, x_ref, wg_ref, wu_ref, wd_ref, postg_ref, gf_ref, o_ref, acc_ref):
    j = pl.program_id(1)

    @pl.when(j == 0)
    def _():
        acc_ref[...] = jnp.zeros_like(acc_ref)

    acc_ref[...] += _swiglu(h_ref[...], wg_ref[...], wu_ref[...], wd_ref[...])

    @pl.when(j == pl.num_programs(1) - 1)
    def _():
        o_ref[...] = x_ref[...] + _rms(acc_ref[...], gf_ref[0] * postg_ref[...])


def _ffn(h, x, mod, l, p, tiles_per_batch):
    N, D = x.shape
    wg, wu, wd = p["ffn_w_gate"], p["ffn_w_up"], p["ffn_w_down"]
    FF = wg.shape[2]
    tm = TM_FFN
    tf = FF // 2
    lf = l // 2
    return pl.pallas_call(
        _ffn_kernel,
        grid=(N // tm, FF // tf),
        in_specs=[
            pl.BlockSpec((tm, D), lambda i, j: (i, 0)),
            pl.BlockSpec((tm, D), lambda i, j: (i, 0)),
            pl.BlockSpec((None, D, tf), lambda i, j: (lf, 0, j)),
            pl.BlockSpec((None, D, tf), lambda i, j: (lf, 0, j)),
            pl.BlockSpec((None, tf, D), lambda i, j: (lf, j, 0)),
            _layer_block(p["post_ffn_g"], l),
            _mod_block(mod, l, 5, lambda i, j: i // tiles_per_batch),
        ],
        out_specs=pl.BlockSpec((tm, D), lambda i, j: (i, 0)),
        out_shape=jax.ShapeDtypeStruct((N, D), F32),
        scratch_shapes=[pltpu.VMEM((tm, D), F32)],
        compiler_params=pltpu.CompilerParams(
            dimension_semantics=("arbitrary", "arbitrary"), vmem_limit_bytes=VMEM_LIMIT),
        name="ffn_dense",
    )(h, x, wg, wu, wd, p["post_ffn_g"], mod)


def _moe_dispatch_kernel(pad_lo_ref, pad_hi_ref, slot_ref, h_ref, hs_hbm, zero_ref, sem):
    tm = h_ref.shape[0] // SUBLANE

    @pl.when(pl.program_id(0) == 0)
    def _():
        zero_ref[...] = jnp.zeros(zero_ref.shape, zero_ref.dtype)
        for e in range(pad_lo_ref.shape[0]):
            lo, hi = pad_lo_ref[e], pad_hi_ref[e]

            def start_pad(s, c):
                _token_copy(zero_ref, hs_hbm, 0, s, sem).start()
                return c

            def wait_pad(s, c):
                _token_copy(zero_ref, hs_hbm, 0, s, sem).wait()
                return c

            lax.fori_loop(lo, hi, start_pad, 0)
            lax.fori_loop(lo, hi, wait_pad, 0)

    def start(r, c):
        for k in range(2):
            _token_copy(h_ref, hs_hbm, r, slot_ref[0, 0, 2 * r + k], sem).start(priority=k)
        return c

    lax.fori_loop(0, tm, start, 0, unroll=4)

    def wait(r, c):
        for k in range(2):
            _token_copy(h_ref, hs_hbm, r, 0, sem).wait()
        return c

    lax.fori_loop(0, tm, wait, 0, unroll=4)


def _moe_dispatch(pad_lo, pad_hi, slots, h_tiles, n_slots):
    N = h_tiles.shape[0] // SUBLANE
    tm = MOE_DMA_BLOCK
    return pl.pallas_call(
        _moe_dispatch_kernel,
        grid_spec=pltpu.PrefetchScalarGridSpec(
            num_scalar_prefetch=2,
            grid=(N // tm,),
            in_specs=[
                pl.BlockSpec((1, 1, 2 * tm), lambda i, lo, hi: (i, 0, 0), memory_space=pltpu.SMEM),
                pl.BlockSpec((tm * SUBLANE, LANE), lambda i, lo, hi: (i, 0)),
            ],
            out_specs=pl.BlockSpec(memory_space=pl.ANY),
            scratch_shapes=[pltpu.VMEM((SUBLANE, LANE), F32), pltpu.SemaphoreType.DMA(())],
        ),
        out_shape=jax.ShapeDtypeStruct((n_slots * SUBLANE, LANE), F32),
        compiler_params=pltpu.CompilerParams(dimension_semantics=("arbitrary",)),
        name="moe_dispatch",
    )(pad_lo, pad_hi, slots.reshape(N // tm, 1, 2 * tm), h_tiles)


def _moe_expert_kernel(texp_ref, nused_ref, hs_ref, wg_ref, wu_ref, wd_ref, ys_ref):
    del texp_ref

    @pl.when(pl.program_id(0) < nused_ref[0])
    def _():
        x = _load_token_tiles(hs_ref).astype(BF16)
        cast = lambda w_ref: w_ref[...].astype(BF16)
        _store_token_tiles(ys_ref, _swiglu(x, cast(wg_ref), cast(wu_ref), cast(wd_ref)))

    @pl.when(pl.program_id(0) >= nused_ref[0])
    def _():
        ys_ref[...] = jnp.zeros(ys_ref.shape, F32)


def _moe_experts(tile_expert, n_used, hs, wg, wu, wd, lm):
    n_slots = hs.shape[0] // SUBLANE
    _, E, D, FF = wg.shape
    t = MOE_TILE
    tile = pl.BlockSpec((t * SUBLANE, LANE), lambda i, te, nu: (i, 0))
    return pl.pallas_call(
        _moe_expert_kernel,
        grid_spec=pltpu.PrefetchScalarGridSpec(
            num_scalar_prefetch=2,
            grid=(n_slots // t,),
            in_specs=[
                tile,
                pl.BlockSpec((None, None, D, FF), lambda i, te, nu: (lm, te[i], 0, 0)),
                pl.BlockSpec((None, None, D, FF), lambda i, te, nu: (lm, te[i], 0, 0)),
                pl.BlockSpec((None, None, FF, D), lambda i, te, nu: (lm, te[i], 0, 0)),
            ],
            out_specs=tile,
        ),
        out_shape=jax.ShapeDtypeStruct(hs.shape, F32),
        compiler_params=pltpu.CompilerParams(dimension_semantics=("arbitrary",), vmem_limit_bytes=VMEM_LIMIT),
        name="moe_experts",
    )(tile_expert, n_used, hs, wg, wu, wd)


def _moe_combine_kernel(slot_ref, next_slot_ref, ys_hbm, route_ref, x_ref, postg_ref, gf_ref, o_ref, buf_ref, sem):
    tm = x_ref.shape[0]
    i = pl.program_id(0)
    cur = lax.rem(i, 2)

    def gather(slots, half):
        def start(r, c):
            for k in range(2):
                _token_copy(ys_hbm, buf_ref.at[half, k], slots[0, 0, 2 * r + k], r, sem.at[half]).start(priority=k)
            return c

        lax.fori_loop(0, tm, start, 0, unroll=4)

    @pl.when(i == 0)
    def _():
        gather(slot_ref, 0)

    @pl.when(i + 1 < pl.num_programs(0))
    def _():
        gather(next_slot_ref, 1 - cur)

    def wait(r, c):
        for k in range(2):
            _token_copy(ys_hbm, buf_ref.at[cur, k], 0, r, sem.at[cur]).wait()
        return c

    lax.fori_loop(0, tm, wait, 0, unroll=4)
    route = route_ref[...]
    lane = lax.broadcasted_iota(jnp.int32, route.shape, 1)
    f = None
    for k in range(2):
        gate = jnp.sum(jnp.where(lane == 2 + k, route, 0.0), axis=-1, keepdims=True)
        y = _load_token_tiles(buf_ref.at[cur, k])
        f = gate * y if f is None else f + gate * y
    o_ref[...] = x_ref[...] + _rms(f, gf_ref[0] * postg_ref[...])


def _moe_combine(slots, ys, route, x, mod, l, p, tiles_per_batch):
    N, D = x.shape
    tm = TM_FFN
    n = N // tm
    slots = slots.reshape(n, 1, 2 * tm)
    return pl.pallas_call(
        _moe_combine_kernel,
        grid=(n,),
        in_specs=[
            pl.BlockSpec((1, 1, 2 * tm), lambda i: (i, 0, 0), memory_space=pltpu.SMEM),
            pl.BlockSpec((1, 1, 2 * tm), lambda i: (jnp.minimum(i + 1, n - 1), 0, 0), memory_space=pltpu.SMEM),
            pl.BlockSpec(memory_space=pl.ANY),
            pl.BlockSpec((tm, LANE), lambda i: (i, 0)),
            pl.BlockSpec((tm, D), lambda i: (i, 0)),
            _layer_block(p["post_ffn_g"], l),
            _mod_block(mod, l, 5, lambda i: i // tiles_per_batch),
        ],
        out_specs=pl.BlockSpec((tm, D), lambda i: (i, 0)),
        out_shape=jax.ShapeDtypeStruct((N, D), F32),
        scratch_shapes=[pltpu.VMEM((2, 2, tm * SUBLANE, LANE), F32), pltpu.SemaphoreType.DMA((2,))],
        compiler_params=pltpu.CompilerParams(dimension_semantics=("arbitrary",), vmem_limit_bytes=VMEM_LIMIT),
        name="moe_combine",
    )(slots, slots, ys, route, x, p["post_ffn_g"], mod)


def _moe_plan(route, n_slots):
    t = MOE_TILE
    pair_expert = route[:, :2].astype(jnp.int32).reshape(-1)
    onehot = (pair_expert[:, None] == jnp.arange(N_EXPERTS, dtype=jnp.int32)[None, :]).astype(jnp.int32)
    csum = jnp.cumsum(onehot, axis=0)
    count = csum[-1]
    padded = (count + t - 1) // t * t
    ends = jnp.cumsum(padded)
    starts = ends - padded
    slot = jnp.sum(onehot * (csum - 1 + starts[None, :]), axis=1)
    n_used = ends[-1:] // t
    tile_start = jnp.arange(n_slots // t, dtype=jnp.int32) * t
    tile_expert = jnp.sum((tile_start[:, None] >= ends[None, :]).astype(jnp.int32), axis=1)
    last_used = tile_expert[jnp.maximum(n_used[0] - 1, 0)]
    tile_expert = jnp.where(tile_start < ends[-1], tile_expert, last_used)
    pad_lo = jnp.concatenate([starts + count, ends[-1:]]).astype(jnp.int32)
    pad_hi = jnp.concatenate([ends, jnp.full((1,), n_slots, ends.dtype)]).astype(jnp.int32)
    return slot, pad_lo, pad_hi, tile_expert, n_used.astype(jnp.int32)


def _moe(h_tiles, x, route, mod, l, p, tiles_per_batch):
    N = x.shape[0]
    n_slots = 2 * N + N_EXPERTS * MOE_TILE
    slot, pad_lo, pad_hi, tile_expert, n_used = _moe_plan(route, n_slots)
    hs = _moe_dispatch(pad_lo, pad_hi, slot, h_tiles, n_slots)
    ys = _moe_experts(tile_expert, n_used, hs, p["moe_w_gate"], p["moe_w_up"], p["moe_w_down"], l // 2)
    return _moe_combine(slot, ys, route, x, mod, l, p, tiles_per_batch)


def _rope_partner(w):
    half = QK_ROPE // 2
    return jnp.concatenate([-w[..., half:], w[..., :half]], axis=-1)


def _rope_tables(S):
    pos = jnp.arange(S, dtype=F32)
    inv = ROPE_THETA ** (-jnp.arange(0, QK_ROPE, 2, dtype=F32) / QK_ROPE)
    ang = pos[:, None] * inv[None, :]
    cos, sin = jnp.cos(ang), jnp.sin(ang)
    cos2 = jnp.concatenate([cos, cos], axis=-1)
    sin2 = jnp.concatenate([sin, sin], axis=-1)
    tk = jnp.concatenate([cos2, sin2, jnp.zeros((S, LANE - 2 * QK_ROPE), F32)], axis=-1)
    return {"cosq": cos2.T * Q_SCALE, "sinq": sin2.T * Q_SCALE, "tk": tk}


def _prepare_params(w_in, q_norm_g, w_uq, kv_norm_g, w_ukv, sgu_ln_g, sgu_ln_b, sgu_w, sgu_b, conv_w, conv_b,
                    lru_wa, lru_ba, lru_wx, lru_bx, lru_lambda, pre_mix_g, post_mix_g, w_o, pre_ffn_g, post_ffn_g,
                    ffn_w_gate, ffn_w_up, ffn_w_down, moe_router, moe_w_gate, moe_w_up, moe_w_down):
    L, D, _ = w_in.shape
    o_q, o_kv = 0, Q_LORA
    o_kr = o_kv + KV_LORA
    o_u = o_kr + QK_ROPE
    o_v = o_u + SGU_WIDTH
    o_x = o_v + SGU_WIDTH
    o_y = o_x + LRU_WIDTH
    w_kr = w_in[..., o_kr:o_kr + QK_ROPE]
    w_in_p = jnp.concatenate([
        w_in[..., o_q:o_q + Q_LORA], w_in[..., o_kv:o_kv + KV_LORA], w_in[..., o_u:o_u + SGU_WIDTH],
        w_in[..., o_v:o_v + SGU_WIDTH], w_in[..., o_x:o_x + LRU_WIDTH], w_in[..., o_y:o_y + LRU_WIDTH],
        w_kr, _rope_partner(w_kr), jnp.zeros((L, D, LANE - 2 * QK_ROPE), F32)], axis=-1).astype(BF16)

    H = MLA_HEADS
    pad = HEAD_PAD - QK_NOPE - QK_ROPE
    wq = w_uq.reshape(L, Q_LORA, H, QK_NOPE + QK_ROPE)
    assert pad == QK_ROPE
    w_q = jnp.concatenate([wq, _rope_partner(wq[..., QK_NOPE:])], axis=-1).reshape(L, Q_LORA, H * HEAD_PAD)
    w_q = jnp.swapaxes(w_q, 1, 2).astype(BF16)

    wkv = w_ukv.reshape(L, KV_LORA, H, QK_NOPE + V_DIM)
    wk_lat = jnp.concatenate([wkv[..., :QK_NOPE], jnp.zeros((L, KV_LORA, H, HEAD_PAD - QK_NOPE), F32)],
                             axis=-1).reshape(L, KV_LORA, H * HEAD_PAD)
    eye = jnp.eye(QK_ROPE, dtype=F32)
    place = jnp.concatenate([jnp.zeros((QK_ROPE, QK_NOPE), F32), eye, jnp.zeros((QK_ROPE, pad), F32)], axis=-1)
    place = jnp.tile(place, (2, H))
    tail = jnp.concatenate([place, jnp.zeros((LANE - 2 * QK_ROPE, H * HEAD_PAD), F32)], axis=0)
    w_k = jnp.concatenate([wk_lat, jnp.broadcast_to(tail, (L,) + tail.shape)], axis=1).astype(BF16)
    w_v = jnp.swapaxes(
        jnp.concatenate([wkv[..., QK_NOPE:], jnp.zeros((L, KV_LORA, H, V_ROWS - V_DIM), F32)],
                        axis=-1).reshape(L, KV_LORA, H * V_ROWS), 1, 2).astype(BF16)

    def block_diag(w):
        sel = jnp.eye(LRU_HEADS, dtype=F32)
        return jnp.einsum("lhij,hg->lhigj", w, sel).reshape(L, LRU_WIDTH, LRU_WIDTH).astype(BF16)

    row = lambda a: a.reshape(a.shape[0], 1, -1)
    r = jnp.pad(moe_router, ((0, 0), (0, 0), (0, LANE - N_EXPERTS)))
    r_hi = r.astype(BF16)
    return {
        "pre_mix_g": row(pre_mix_g), "w_in": w_in_p, "qg": row(q_norm_g), "w_q": w_q, "kvg": row(kv_norm_g),
        "w_k": w_k, "w_v": w_v, "ln_g": row(sgu_ln_g), "ln_b": row(sgu_ln_b),
        "w_s": sgu_w.reshape(L, SGU_GROUPS * SGU_BLOCK, SGU_BLOCK),
        "b_s": jnp.repeat(jnp.swapaxes(sgu_b, 1, 2), SGU_GROUP_DIM, axis=2),
        "conv_w": conv_w, "conv_b": row(conv_b), "wa": block_diag(lru_wa), "ba": row(lru_ba),
        "wx": block_diag(lru_wx), "bx": row(lru_bx), "lam": row(lru_lambda),
        "w_o": w_o.astype(BF16), "post_mix_g": row(post_mix_g), "pre_ffn_g": row(pre_ffn_g),
        "post_ffn_g": row(post_ffn_g),
        "ffn_w_gate": ffn_w_gate.astype(BF16), "ffn_w_up": ffn_w_up.astype(BF16),
        "ffn_w_down": ffn_w_down.astype(BF16),
        "router": jnp.concatenate([r_hi, (r - r_hi.astype(F32)).astype(BF16)], axis=-1),
        "moe_w_gate": moe_w_gate, "moe_w_up": moe_w_up, "moe_w_down": moe_w_down,
    }


def kernel(x, c, w_mod, b_mod, pre_mix_g, post_mix_g, w_in, q_norm_g, w_uq, kv_norm_g, w_ukv, sgu_ln_g,
           sgu_ln_b, sgu_w, sgu_b, conv_w, conv_b, lru_wa, lru_ba, lru_wx, lru_bx, lru_lambda, w_o,
           pre_ffn_g, post_ffn_g, ffn_w_gate, ffn_w_up, ffn_w_down, moe_router, moe_w_gate, moe_w_up,
           moe_w_down):
    B, S, D = x.shape
    L = w_mod.shape[0]
    N = B * S
    tabs = _rope_tables(S)
    p = _prepare_params(w_in, q_norm_g, w_uq, kv_norm_g, w_ukv, sgu_ln_g, sgu_ln_b, sgu_w, sgu_b, conv_w, conv_b,
                        lru_wa, lru_ba, lru_wx, lru_bx, lru_lambda, pre_mix_g, post_mix_g, w_o, pre_ffn_g,
                        post_ffn_g, ffn_w_gate, ffn_w_up, ffn_w_down, moe_router, moe_w_gate, moe_w_up, moe_w_down)
    mod = _modulation(c, w_mod, b_mod).reshape(L, B * 6, 1, D)
    xf = x.reshape(N, D)
    for l in range(L):
        q, k, v, ymix = _mix_in(xf.reshape(B, S, D), mod, l, p, tabs)
        o = _attention(q, k, v)
        moe = l % 2 == 1
        res = _mix_out(o.reshape(N, -1), ymix.reshape(N, -1), xf, mod, l, p, p["router"] if moe else None,
                       S // TM_OUT)
        if moe:
            xf, h_tiles, route = res
            xf = _moe(h_tiles, xf, route, mod, l, p, S // TM_FFN)
        else:
            xf, h2 = res
            xf = _ffn(h2, xf, mod, l, p, S // TM_FFN)
    return xf.reshape(B, S, D)
```

```python
import jax
import jax.numpy as jnp
from jax import lax
from jax.experimental import pallas as pl
from jax.experimental.pallas import tpu as pltpu

F32 = jnp.float32
BF16 = jnp.bfloat16

MLA_HEADS = 8
QK_NOPE = 64
QK_ROPE = 32
V_DIM = 64
Q_LORA = 256
KV_LORA = 128
ROPE_THETA = 10000.0
CHUNK = 64
SGU_GROUPS = 4
SGU_GROUP_DIM = 64
SGU_WIDTH = SGU_GROUPS * SGU_GROUP_DIM
SGU_BLOCK = 128
LRU_HEADS = 4
LRU_HEAD_DIM = 64
LRU_WIDTH = LRU_HEADS * LRU_HEAD_DIM
CONV_W = 4
LRU_C = 8.0
N_EXPERTS = 8
NEG_INF = -1e30
RMS_EPS = 1e-6
LN_EPS = 1e-5

LANE = 128
SUBLANE = 8
HEAD_PAD = LANE
V_ROWS = 80
Q_SCALE = (QK_NOPE + QK_ROPE) ** -0.5 * 1.4426950408889634

C_Q = 0
C_KV = C_Q + Q_LORA
C_U = C_KV + KV_LORA
C_V = C_U + SGU_WIDTH
C_X = C_V + SGU_WIDTH
C_Y = C_X + LRU_WIDTH
C_KR = C_Y + LRU_WIDTH
D_IN_PAD = C_KR + LANE

VMEM_LIMIT = 56 * 1024 * 1024

TM_MIX = 512
MIX_SPLIT = 2
TQ = 512
TK = 512
ATTN_HEADS_PER_STEP = 8
AHEAD = 3
TM_OUT = 512
TM_FFN = 512
MOE_TILE = 512
MOE_DMA_BLOCK = 2048
MOE_ZERO_CHUNK = 64


def _rms(x, g):
    return x * lax.rsqrt(jnp.mean(x * x, axis=-1, keepdims=True) + RMS_EPS) * g


def _gelu(x):
    k = -2.0 * 0.7978845608028654 * 1.4426950408889634
    return x / (1.0 + jnp.exp2(x * (k + (k * 0.044715) * (x * x))))


def _sigmoid(x):
    return 1.0 / (1.0 + jnp.exp(-x))


def _silu(x):
    return x * _sigmoid(x)


def _bdot(a, b):
    return jnp.dot(a, b, preferred_element_type=F32)


def _layer_block(a, l):
    rest = a.shape[1:]
    return pl.BlockSpec((None,) + rest, lambda *_: (l,) + (0,) * len(rest))


def _mod_block(mod, l, j, batch_of):
    D = mod.shape[-1]
    return pl.BlockSpec((None, 1, 1, D), lambda *g: (l, 6 * batch_of(*g) + j, 0, 0))


def _split_bf16(a):
    hi = a.astype(BF16)
    return hi, (a - hi.astype(F32)).astype(BF16)


def _mod_kernel(c_ref, w_ref, b_ref, o_ref):
    c = c_ref[...]
    c_hi, c_lo = _split_bf16(_silu(c))
    w_hi, w_lo = _split_bf16(w_ref[0])
    o_ref[0] = _bdot(c_hi, w_hi) + (_bdot(c_hi, w_lo) + _bdot(c_lo, w_hi)) + b_ref[0]


def _modulation(c, w_mod, b_mod):
    L, D, D6 = w_mod.shape
    B = c.shape[0]
    return pl.pallas_call(
        _mod_kernel,
        grid=(L, D6 // D),
        in_specs=[
            pl.BlockSpec((B, D), lambda l, j: (0, 0)),
            pl.BlockSpec((1, D, D), lambda l, j: (l, 0, j)),
            pl.BlockSpec((1, 1, D), lambda l, j: (l, 0, j)),
        ],
        out_specs=pl.BlockSpec((1, B, D), lambda l, j: (l, 0, j)),
        out_shape=jax.ShapeDtypeStruct((L, B, D6), F32),
        compiler_params=pltpu.CompilerParams(vmem_limit_bytes=VMEM_LIMIT),
        name="modulation",
    )(c, w_mod, b_mod.reshape(L, 1, D6))


def _shift_rows(cur, prev8, k):
    rolled = pltpu.roll(cur, k, 0)
    fix = pltpu.roll(prev8, k, 0)
    row = lax.broadcasted_iota(jnp.int32, fix.shape, 0)
    top = jnp.where(row < k, fix, rolled[:SUBLANE])
    return jnp.concatenate([top, rolled[SUBLANE:]], axis=0)


def _mix_in_kernel(x_ref, sh_ref, sc_ref, preg_ref, win_ref, qg_ref, wq_ref, kvg_ref, wk_ref, wv_ref,
                   cosq_ref, sinq_ref, tk_ref, lng_ref, lnb_ref, ws_ref, bs_ref,
                   cw_ref, cb_ref, wa_ref, ba_ref, wx_ref, bx_ref, lam_ref,
                   q_ref, k_ref, v_ref, y_ref, prev_ref, hc_ref):
    tm = x_ref.shape[1]

    @pl.when(pl.program_id(1) == 0)
    def _():
        prev_ref[...] = jnp.zeros_like(prev_ref)
        hc_ref[...] = jnp.zeros_like(hc_ref)

    pre_scale = preg_ref[...] * (1.0 + sc_ref[0])
    ws = ws_ref[...]
    r_t = lax.broadcasted_iota(jnp.int32, ws.shape, 0) % SGU_BLOCK
    c_s = lax.broadcasted_iota(jnp.int32, ws.shape, 1)
    ws = jnp.where(c_s <= r_t, ws, 0.0).astype(BF16)
    lam = lam_ref[...]
    softplus_neg_lam = jnp.maximum(-lam, 0.0) + jnp.log(1.0 + jnp.exp(-jnp.abs(lam)))

    n = tm // MIX_SPLIT
    zs = []
    for part in range(MIX_SPLIT):
        x = x_ref[0, part * n:(part + 1) * n, :]
        h = _rms(x, pre_scale) + sh_ref[0]
        zs.append(_bdot(h.astype(BF16), win_ref[...]))
    for part in range(MIX_SPLIT):
        z = zs[part]
        r0 = part * n
        rows = slice(r0, r0 + n)

        nt = (((1,), (1,)), ((), ()))
        qn = _rms(z[:, C_Q:C_Q + Q_LORA], qg_ref[...]).astype(BF16)
        qq = lax.dot_general(wq_ref[...], qn, nt, preferred_element_type=F32)
        cosq = cosq_ref[:, rows]
        sinq = sinq_ref[:, rows]
        for hd in range(MLA_HEADS):
            lo = hd * HEAD_PAD
            ro, po = lo + QK_NOPE, lo + QK_NOPE + QK_ROPE
            q_ref[0, lo:ro, rows] = (qq[lo:ro] * Q_SCALE).astype(BF16)
            q_ref[0, ro:po, rows] = (qq[ro:po] * cosq + qq[po:po + QK_ROPE] * sinq).astype(BF16)
            q_ref[0, po:po + QK_ROPE, rows] = qq[po:po + QK_ROPE].astype(BF16)
        kvn = _rms(z[:, C_KV:C_KV + KV_LORA], kvg_ref[...]).astype(BF16)
        kr = (z[:, C_KR:C_KR + LANE] * tk_ref[rows, :]).astype(BF16)
        k_in = jnp.concatenate([kvn, kr], axis=-1)
        k_ref[0, rows, :] = _bdot(k_in, wk_ref[...]).astype(BF16)
        vv = lax.dot_general(wv_ref[...], kvn, nt, preferred_element_type=F32)
        ones_row = lax.broadcasted_iota(jnp.int32, vv.shape, 0) % V_ROWS == V_DIM
        vv = jnp.where(ones_row, 1.0, vv).astype(BF16)
        piece = min(n, TK)
        for c in range(n // piece):
            g0 = r0 + c * piece
            v_ref[0, g0 // TK, :, g0 % TK:g0 % TK + piece] = vv[:, c * piece:(c + 1) * piece]

        u = _gelu(z[:, C_U:C_U + SGU_WIDTH])
        gv = _gelu(z[:, C_V:C_V + SGU_WIDTH])
        mu = jnp.mean(gv, axis=-1, keepdims=True)
        var = jnp.mean(jnp.square(gv - mu), axis=-1, keepdims=True)
        vn = ((gv - mu) * lax.rsqrt(var + LN_EPS) * lng_ref[...] + lnb_ref[...]).astype(BF16)
        grp = lax.broadcasted_iota(jnp.int32, (SGU_BLOCK, SGU_WIDTH), 1) // SGU_GROUP_DIM
        for blk in range(n // SGU_BLOCK):
            b0 = blk * SGU_BLOCK
            res = _bdot(ws, vn[b0:b0 + SGU_BLOCK])
            mixed = bs_ref[...]
            for g in range(SGU_GROUPS):
                mixed = mixed + jnp.where(grp == g, res[g * SGU_BLOCK:(g + 1) * SGU_BLOCK], 0.0)
            y_ref[0, r0 + b0:r0 + b0 + SGU_BLOCK, 0:SGU_WIDTH] = (u[b0:b0 + SGU_BLOCK] * mixed).astype(BF16)

        zx = z[:, C_X:C_X + LRU_WIDTH]
        prev8 = prev_ref[...]
        cw = cw_ref[...]
        xc = cb_ref[...] + zx * cw[CONV_W - 1:CONV_W]
        for k in range(1, CONV_W):
            xc = xc + _shift_rows(zx, prev8, k) * cw[CONV_W - 1 - k:CONV_W - k]
        prev_ref[...] = zx[n - SUBLANE:]
        xcb = xc.astype(BF16)
        r = _sigmoid(_bdot(xcb, wa_ref[...]) + ba_ref[...])
        ig = _sigmoid(_bdot(xcb, wx_ref[...]) + bx_ref[...])
        log_a = -LRU_C * r * softplus_neg_lam
        a = jnp.exp(log_a)
        t = jnp.tanh(log_a)
        b = jnp.sqrt(-2.0 * t / (1.0 - t)) * (ig * xc)
        row = lax.broadcasted_iota(jnp.int32, a.shape, 0)
        k = 1
        while k < SUBLANE:
            valid = row >= k
            a_sh = pltpu.roll(a, k, 0)
            b_sh = pltpu.roll(b, k, 0)
            b = jnp.where(valid, b + a * b_sh, b)
            a = jnp.where(valid, a * a_sh, a)
            k *= 2
        while k < n:
            b = jnp.concatenate([b[:k], b[k:] + a[k:] * b[:-k]], axis=0)
            a = jnp.concatenate([a[:k], a[k:] * a[:-k]], axis=0)
            k *= 2
        hs = b + a * hc_ref[0:1]
        hc_ref[0:1] = hs[n - 1:n]
        y_ref[0, rows, SGU_WIDTH:SGU_WIDTH + LRU_WIDTH] = (hs * _gelu(z[:, C_Y:C_Y + LRU_WIDTH])).astype(BF16)


def _mix_in(x, mod, l, p, tabs):
    B, S, D = x.shape
    tm = TM_MIX
    batch = lambda b, s: b
    tab = pl.BlockSpec((tm, LANE), lambda b, s: (s, 0))
    tab_t = pl.BlockSpec((QK_ROPE, tm), lambda b, s: (0, s))
    assert tm % TK == 0
    consts = [p[k] for k in ("pre_mix_g", "w_in", "qg", "w_q", "kvg", "w_k", "w_v")]
    consts2 = [p[k] for k in ("ln_g", "ln_b", "w_s", "b_s", "conv_w", "conv_b", "wa", "ba", "wx", "bx", "lam")]
    qw = MLA_HEADS * HEAD_PAD
    return pl.pallas_call(
        _mix_in_kernel,
        grid=(B, S // tm),
        in_specs=[pl.BlockSpec((1, tm, D), lambda b, s: (b, s, 0)), _mod_block(mod, l, 0, batch),
                  _mod_block(mod, l, 1, batch)]
        + [_layer_block(a, l) for a in consts] + [tab_t, tab_t, tab] + [_layer_block(a, l) for a in consts2],
        out_specs=[
            pl.BlockSpec((1, qw, tm), lambda b, s: (b, 0, s)),
            pl.BlockSpec((1, tm, qw), lambda b, s: (b, s, 0)),
            pl.BlockSpec((1, tm // TK, MLA_HEADS * V_ROWS, TK), lambda b, s: (b, s, 0, 0)),
            pl.BlockSpec((1, tm, SGU_WIDTH + LRU_WIDTH), lambda b, s: (b, s, 0)),
        ],
        out_shape=[
            jax.ShapeDtypeStruct((B, qw, S), BF16),
            jax.ShapeDtypeStruct((B, S, qw), BF16),
            jax.ShapeDtypeStruct((B, S // TK, MLA_HEADS * V_ROWS, TK), BF16),
            jax.ShapeDtypeStruct((B, S, SGU_WIDTH + LRU_WIDTH), BF16),
        ],
        scratch_shapes=[pltpu.VMEM((SUBLANE, LRU_WIDTH), F32), pltpu.VMEM((SUBLANE, LRU_WIDTH), F32)],
        compiler_params=pltpu.CompilerParams(
            dimension_semantics=("arbitrary", "arbitrary"), vmem_limit_bytes=VMEM_LIMIT),
        name="mix_in",
    )(x, mod, mod, *consts, tabs["cosq"], tabs["sinq"], tabs["tk"], *consts2)


def _attn_kernel(q_ref, k_ref, v_ref, o_ref, m_ref, acc_ref):
    i = pl.program_id(2)
    heads = q_ref.shape[1] // HEAD_PAD
    feat = lambda hd: slice(hd * HEAD_PAD, (hd + 1) * HEAD_PAD)

    m_ref[...] = jnp.full(m_ref.shape, NEG_INF, F32)
    acc_ref[...] = jnp.zeros(acc_ref.shape, F32)

    def advance(hd, s, v, lanes):
        m = m_ref[hd, :, lanes]
        m_new = jnp.maximum(m, jnp.max(s, axis=0, keepdims=True))
        p = jnp.exp2(s - m_new).astype(BF16)
        acc_ref[hd, 0:V_ROWS, lanes] = jnp.exp2(m - m_new) * acc_ref[hd, 0:V_ROWS, lanes] + _bdot(v, p)
        m_ref[hd, :, lanes] = m_new

    def step(t, key_lo, n_keys, query_lo, masked):
        keys = pl.ds(pl.multiple_of(t * TK + key_lo, n_keys), n_keys)
        lanes = slice(query_lo, TQ)

        def scores(hd):
            s = _bdot(k_ref[0, keys, feat(hd)], q_ref[0, feat(hd), lanes])
            if masked:
                kc = (lax.broadcasted_iota(jnp.int32, s.shape, 0) + key_lo) // CHUNK
                qc = (lax.broadcasted_iota(jnp.int32, s.shape, 1) + query_lo) // CHUNK
                s = jnp.where(kc <= qc, s, NEG_INF)
            return s

        ss = {hd: scores(hd) for hd in range(min(AHEAD, heads))}
        for hd in range(heads):
            if hd + AHEAD < heads:
                ss[hd + AHEAD] = scores(hd + AHEAD)
            advance(hd, ss.pop(hd), v_ref[0, t, hd * V_ROWS:(hd + 1) * V_ROWS, key_lo:key_lo + n_keys], lanes)

    def body(t, c):
        step(t, 0, TK, 0, False)
        return c

    lax.fori_loop(0, i, body, 0)
    step(i, 0, TK // 2, 0, True)
    step(i, TK // 2, TK // 2, TQ // 2, True)
    outs = []
    for hd in range(heads):
        acc = acc_ref[hd]
        outs.append((acc / acc[V_DIM:V_DIM + 1]).T[:, :V_DIM])
    o_ref[0] = jnp.concatenate(outs, axis=-1).astype(BF16)


def _attention(q, k, v):
    B, S, _ = k.shape
    hp = ATTN_HEADS_PER_STEP
    assert TQ == TK
    return pl.pallas_call(
        _attn_kernel,
        grid=(B, MLA_HEADS // hp, S // TQ),
        in_specs=[
            pl.BlockSpec((1, hp * HEAD_PAD, TQ), lambda b, h, i: (b, h, i)),
            pl.BlockSpec((1, S, hp * HEAD_PAD), lambda b, h, i: (b, 0, h)),
            pl.BlockSpec((1, S // TK, hp * V_ROWS, TK), lambda b, h, i: (b, 0, h, 0)),
        ],
        out_specs=pl.BlockSpec((1, TQ, hp * V_DIM), lambda b, h, i: (b, i, h)),
        out_shape=jax.ShapeDtypeStruct((B, S, MLA_HEADS * V_DIM), BF16),
        scratch_shapes=[
            pltpu.VMEM((hp, 1, TQ), F32),
            pltpu.VMEM((hp, HEAD_PAD, TQ), F32),
        ],
        compiler_params=pltpu.CompilerParams(
            dimension_semantics=("arbitrary", "arbitrary", "arbitrary"), vmem_limit_bytes=VMEM_LIMIT),
        name="attention",
    )(q, k, v)


def _mix_out_kernel(o_ref, y_ref, x_ref, wo_ref, postg_ref, gm_ref, preg_ref, scf_ref, shf_ref,
                    xo_ref, h_ref):
    no = o_ref.shape[1]
    tm = x_ref.shape[0]
    parts = [slice(r, r + tm // 2) for r in (0, tm // 2)]
    ys = [_bdot(o_ref[rows, :], wo_ref[0:no]) + _bdot(y_ref[rows, :], wo_ref[no:]) for rows in parts]
    post_scale = gm_ref[0] * postg_ref[...]
    pre_scale = preg_ref[...] * (1.0 + scf_ref[0])
    for rows, y in zip(parts, ys):
        x = x_ref[rows, :] + _rms(y, post_scale)
        xo_ref[rows, :] = x
        h_ref[rows, :] = (_rms(x, pre_scale) + shf_ref[0]).astype(BF16)


def _store_token_tiles(ref, x):
    rows, d = x.shape
    assert d == SUBLANE * LANE
    for j in range(SUBLANE):
        ref[pl.ds(j, rows, stride=SUBLANE), :] = x[:, j * LANE:(j + 1) * LANE]


def _load_token_tiles(ref):
    rows = ref.shape[0] // SUBLANE
    return jnp.concatenate([ref[pl.ds(j, rows, stride=SUBLANE), :] for j in range(SUBLANE)], axis=-1)


def _token_copy(src, dst, src_token, dst_token, sem):
    def rows(t):
        start = t * SUBLANE
        return pl.ds(start if isinstance(t, int) else pl.multiple_of(start, SUBLANE), SUBLANE)

    return pltpu.make_async_copy(src.at[rows(src_token)], dst.at[rows(dst_token)], sem)


def _mix_out_moe_kernel(o_ref, y_ref, x_ref, wo_ref, postg_ref, gm_ref, preg_ref, scf_ref, shf_ref,
                        router_ref, xo_ref, h_ref, route_ref):
    no = o_ref.shape[1]
    tm = x_ref.shape[0]
    parts = [slice(r, r + tm // 2) for r in (0, tm // 2)]
    ys = [_bdot(o_ref[rows, :], wo_ref[0:no]) + _bdot(y_ref[rows, :], wo_ref[no:]) for rows in parts]
    post_scale = gm_ref[0] * postg_ref[...]
    pre_scale = preg_ref[...] * (1.0 + scf_ref[0])
    for rows, y in zip(parts, ys):
        x = x_ref[rows, :] + _rms(y, post_scale)
        xo_ref[rows, :] = x
        h = _rms(x, pre_scale) + shf_ref[0]
        _store_token_tiles(h_ref.at[pl.ds(rows.start * SUBLANE, (rows.stop - rows.start) * SUBLANE)], h)
        h_hi, h_lo = _split_bf16(h)
        t = _bdot(h_hi, router_ref[...])
        logits = t[:, :LANE] + t[:, LANE:] + _bdot(h_lo, router_ref[:, :LANE])
        lane = lax.broadcasted_iota(jnp.int32, logits.shape, 1)
        logits = jnp.where(lane < N_EXPERTS, logits, -jnp.inf)
        m1 = jnp.max(logits, axis=-1, keepdims=True)
        i1 = jnp.min(jnp.where(logits == m1, lane, LANE), axis=-1, keepdims=True)
        rest = jnp.where(lane == i1, -jnp.inf, logits)
        m2 = jnp.max(rest, axis=-1, keepdims=True)
        i2 = jnp.min(jnp.where(rest == m2, lane, LANE), axis=-1, keepdims=True)
        e = jnp.exp(m2 - m1)
        g1 = 1.0 / (1.0 + e)
        route_ref[rows, :] = (jnp.where(lane == 0, i1.astype(F32), 0.0) + jnp.where(lane == 1, i2.astype(F32), 0.0)
                              + jnp.where(lane == 2, g1, 0.0) + jnp.where(lane == 3, e * g1, 0.0))


def _mix_out(o, y, x, mod, l, p, router, tiles_per_batch):
    N, D = x.shape
    tm = TM_OUT
    row = lambda w: pl.BlockSpec((tm, w), lambda i: (i, 0))
    batch = lambda i: i // tiles_per_batch
    in_specs = [row(o.shape[1]), row(y.shape[1]), row(D), _layer_block(p["w_o"], l),
                _layer_block(p["post_mix_g"], l), _mod_block(mod, l, 2, batch), _layer_block(p["pre_ffn_g"], l),
                _mod_block(mod, l, 4, batch), _mod_block(mod, l, 3, batch)]
    out_specs = [row(D), row(D)]
    out_shape = [jax.ShapeDtypeStruct((N, D), F32), jax.ShapeDtypeStruct((N, D), BF16)]
    args = [o, y, x, p["w_o"], p["post_mix_g"], mod, p["pre_ffn_g"], mod, mod]
    body = _mix_out_kernel
    if router is not None:
        in_specs.append(_layer_block(router, l // 2))
        out_specs = [row(D), pl.BlockSpec((tm * SUBLANE, LANE), lambda i: (i, 0)), row(LANE)]
        out_shape = [jax.ShapeDtypeStruct((N, D), F32), jax.ShapeDtypeStruct((N * SUBLANE, LANE), F32),
                     jax.ShapeDtypeStruct((N, LANE), F32)]
        args.append(router)
        body = _mix_out_moe_kernel
    return pl.pallas_call(
        body,
        grid=(N // tm,),
        in_specs=in_specs,
        out_specs=out_specs,
        out_shape=out_shape,
        compiler_params=pltpu.CompilerParams(dimension_semantics=("arbitrary",), vmem_limit_bytes=VMEM_LIMIT),
        name="mix_out",
    )(*args)


def _swiglu(h, wg, wu, wd):
    g = _bdot(h, wg)
    u = _bdot(h, wu)
    return _bdot((_silu(g) * u).astype(BF16), wd)


def _ffn_kernel(h_ref, x_ref, wg_ref, wu_ref, wd_ref, postg_ref, gf_ref, o_ref, acc_ref):
    j = pl.program_id(1)

    @pl.when(j == 0)
    def _():
        acc_ref[...] = jnp.zeros_like(acc_ref)

    acc_ref[...] += _swiglu(h_ref[...], wg_ref[...], wu_ref[...], wd_ref[...])

    @pl.when(j == pl.num_programs(1) - 1)
    def _():
        o_ref[...] = x_ref[...] + _rms(acc_ref[...], gf_ref[0] * postg_ref[...])


def _ffn(h, x, mod, l, p, tiles_per_batch):
    N, D = x.shape
    wg, wu, wd = p["ffn_w_gate"], p["ffn_w_up"], p["ffn_w_down"]
    FF = wg.shape[2]
    tm = TM_FFN
    tf = FF // 2
    lf = l // 2
    return pl.pallas_call(
        _ffn_kernel,
        grid=(N // tm, FF // tf),
        in_specs=[
            pl.BlockSpec((tm, D), lambda i, j: (i, 0)),
            pl.BlockSpec((tm, D), lambda i, j: (i, 0)),
            pl.BlockSpec((None, D, tf), lambda i, j: (lf, 0, j)),
            pl.BlockSpec((None, D, tf), lambda i, j: (lf, 0, j)),
            pl.BlockSpec((None, tf, D), lambda i, j: (lf, j, 0)),
            _layer_block(p["post_ffn_g"], l),
            _mod_block(mod, l, 5, lambda i, j: i // tiles_per_batch),
        ],
        out_specs=pl.BlockSpec((tm, D), lambda i, j: (i, 0)),
        out_shape=jax.ShapeDtypeStruct((N, D), F32),
        scratch_shapes=[pltpu.VMEM((tm, D), F32)],
        compiler_params=pltpu.CompilerParams(
            dimension_semantics=("arbitrary", "arbitrary"), vmem_limit_bytes=VMEM_LIMIT),
        name="ffn_dense",
    )(h, x, wg, wu, wd, p["post_ffn_g"], mod)


def _moe_dispatch_kernel(pad_lo_ref, pad_hi_ref, slot_ref, h_ref, hs_hbm, zero_ref, sem):
    tm = h_ref.shape[0] // SUBLANE

    @pl.when(pl.program_id(0) == 0)
    def _():
        zero_ref[...] = jnp.zeros(zero_ref.shape, zero_ref.dtype)
        chunk = zero_ref.shape[0] // SUBLANE

        def big(s):
            rows = pl.ds(pl.multiple_of(s * SUBLANE, SUBLANE), chunk * SUBLANE)
            return pltpu.make_async_copy(zero_ref, hs_hbm.at[rows], sem)

        for e in range(pad_lo_ref.shape[0]):
            lo, hi = pad_lo_ref[e], pad_hi_ref[e]
            n_big = (hi - lo) // chunk
            mid = lo + n_big * chunk

            def start_big(j, c):
                big(lo + j * chunk).start()
                return c

            def wait_big(j, c):
                big(lo + j * chunk).wait()
                return c

            def start_pad(s, c):
                _token_copy(zero_ref, hs_hbm, 0, s, sem).start()
                return c

            def wait_pad(s, c):
                _token_copy(zero_ref, hs_hbm, 0, s, sem).wait()
                return c

            lax.fori_loop(0, n_big, start_big, 0)
            lax.fori_loop(mid, hi, start_pad, 0)
            lax.fori_loop(0, n_big, wait_big, 0)
            lax.fori_loop(mid, hi, wait_pad, 0)

    def start(r, c):
        for k in range(2):
            _token_copy(h_ref, hs_hbm, r, slot_ref[0, 0, 2 * r + k], sem).start(priority=k)
        return c

    lax.fori_loop(0, tm, start, 0, unroll=4)

    def wait(r, c):
        for k in range(2):
            _token_copy(h_ref, hs_hbm, r, 0, sem).wait()
        return c

    lax.fori_loop(0, tm, wait, 0, unroll=4)


def _moe_dispatch(pad_lo, pad_hi, slots, h_tiles, n_slots):
    N = h_tiles.shape[0] // SUBLANE
    tm = MOE_DMA_BLOCK
    return pl.pallas_call(
        _moe_dispatch_kernel,
        grid_spec=pltpu.PrefetchScalarGridSpec(
            num_scalar_prefetch=2,
            grid=(N // tm,),
            in_specs=[
                pl.BlockSpec((1, 1, 2 * tm), lambda i, lo, hi: (i, 0, 0), memory_space=pltpu.SMEM),
                pl.BlockSpec((tm * SUBLANE, LANE), lambda i, lo, hi: (i, 0)),
            ],
            out_specs=pl.BlockSpec(memory_space=pl.ANY),
            scratch_shapes=[pltpu.VMEM((MOE_ZERO_CHUNK * SUBLANE, LANE), F32), pltpu.SemaphoreType.DMA(())],
        ),
        out_shape=jax.ShapeDtypeStruct((n_slots * SUBLANE, LANE), F32),
        compiler_params=pltpu.CompilerParams(dimension_semantics=("arbitrary",)),
        name="moe_dispatch",
    )(pad_lo, pad_hi, slots.reshape(N // tm, 1, 2 * tm), h_tiles)


def _moe_expert_kernel(texp_ref, nused_ref, hs_ref, wg_ref, wu_ref, wd_ref, ys_ref):
    del texp_ref

    @pl.when(pl.program_id(0) < nused_ref[0])
    def _():
        x = _load_token_tiles(hs_ref).astype(BF16)
        cast = lambda w_ref: w_ref[...].astype(BF16)
        _store_token_tiles(ys_ref, _swiglu(x, cast(wg_ref), cast(wu_ref), cast(wd_ref)))

    @pl.when(pl.program_id(0) >= nused_ref[0])
    def _():
        ys_ref[...] = jnp.zeros(ys_ref.shape, F32)


def _moe_experts(tile_expert, n_used, hs, wg, wu, wd, lm):
    n_slots = hs.shape[0] // SUBLANE
    _, E, D, FF = wg.shape
    t = MOE_TILE
    tile = pl.BlockSpec((t * SUBLANE, LANE), lambda i, te, nu: (i, 0))
    return pl.pallas_call(
        _moe_expert_kernel,
        grid_spec=pltpu.PrefetchScalarGridSpec(
            num_scalar_prefetch=2,
            grid=(n_slots // t,),
            in_specs=[
                tile,
                pl.BlockSpec((None, None, D, FF), lambda i, te, nu: (lm, te[i], 0, 0)),
                pl.BlockSpec((None, None, D, FF), lambda i, te, nu: (lm, te[i], 0, 0)),
                pl.BlockSpec((None, None, FF, D), lambda i, te, nu: (lm, te[i], 0, 0)),
            ],
            out_specs=tile,
        ),
        out_shape=jax.ShapeDtypeStruct(hs.shape, F32),
        compiler_params=pltpu.CompilerParams(dimension_semantics=("arbitrary",), vmem_limit_bytes=VMEM_LIMIT),
        name="moe_experts",
    )(tile_expert, n_used, hs, wg, wu, wd)


def _moe_combine_kernel(slot_ref, next_slot_ref, ys_hbm, route_ref, x_ref, postg_ref, gf_ref, o_ref, buf_ref, sem):
    tm = x_ref.shape[0]
    i = pl.program_id(0)
    cur = lax.rem(i, 2)

    def gather(slots, half):
        def start(r, c):
            for k in range(2):
                _token_copy(ys_hbm, buf_ref.at[half, k], slots[0, 0, 2 * r + k], r, sem.at[half]).start(priority=k)
            return c

        lax.fori_loop(0, tm, start, 0, unroll=4)

    @pl.when(i == 0)
    def _():
        gather(slot_ref, 0)

    @pl.when(i + 1 < pl.num_programs(0))
    def _():
        gather(next_slot_ref, 1 - cur)

    def wait(r, c):
        for k in range(2):
            _token_copy(ys_hbm, buf_ref.at[cur, k], 0, r, sem.at[cur]).wait()
        return c

    lax.fori_loop(0, tm, wait, 0, unroll=4)
    route = route_ref[...]
    lane = lax.broadcasted_iota(jnp.int32, route.shape, 1)
    f = None
    for k in range(2):
        gate = jnp.sum(jnp.where(lane == 2 + k, route, 0.0), axis=-1, keepdims=True)
        y = _load_token_tiles(buf_ref.at[cur, k])
        f = gate * y if f is None else f + gate * y
    o_ref[...] = x_ref[...] + _rms(f, gf_ref[0] * postg_ref[...])


def _moe_combine(slots, ys, route, x, mod, l, p, tiles_per_batch):
    N, D = x.shape
    tm = TM_FFN
    n = N // tm
    slots = slots.reshape(n, 1, 2 * tm)
    return pl.pallas_call(
        _moe_combine_kernel,
        grid=(n,),
        in_specs=[
            pl.BlockSpec((1, 1, 2 * tm), lambda i: (i, 0, 0), memory_space=pltpu.SMEM),
            pl.BlockSpec((1, 1, 2 * tm), lambda i: (jnp.minimum(i + 1, n - 1), 0, 0), memory_space=pltpu.SMEM),
            pl.BlockSpec(memory_space=pl.ANY),
            pl.BlockSpec((tm, LANE), lambda i: (i, 0)),
            pl.BlockSpec((tm, D), lambda i: (i, 0)),
            _layer_block(p["post_ffn_g"], l),
            _mod_block(mod, l, 5, lambda i: i // tiles_per_batch),
        ],
        out_specs=pl.BlockSpec((tm, D), lambda i: (i, 0)),
        out_shape=jax.ShapeDtypeStruct((N, D), F32),
        scratch_shapes=[pltpu.VMEM((2, 2, tm * SUBLANE, LANE), F32), pltpu.SemaphoreType.DMA((2,))],
        compiler_params=pltpu.CompilerParams(dimension_semantics=("arbitrary",), vmem_limit_bytes=VMEM_LIMIT),
        name="moe_combine",
    )(slots, slots, ys, route, x, p["post_ffn_g"], mod)


def _moe_plan(route, n_slots):
    t = MOE_TILE
    pair_expert = route[:, :2].astype(jnp.int32).reshape(-1)
    onehot = (pair_expert[None, :] == jnp.arange(N_EXPERTS, dtype=jnp.int32)[:, None]).astype(jnp.int32)
    csum = jnp.cumsum(onehot, axis=1)
    count = csum[:, -1]
    padded = (count + t - 1) // t * t
    ends = jnp.cumsum(padded)
    starts = ends - padded
    slot = jnp.sum(onehot * (csum - 1 + starts[:, None]), axis=0)
    n_used = ends[-1:] // t
    tile_start = jnp.arange(n_slots // t, dtype=jnp.int32) * t
    tile_expert = jnp.sum((tile_start[:, None] >= ends[None, :]).astype(jnp.int32), axis=1)
    last_used = tile_expert[jnp.maximum(n_used[0] - 1, 0)]
    tile_expert = jnp.where(tile_start < ends[-1], tile_expert, last_used)
    pad_lo = jnp.concatenate([starts + count, ends[-1:]]).astype(jnp.int32)
    pad_hi = jnp.concatenate([ends, jnp.full((1,), n_slots, ends.dtype)]).astype(jnp.int32)
    return slot, pad_lo, pad_hi, tile_expert, n_used.astype(jnp.int32)


def _moe(h_tiles, x, route, mod, l, p, tiles_per_batch):
    N = x.shape[0]
    n_slots = 2 * N + N_EXPERTS * MOE_TILE
    slot, pad_lo, pad_hi, tile_expert, n_used = _moe_plan(route, n_slots)
    hs = _moe_dispatch(pad_lo, pad_hi, slot, h_tiles, n_slots)
    ys = _moe_experts(tile_expert, n_used, hs, p["moe_w_gate"], p["moe_w_up"], p["moe_w_down"], l // 2)
    return _moe_combine(slot, ys, route, x, mod, l, p, tiles_per_batch)


def _rope_partner(w):
    half = QK_ROPE // 2
    return jnp.concatenate([-w[..., half:], w[..., :half]], axis=-1)


def _rope_tables(S):
    pos = jnp.arange(S, dtype=F32)
    inv = ROPE_THETA ** (-jnp.arange(0, QK_ROPE, 2, dtype=F32) / QK_ROPE)
    ang = pos[:, None] * inv[None, :]
    cos, sin = jnp.cos(ang), jnp.sin(ang)
    cos2 = jnp.concatenate([cos, cos], axis=-1)
    sin2 = jnp.concatenate([sin, sin], axis=-1)
    tk = jnp.concatenate([cos2, sin2, jnp.zeros((S, LANE - 2 * QK_ROPE), F32)], axis=-1)
    return {"cosq": cos2.T * Q_SCALE, "sinq": sin2.T * Q_SCALE, "tk": tk}


def _prepare_params(w_in, q_norm_g, w_uq, kv_norm_g, w_ukv, sgu_ln_g, sgu_ln_b, sgu_w, sgu_b, conv_w, conv_b,
                    lru_wa, lru_ba, lru_wx, lru_bx, lru_lambda, pre_mix_g, post_mix_g, w_o, pre_ffn_g, post_ffn_g,
                    ffn_w_gate, ffn_w_up, ffn_w_down, moe_router, moe_w_gate, moe_w_up, moe_w_down):
    L, D, _ = w_in.shape
    o_q, o_kv = 0, Q_LORA
    o_kr = o_kv + KV_LORA
    o_u = o_kr + QK_ROPE
    o_v = o_u + SGU_WIDTH
    o_x = o_v + SGU_WIDTH
    o_y = o_x + LRU_WIDTH
    w_kr = w_in[..., o_kr:o_kr + QK_ROPE]
    w_in_p = jnp.concatenate([
        w_in[..., o_q:o_q + Q_LORA], w_in[..., o_kv:o_kv + KV_LORA], w_in[..., o_u:o_u + SGU_WIDTH],
        w_in[..., o_v:o_v + SGU_WIDTH], w_in[..., o_x:o_x + LRU_WIDTH], w_in[..., o_y:o_y + LRU_WIDTH],
        w_kr, _rope_partner(w_kr), jnp.zeros((L, D, LANE - 2 * QK_ROPE), F32)], axis=-1).astype(BF16)

    H = MLA_HEADS
    pad = HEAD_PAD - QK_NOPE - QK_ROPE
    wq = w_uq.reshape(L, Q_LORA, H, QK_NOPE + QK_ROPE)
    assert pad == QK_ROPE
    w_q = jnp.concatenate([wq, _rope_partner(wq[..., QK_NOPE:])], axis=-1).reshape(L, Q_LORA, H * HEAD_PAD)
    w_q = jnp.swapaxes(w_q, 1, 2).astype(BF16)

    wkv = w_ukv.reshape(L, KV_LORA, H, QK_NOPE + V_DIM)
    wk_lat = jnp.concatenate([wkv[..., :QK_NOPE], jnp.zeros((L, KV_LORA, H, HEAD_PAD - QK_NOPE), F32)],
                             axis=-1).reshape(L, KV_LORA, H * HEAD_PAD)
    eye = jnp.eye(QK_ROPE, dtype=F32)
    place = jnp.concatenate([jnp.zeros((QK_ROPE, QK_NOPE), F32), eye, jnp.zeros((QK_ROPE, pad), F32)], axis=-1)
    place = jnp.tile(place, (2, H))
    tail = jnp.concatenate([place, jnp.zeros((LANE - 2 * QK_ROPE, H * HEAD_PAD), F32)], axis=0)
    w_k = jnp.concatenate([wk_lat, jnp.broadcast_to(tail, (L,) + tail.shape)], axis=1).astype(BF16)
    w_v = jnp.swapaxes(
        jnp.concatenate([wkv[..., QK_NOPE:], jnp.zeros((L, KV_LORA, H, V_ROWS - V_DIM), F32)],
                        axis=-1).reshape(L, KV_LORA, H * V_ROWS), 1, 2).astype(BF16)

    def block_diag(w):
        sel = jnp.eye(LRU_HEADS, dtype=F32)
        return jnp.einsum("lhij,hg->lhigj", w, sel).reshape(L, LRU_WIDTH, LRU_WIDTH).astype(BF16)

    row = lambda a: a.reshape(a.shape[0], 1, -1)
    r = jnp.pad(moe_router, ((0, 0), (0, 0), (0, LANE - N_EXPERTS)))
    r_hi = r.astype(BF16)
    return {
        "pre_mix_g": row(pre_mix_g), "w_in": w_in_p, "qg": row(q_norm_g), "w_q": w_q, "kvg": row(kv_norm_g),
        "w_k": w_k, "w_v": w_v, "ln_g": row(sgu_ln_g), "ln_b": row(sgu_ln_b),
        "w_s": sgu_w.reshape(L, SGU_GROUPS * SGU_BLOCK, SGU_BLOCK),
        "b_s": jnp.repeat(jnp.swapaxes(sgu_b, 1, 2), SGU_GROUP_DIM, axis=2),
        "conv_w": conv_w, "conv_b": row(conv_b), "wa": block_diag(lru_wa), "ba": row(lru_ba),
        "wx": block_diag(lru_wx), "bx": row(lru_bx), "lam": row(lru_lambda),
        "w_o": w_o.astype(BF16), "post_mix_g": row(post_mix_g), "pre_ffn_g": row(pre_ffn_g),
        "post_ffn_g": row(post_ffn_g),
        "ffn_w_gate": ffn_w_gate.astype(BF16), "ffn_w_up": ffn_w_up.astype(BF16),
        "ffn_w_down": ffn_w_down.astype(BF16),
        "router": jnp.concatenate([r_hi, (r - r_hi.astype(F32)).astype(BF16)], axis=-1),
        "moe_w_gate": moe_w_gate, "moe_w_up": moe_w_up, "moe_w_down": moe_w_down,
    }


def kernel(x, c, w_mod, b_mod, pre_mix_g, post_mix_g, w_in, q_norm_g, w_uq, kv_norm_g, w_ukv, sgu_ln_g,
           sgu_ln_b, sgu_w, sgu_b, conv_w, conv_b, lru_wa, lru_ba, lru_wx, lru_bx, lru_lambda, w_o,
           pre_ffn_g, post_ffn_g, ffn_w_gate, ffn_w_up, ffn_w_down, moe_router, moe_w_gate, moe_w_up,
           moe_w_down):
    B, S, D = x.shape
    L = w_mod.shape[0]
    N = B * S
    tabs = _rope_tables(S)
    p = _prepare_params(w_in, q_norm_g, w_uq, kv_norm_g, w_ukv, sgu_ln_g, sgu_ln_b, sgu_w, sgu_b, conv_w, conv_b,
                        lru_wa, lru_ba, lru_wx, lru_bx, lru_lambda, pre_mix_g, post_mix_g, w_o, pre_ffn_g,
                        post_ffn_g, ffn_w_gate, ffn_w_up, ffn_w_down, moe_router, moe_w_gate, moe_w_up, moe_w_down)
    mod = _modulation(c, w_mod, b_mod).reshape(L, B * 6, 1, D)
    xf = x.reshape(N, D)
    for l in range(L):
        q, k, v, ymix = _mix_in(xf.reshape(B, S, D), mod, l, p, tabs)
        o = _attention(q, k, v)
        moe = l % 2 == 1
        res = _mix_out(o.reshape(N, -1), ymix.reshape(N, -1), xf, mod, l, p, p["router"] if moe else None,
                       S // TM_OUT)
        if moe:
            xf, h_tiles, route = res
            xf = _moe(h_tiles, xf, route, mod, l, p, S // TM_FFN)
        else:
            xf, h2 = res
            xf = _ffn(h2, xf, mod, l, p, S // TM_FFN)
    return xf.reshape(B, S, D)
```

```python
import jax
import jax.numpy as jnp
from jax import lax
from jax.experimental import pallas as pl
from jax.experimental.pallas import tpu as pltpu

F32 = jnp.float32
BF16 = jnp.bfloat16

MLA_HEADS = 8
QK_NOPE = 64
QK_ROPE = 32
V_DIM = 64
Q_LORA = 256
KV_LORA = 128
ROPE_THETA = 10000.0
CHUNK = 64
SGU_GROUPS = 4
SGU_GROUP_DIM = 64
SGU_WIDTH = SGU_GROUPS * SGU_GROUP_DIM
SGU_BLOCK = 128
LRU_HEADS = 4
LRU_HEAD_DIM = 64
LRU_WIDTH = LRU_HEADS * LRU_HEAD_DIM
CONV_W = 4
LRU_C = 8.0
N_EXPERTS = 8
NEG_INF = -1e30
RMS_EPS = 1e-6
LN_EPS = 1e-5

LANE = 128
SUBLANE = 8
HEAD_PAD = LANE
V_ROWS = 80
Q_SCALE = (QK_NOPE + QK_ROPE) ** -0.5 * 1.4426950408889634

C_Q = 0
C_KV = C_Q + Q_LORA
C_U = C_KV + KV_LORA
C_V = C_U + SGU_WIDTH
C_X = C_V + SGU_WIDTH
C_Y = C_X + LRU_WIDTH
C_KR = C_Y + LRU_WIDTH
D_IN_PAD = C_KR + LANE

VMEM_LIMIT = 56 * 1024 * 1024

TM_MIX = 1024
MIX_SPLIT = 4
TQ = 512
TK = 512
ATTN_HEADS_PER_STEP = 8
AHEAD = 3
TM_OUT = 1024
OUT_SPLIT = 4
TM_FFN = 512
MOE_TILE = 512
MOE_DMA_BLOCK = 2048
MOE_ZERO_CHUNK = 64


def _rms(x, g):
    return x * lax.rsqrt(jnp.mean(x * x, axis=-1, keepdims=True) + RMS_EPS) * g


def _gelu(x):
    k = -2.0 * 0.7978845608028654 * 1.4426950408889634
    return x / (1.0 + jnp.exp2(x * (k + (k * 0.044715) * (x * x))))


def _sigmoid(x):
    return 1.0 / (1.0 + jnp.exp(-x))


def _silu(x):
    return x * _sigmoid(x)


def _bdot(a, b):
    return jnp.dot(a, b, preferred_element_type=F32)


def _layer_block(a, l):
    rest = a.shape[1:]
    return pl.BlockSpec((None,) + rest, lambda *_: (l,) + (0,) * len(rest))


def _mod_block(mod, l, j, batch_of):
    D = mod.shape[-1]
    return pl.BlockSpec((None, 1, 1, D), lambda *g: (l, 6 * batch_of(*g) + j, 0, 0))


def _split_bf16(a):
    hi = a.astype(BF16)
    return hi, (a - hi.astype(F32)).astype(BF16)


def _mod_kernel(c_ref, w_ref, b_ref, o_ref):
    c = c_ref[...]
    c_hi, c_lo = _split_bf16(_silu(c))
    w_hi, w_lo = _split_bf16(w_ref[0])
    o_ref[0] = _bdot(c_hi, w_hi) + (_bdot(c_hi, w_lo) + _bdot(c_lo, w_hi)) + b_ref[0]


def _modulation(c, w_mod, b_mod):
    L, D, D6 = w_mod.shape
    B = c.shape[0]
    return pl.pallas_call(
        _mod_kernel,
        grid=(L, D6 // D),
        in_specs=[
            pl.BlockSpec((B, D), lambda l, j: (0, 0)),
            pl.BlockSpec((1, D, D), lambda l, j: (l, 0, j)),
            pl.BlockSpec((1, 1, D), lambda l, j: (l, 0, j)),
        ],
        out_specs=pl.BlockSpec((1, B, D), lambda l, j: (l, 0, j)),
        out_shape=jax.ShapeDtypeStruct((L, B, D6), F32),
        compiler_params=pltpu.CompilerParams(vmem_limit_bytes=VMEM_LIMIT),
        name="modulation",
    )(c, w_mod, b_mod.reshape(L, 1, D6))


def _shift_rows(cur, prev8, k):
    rolled = pltpu.roll(cur, k, 0)
    fix = pltpu.roll(prev8, k, 0)
    row = lax.broadcasted_iota(jnp.int32, fix.shape, 0)
    top = jnp.where(row < k, fix, rolled[:SUBLANE])
    return jnp.concatenate([top, rolled[SUBLANE:]], axis=0)


def _mix_in_kernel(x_ref, sh_ref, sc_ref, preg_ref, win_ref, qg_ref, wq_ref, kvg_ref, wk_ref, wv_ref,
                   cosq_ref, sinq_ref, tk_ref, lng_ref, lnb_ref, ws_ref, bs_ref,
                   cw_ref, cb_ref, wa_ref, ba_ref, wx_ref, bx_ref, lam_ref,
                   q_ref, k_ref, v_ref, y_ref, prev_ref, hc_ref):
    tm = x_ref.shape[1]

    @pl.when(pl.program_id(1) == 0)
    def _():
        prev_ref[...] = jnp.zeros_like(prev_ref)
        hc_ref[...] = jnp.zeros_like(hc_ref)

    pre_scale = preg_ref[...] * (1.0 + sc_ref[0])
    ws = ws_ref[...]
    r_t = lax.broadcasted_iota(jnp.int32, ws.shape, 0) % SGU_BLOCK
    c_s = lax.broadcasted_iota(jnp.int32, ws.shape, 1)
    ws = jnp.where(c_s <= r_t, ws, 0.0).astype(BF16)
    lam = lam_ref[...]
    softplus_neg_lam = jnp.maximum(-lam, 0.0) + jnp.log(1.0 + jnp.exp(-jnp.abs(lam)))

    n = tm // MIX_SPLIT
    zs = []
    for part in range(MIX_SPLIT):
        x = x_ref[0, part * n:(part + 1) * n, :]
        h = _rms(x, pre_scale) + sh_ref[0]
        zs.append(_bdot(h.astype(BF16), win_ref[...]))
    for part in range(MIX_SPLIT):
        z = zs[part]
        r0 = part * n
        rows = slice(r0, r0 + n)

        nt = (((1,), (1,)), ((), ()))
        qn = _rms(z[:, C_Q:C_Q + Q_LORA], qg_ref[...]).astype(BF16)
        qq = lax.dot_general(wq_ref[...], qn, nt, preferred_element_type=F32)
        cosq = cosq_ref[:, rows]
        sinq = sinq_ref[:, rows]
        for hd in range(MLA_HEADS):
            lo = hd * HEAD_PAD
            ro, po = lo + QK_NOPE, lo + QK_NOPE + QK_ROPE
            q_ref[0, lo:ro, rows] = (qq[lo:ro] * Q_SCALE).astype(BF16)
            q_ref[0, ro:po, rows] = (qq[ro:po] * cosq + qq[po:po + QK_ROPE] * sinq).astype(BF16)
            q_ref[0, po:po + QK_ROPE, rows] = qq[po:po + QK_ROPE].astype(BF16)
        kvn = _rms(z[:, C_KV:C_KV + KV_LORA], kvg_ref[...]).astype(BF16)
        kr = (z[:, C_KR:C_KR + LANE] * tk_ref[rows, :]).astype(BF16)
        k_in = jnp.concatenate([kvn, kr], axis=-1)
        k_ref[0, rows, :] = _bdot(k_in, wk_ref[...]).astype(BF16)
        vv = lax.dot_general(wv_ref[...], kvn, nt, preferred_element_type=F32)
        ones_row = lax.broadcasted_iota(jnp.int32, vv.shape, 0) % V_ROWS == V_DIM
        vv = jnp.where(ones_row, 1.0, vv).astype(BF16)
        piece = min(n, TK)
        for c in range(n // piece):
            g0 = r0 + c * piece
            v_ref[0, g0 // TK, :, g0 % TK:g0 % TK + piece] = vv[:, c * piece:(c + 1) * piece]

        u = _gelu(z[:, C_U:C_U + SGU_WIDTH])
        gv = _gelu(z[:, C_V:C_V + SGU_WIDTH])
        mu = jnp.mean(gv, axis=-1, keepdims=True)
        var = jnp.mean(jnp.square(gv - mu), axis=-1, keepdims=True)
        vn = ((gv - mu) * lax.rsqrt(var + LN_EPS) * lng_ref[...] + lnb_ref[...]).astype(BF16)
        grp = lax.broadcasted_iota(jnp.int32, (SGU_BLOCK, SGU_WIDTH), 1) // SGU_GROUP_DIM
        for blk in range(n // SGU_BLOCK):
            b0 = blk * SGU_BLOCK
            res = _bdot(ws, vn[b0:b0 + SGU_BLOCK])
            mixed = bs_ref[...]
            for g in range(SGU_GROUPS):
                mixed = mixed + jnp.where(grp == g, res[g * SGU_BLOCK:(g + 1) * SGU_BLOCK], 0.0)
            y_ref[0, r0 + b0:r0 + b0 + SGU_BLOCK, 0:SGU_WIDTH] = (u[b0:b0 + SGU_BLOCK] * mixed).astype(BF16)

        zx = z[:, C_X:C_X + LRU_WIDTH]
        prev8 = prev_ref[...]
        cw = cw_ref[...]
        xc = cb_ref[...] + zx * cw[CONV_W - 1:CONV_W]
        for k in range(1, CONV_W):
            xc = xc + _shift_rows(zx, prev8, k) * cw[CONV_W - 1 - k:CONV_W - k]
        prev_ref[...] = zx[n - SUBLANE:]
        xcb = xc.astype(BF16)
        r = _sigmoid(_bdot(xcb, wa_ref[...]) + ba_ref[...])
        ig = _sigmoid(_bdot(xcb, wx_ref[...]) + bx_ref[...])
        log_a = -LRU_C * r * softplus_neg_lam
        a = jnp.exp(log_a)
        t = jnp.tanh(log_a)
        b = jnp.sqrt(-2.0 * t / (1.0 - t)) * (ig * xc)
        row = lax.broadcasted_iota(jnp.int32, a.shape, 0)
        k = 1
        while k < SUBLANE:
            valid = row >= k
            a_sh = pltpu.roll(a, k, 0)
            b_sh = pltpu.roll(b, k, 0)
            b = jnp.where(valid, b + a * b_sh, b)
            a = jnp.where(valid, a * a_sh, a)
            k *= 2
        while k < n:
            b = jnp.concatenate([b[:k], b[k:] + a[k:] * b[:-k]], axis=0)
            a = jnp.concatenate([a[:k], a[k:] * a[:-k]], axis=0)
            k *= 2
        hs = b + a * hc_ref[0:1]
        hc_ref[0:1] = hs[n - 1:n]
        y_ref[0, rows, SGU_WIDTH:SGU_WIDTH + LRU_WIDTH] = (hs * _gelu(z[:, C_Y:C_Y + LRU_WIDTH])).astype(BF16)


def _mix_in(x, mod, l, p, tabs):
    B, S, D = x.shape
    tm = TM_MIX
    batch = lambda b, s: b
    tab = pl.BlockSpec((tm, LANE), lambda b, s: (s, 0))
    tab_t = pl.BlockSpec((QK_ROPE, tm), lambda b, s: (0, s))
    assert tm % TK == 0
    consts = [p[k] for k in ("pre_mix_g", "w_in", "qg", "w_q", "kvg", "w_k", "w_v")]
    consts2 = [p[k] for k in ("ln_g", "ln_b", "w_s", "b_s", "conv_w", "conv_b", "wa", "ba", "wx", "bx", "lam")]
    qw = MLA_HEADS * HEAD_PAD
    return pl.pallas_call(
        _mix_in_kernel,
        grid=(B, S // tm),
        in_specs=[pl.BlockSpec((1, tm, D), lambda b, s: (b, s, 0)), _mod_block(mod, l, 0, batch),
                  _mod_block(mod, l, 1, batch)]
        + [_layer_block(a, l) for a in consts] + [tab_t, tab_t, tab] + [_layer_block(a, l) for a in consts2],
        out_specs=[
            pl.BlockSpec((1, qw, tm), lambda b, s: (b, 0, s)),
            pl.BlockSpec((1, tm, qw), lambda b, s: (b, s, 0)),
            pl.BlockSpec((1, tm // TK, MLA_HEADS * V_ROWS, TK), lambda b, s: (b, s, 0, 0)),
            pl.BlockSpec((1, tm, SGU_WIDTH + LRU_WIDTH), lambda b, s: (b, s, 0)),
        ],
        out_shape=[
            jax.ShapeDtypeStruct((B, qw, S), BF16),
            jax.ShapeDtypeStruct((B, S, qw), BF16),
            jax.ShapeDtypeStruct((B, S // TK, MLA_HEADS * V_ROWS, TK), BF16),
            jax.ShapeDtypeStruct((B, S, SGU_WIDTH + LRU_WIDTH), BF16),
        ],
        scratch_shapes=[pltpu.VMEM((SUBLANE, LRU_WIDTH), F32), pltpu.VMEM((SUBLANE, LRU_WIDTH), F32)],
        compiler_params=pltpu.CompilerParams(
            dimension_semantics=("arbitrary", "arbitrary"), vmem_limit_bytes=VMEM_LIMIT),
        name="mix_in",
    )(x, mod, mod, *consts, tabs["cosq"], tabs["sinq"], tabs["tk"], *consts2)


def _attn_kernel(q_ref, k_ref, v_ref, o_ref, m_ref, acc_ref):
    i = pl.program_id(2)
    heads = q_ref.shape[1] // HEAD_PAD
    feat = lambda hd: slice(hd * HEAD_PAD, (hd + 1) * HEAD_PAD)

    m_ref[...] = jnp.full(m_ref.shape, NEG_INF, F32)
    acc_ref[...] = jnp.zeros(acc_ref.shape, F32)

    def advance(hd, s, v, lanes):
        m = m_ref[hd, :, lanes]
        m_new = jnp.maximum(m, jnp.max(s, axis=0, keepdims=True))
        p = jnp.exp2(s - m_new).astype(BF16)
        acc_ref[hd, 0:V_ROWS, lanes] = jnp.exp2(m - m_new) * acc_ref[hd, 0:V_ROWS, lanes] + _bdot(v, p)
        m_ref[hd, :, lanes] = m_new

    def step(t, key_lo, n_keys, query_lo, masked):
        keys = pl.ds(pl.multiple_of(t * TK + key_lo, n_keys), n_keys)
        lanes = slice(query_lo, TQ)

        def scores(hd):
            s = _bdot(k_ref[0, keys, feat(hd)], q_ref[0, feat(hd), lanes])
            if masked:
                kc = (lax.broadcasted_iota(jnp.int32, s.shape, 0) + key_lo) // CHUNK
                qc = (lax.broadcasted_iota(jnp.int32, s.shape, 1) + query_lo) // CHUNK
                s = jnp.where(kc <= qc, s, NEG_INF)
            return s

        ss = {hd: scores(hd) for hd in range(min(AHEAD, heads))}
        for hd in range(heads):
            if hd + AHEAD < heads:
                ss[hd + AHEAD] = scores(hd + AHEAD)
            advance(hd, ss.pop(hd), v_ref[0, t, hd * V_ROWS:(hd + 1) * V_ROWS, key_lo:key_lo + n_keys], lanes)

    def body(t, c):
        step(t, 0, TK, 0, False)
        return c

    lax.fori_loop(0, i, body, 0)
    step(i, 0, TK // 2, 0, True)
    step(i, TK // 2, TK // 2, TQ // 2, True)
    outs = []
    for hd in range(heads):
        acc = acc_ref[hd]
        outs.append((acc / acc[V_DIM:V_DIM + 1]).T[:, :V_DIM])
    o_ref[0] = jnp.concatenate(outs, axis=-1).astype(BF16)


def _attention(q, k, v):
    B, S, _ = k.shape
    hp = ATTN_HEADS_PER_STEP
    assert TQ == TK
    return pl.pallas_call(
        _attn_kernel,
        grid=(B, MLA_HEADS // hp, S // TQ),
        in_specs=[
            pl.BlockSpec((1, hp * HEAD_PAD, TQ), lambda b, h, i: (b, h, i)),
            pl.BlockSpec((1, S, hp * HEAD_PAD), lambda b, h, i: (b, 0, h)),
            pl.BlockSpec((1, S // TK, hp * V_ROWS, TK), lambda b, h, i: (b, 0, h, 0)),
        ],
        out_specs=pl.BlockSpec((1, TQ, hp * V_DIM), lambda b, h, i: (b, i, h)),
        out_shape=jax.ShapeDtypeStruct((B, S, MLA_HEADS * V_DIM), BF16),
        scratch_shapes=[
            pltpu.VMEM((hp, 1, TQ), F32),
            pltpu.VMEM((hp, HEAD_PAD, TQ), F32),
        ],
        compiler_params=pltpu.CompilerParams(
            dimension_semantics=("arbitrary", "arbitrary", "arbitrary"), vmem_limit_bytes=VMEM_LIMIT),
        name="attention",
    )(q, k, v)


def _mix_out_kernel(o_ref, y_ref, x_ref, wo_ref, postg_ref, gm_ref, preg_ref, scf_ref, shf_ref,
                    xo_ref, h_ref):
    no = o_ref.shape[1]
    tm = x_ref.shape[0]
    parts = [slice(r, r + tm // OUT_SPLIT) for r in range(0, tm, tm // OUT_SPLIT)]
    ys = [_bdot(o_ref[rows, :], wo_ref[0:no]) + _bdot(y_ref[rows, :], wo_ref[no:]) for rows in parts]
    post_scale = gm_ref[0] * postg_ref[...]
    pre_scale = preg_ref[...] * (1.0 + scf_ref[0])
    for rows, y in zip(parts, ys):
        x = x_ref[rows, :] + _rms(y, post_scale)
        xo_ref[rows, :] = x
        h_ref[rows, :] = (_rms(x, pre_scale) + shf_ref[0]).astype(BF16)


def _store_token_tiles(ref, x):
    rows, d = x.shape
    assert d == SUBLANE * LANE
    for j in range(SUBLANE):
        ref[pl.ds(j, rows, stride=SUBLANE), :] = x[:, j * LANE:(j + 1) * LANE]


def _load_token_tiles(ref):
    rows = ref.shape[0] // SUBLANE
    return jnp.concatenate([ref[pl.ds(j, rows, stride=SUBLANE), :] for j in range(SUBLANE)], axis=-1)


def _token_copy(src, dst, src_token, dst_token, sem):
    def rows(t):
        start = t * SUBLANE
        return pl.ds(start if isinstance(t, int) else pl.multiple_of(start, SUBLANE), SUBLANE)

    return pltpu.make_async_copy(src.at[rows(src_token)], dst.at[rows(dst_token)], sem)


def _mix_out_moe_kernel(o_ref, y_ref, x_ref, wo_ref, postg_ref, gm_ref, preg_ref, scf_ref, shf_ref,
                        router_ref, xo_ref, h_ref, route_ref):
    no = o_ref.shape[1]
    tm = x_ref.shape[0]
    parts = [slice(r, r + tm // OUT_SPLIT) for r in range(0, tm, tm // OUT_SPLIT)]
    ys = [_bdot(o_ref[rows, :], wo_ref[0:no]) + _bdot(y_ref[rows, :], wo_ref[no:]) for rows in parts]
    post_scale = gm_ref[0] * postg_ref[...]
    pre_scale = preg_ref[...] * (1.0 + scf_ref[0])
    for rows, y in zip(parts, ys):
        x = x_ref[rows, :] + _rms(y, post_scale)
        xo_ref[rows, :] = x
        h = _rms(x, pre_scale) + shf_ref[0]
        _store_token_tiles(h_ref.at[pl.ds(rows.start * SUBLANE, (rows.stop - rows.start) * SUBLANE)], h)
        h_hi, h_lo = _split_bf16(h)
        t = _bdot(h_hi, router_ref[...])
        logits = t[:, :LANE] + t[:, LANE:] + _bdot(h_lo, router_ref[:, :LANE])
        lane = lax.broadcasted_iota(jnp.int32, logits.shape, 1)
        logits = jnp.where(lane < N_EXPERTS, logits, -jnp.inf)
        m1 = jnp.max(logits, axis=-1, keepdims=True)
        i1 = jnp.min(jnp.where(logits == m1, lane, LANE), axis=-1, keepdims=True)
        rest = jnp.where(lane == i1, -jnp.inf, logits)
        m2 = jnp.max(rest, axis=-1, keepdims=True)
        i2 = jnp.min(jnp.where(rest == m2, lane, LANE), axis=-1, keepdims=True)
        e = jnp.exp(m2 - m1)
        g1 = 1.0 / (1.0 + e)
        route_ref[rows, :] = (jnp.where(lane == 0, i1.astype(F32), 0.0) + jnp.where(lane == 1, i2.astype(F32), 0.0)
                              + jnp.where(lane == 2, g1, 0.0) + jnp.where(lane == 3, e * g1, 0.0))


def _mix_out(o, y, x, mod, l, p, router, tiles_per_batch):
    N, D = x.shape
    tm = TM_OUT
    row = lambda w: pl.BlockSpec((tm, w), lambda i: (i, 0))
    batch = lambda i: i // tiles_per_batch
    in_specs = [row(o.shape[1]), row(y.shape[1]), row(D), _layer_block(p["w_o"], l),
                _layer_block(p["post_mix_g"], l), _mod_block(mod, l, 2, batch), _layer_block(p["pre_ffn_g"], l),
                _mod_block(mod, l, 4, batch), _mod_block(mod, l, 3, batch)]
    out_specs = [row(D), row(D)]
    out_shape = [jax.ShapeDtypeStruct((N, D), F32), jax.ShapeDtypeStruct((N, D), BF16)]
    args = [o, y, x, p["w_o"], p["post_mix_g"], mod, p["pre_ffn_g"], mod, mod]
    body = _mix_out_kernel
    if router is not None:
        in_specs.append(_layer_block(router, l // 2))
        out_specs = [row(D), pl.BlockSpec((tm * SUBLANE, LANE), lambda i: (i, 0)), row(LANE)]
        out_shape = [jax.ShapeDtypeStruct((N, D), F32), jax.ShapeDtypeStruct((N * SUBLANE, LANE), F32),
                     jax.ShapeDtypeStruct((N, LANE), F32)]
        args.append(router)
        body = _mix_out_moe_kernel
    return pl.pallas_call(
        body,
        grid=(N // tm,),
        in_specs=in_specs,
        out_specs=out_specs,
        out_shape=out_shape,
        compiler_params=pltpu.CompilerParams(dimension_semantics=("arbitrary",), vmem_limit_bytes=VMEM_LIMIT),
        name="mix_out",
    )(*args)


def _swiglu(h, wg, wu, wd):
    g = _bdot(h, wg)
    u = _bdot(h, wu)
    return _bdot((_silu(g) * u).astype(BF16), wd)


def _ffn_kernel(h_ref, x_ref, wg_ref, wu_ref, wd_ref, postg_ref, gf_ref, o_ref, acc_ref):
    j = pl.program_id(1)

    @pl.when(j == 0)
    def _():
        acc_ref[...] = jnp.zeros_like(acc_ref)

    acc_ref[...] += _swiglu(h_ref[...], wg_ref[...], wu_ref[...], wd_ref[...])

    @pl.when(j == pl.num_programs(1) - 1)
    def _():
        o_ref[...] = x_ref[...] + _rms(acc_ref[...], gf_ref[0] * postg_ref[...])


def _ffn(h, x, mod, l, p, tiles_per_batch):
    N, D = x.shape
    wg, wu, wd = p["ffn_w_gate"], p["ffn_w_up"], p["ffn_w_down"]
    FF = wg.shape[2]
    tm = TM_FFN
    tf = FF // 2
    lf = l // 2
    return pl.pallas_call(
        _ffn_kernel,
        grid=(N // tm, FF // tf),
        in_specs=[
            pl.BlockSpec((tm, D), lambda i, j: (i, 0)),
            pl.BlockSpec((tm, D), lambda i, j: (i, 0)),
            pl.BlockSpec((None, D, tf), lambda i, j: (lf, 0, j)),
            pl.BlockSpec((None, D, tf), lambda i, j: (lf, 0, j)),
            pl.BlockSpec((None, tf, D), lambda i, j: (lf, j, 0)),
            _layer_block(p["post_ffn_g"], l),
            _mod_block(mod, l, 5, lambda i, j: i // tiles_per_batch),
        ],
        out_specs=pl.BlockSpec((tm, D), lambda i, j: (i, 0)),
        out_shape=jax.ShapeDtypeStruct((N, D), F32),
        scratch_shapes=[pltpu.VMEM((tm, D), F32)],
        compiler_params=pltpu.CompilerParams(
            dimension_semantics=("arbitrary", "arbitrary"), vmem_limit_bytes=VMEM_LIMIT),
        name="ffn_dense",
    )(h, x, wg, wu, wd, p["post_ffn_g"], mod)


def _moe_dispatch_kernel(pad_lo_ref, pad_hi_ref, slot_ref, h_ref, hs_hbm, zero_ref, sem):
    tm = h_ref.shape[0] // SUBLANE

    @pl.when(pl.program_id(0) == 0)
    def _():
        zero_ref[...] = jnp.zeros(zero_ref.shape, zero_ref.dtype)
        chunk = zero_ref.shape[0] // SUBLANE

        def big(s):
            rows = pl.ds(pl.multiple_of(s * SUBLANE, SUBLANE), chunk * SUBLANE)
            return pltpu.make_async_copy(zero_ref, hs_hbm.at[rows], sem)

        for e in range(pad_lo_ref.shape[0]):
            lo, hi = pad_lo_ref[e], pad_hi_ref[e]
            n_big = (hi - lo) // chunk
            mid = lo + n_big * chunk

            def start_big(j, c):
                big(lo + j * chunk).start()
                return c

            def wait_big(j, c):
                big(lo + j * chunk).wait()
                return c

            def start_pad(s, c):
                _token_copy(zero_ref, hs_hbm, 0, s, sem).start()
                return c

            def wait_pad(s, c):
                _token_copy(zero_ref, hs_hbm, 0, s, sem).wait()
                return c

            lax.fori_loop(0, n_big, start_big, 0)
            lax.fori_loop(mid, hi, start_pad, 0)
            lax.fori_loop(0, n_big, wait_big, 0)
            lax.fori_loop(mid, hi, wait_pad, 0)

    def start(r, c):
        for k in range(2):
            _token_copy(h_ref, hs_hbm, r, slot_ref[0, 0, 2 * r + k], sem).start(priority=k)
        return c

    lax.fori_loop(0, tm, start, 0, unroll=4)

    def wait(r, c):
        for k in range(2):
            _token_copy(h_ref, hs_hbm, r, 0, sem).wait()
        return c

    lax.fori_loop(0, tm, wait, 0, unroll=4)


def _moe_dispatch(pad_lo, pad_hi, slots, h_tiles, n_slots):
    N = h_tiles.shape[0] // SUBLANE
    tm = MOE_DMA_BLOCK
    return pl.pallas_call(
        _moe_dispatch_kernel,
        grid_spec=pltpu.PrefetchScalarGridSpec(
            num_scalar_prefetch=2,
            grid=(N // tm,),
            in_specs=[
                pl.BlockSpec((1, 1, 2 * tm), lambda i, lo, hi: (i, 0, 0), memory_space=pltpu.SMEM),
                pl.BlockSpec((tm * SUBLANE, LANE), lambda i, lo, hi: (i, 0)),
            ],
            out_specs=pl.BlockSpec(memory_space=pl.ANY),
            scratch_shapes=[pltpu.VMEM((MOE_ZERO_CHUNK * SUBLANE, LANE), F32), pltpu.SemaphoreType.DMA(())],
        ),
        out_shape=jax.ShapeDtypeStruct((n_slots * SUBLANE, LANE), F32),
        compiler_params=pltpu.CompilerParams(dimension_semantics=("arbitrary",)),
        name="moe_dispatch",
    )(pad_lo, pad_hi, slots.reshape(N // tm, 1, 2 * tm), h_tiles)


def _moe_expert_kernel(texp_ref, nused_ref, hs_ref, wg_ref, wu_ref, wd_ref, ys_ref):
    del texp_ref

    @pl.when(pl.program_id(0) < nused_ref[0])
    def _():
        x = _load_token_tiles(hs_ref).astype(BF16)
        cast = lambda w_ref: w_ref[...].astype(BF16)
        _store_token_tiles(ys_ref, _swiglu(x, cast(wg_ref), cast(wu_ref), cast(wd_ref)))

    @pl.when(pl.program_id(0) >= nused_ref[0])
    def _():
        ys_ref[...] = jnp.zeros(ys_ref.shape, F32)


def _moe_experts(tile_expert, n_used, hs, wg, wu, wd, lm):
    n_slots = hs.shape[0] // SUBLANE
    _, E, D, FF = wg.shape
    t = MOE_TILE
    tile = pl.BlockSpec((t * SUBLANE, LANE), lambda i, te, nu: (i, 0))
    return pl.pallas_call(
        _moe_expert_kernel,
        grid_spec=pltpu.PrefetchScalarGridSpec(
            num_scalar_prefetch=2,
            grid=(n_slots // t,),
            in_specs=[
                tile,
                pl.BlockSpec((None, None, D, FF), lambda i, te, nu: (lm, te[i], 0, 0)),
                pl.BlockSpec((None, None, D, FF), lambda i, te, nu: (lm, te[i], 0, 0)),
                pl.BlockSpec((None, None, FF, D), lambda i, te, nu: (lm, te[i], 0, 0)),
            ],
            out_specs=tile,
        ),
        out_shape=jax.ShapeDtypeStruct(hs.shape, F32),
        compiler_params=pltpu.CompilerParams(dimension_semantics=("arbitrary",), vmem_limit_bytes=VMEM_LIMIT),
        name="moe_experts",
    )(tile_expert, n_used, hs, wg, wu, wd)


def _moe_combine_kernel(slot_ref, next_slot_ref, ys_hbm, route_ref, x_ref, postg_ref, gf_ref, o_ref, buf_ref, sem):
    tm = x_ref.shape[0]
    i = pl.program_id(0)
    cur = lax.rem(i, 2)

    def gather(slots, half):
        def start(r, c):
            for k in range(2):
                _token_copy(ys_hbm, buf_ref.at[half, k], slots[0, 0, 2 * r + k], r, sem.at[half]).start(priority=k)
            return c

        lax.fori_loop(0, tm, start, 0, unroll=4)

    @pl.when(i == 0)
    def _():
        gather(slot_ref, 0)

    @pl.when(i + 1 < pl.num_programs(0))
    def _():
        gather(next_slot_ref, 1 - cur)

    def wait(r, c):
        for k in range(2):
            _token_copy(ys_hbm, buf_ref.at[cur, k], 0, r, sem.at[cur]).wait()
        return c

    lax.fori_loop(0, tm, wait, 0, unroll=4)
    route = route_ref[...]
    lane = lax.broadcasted_iota(jnp.int32, route.shape, 1)
    f = None
    for k in range(2):
        gate = jnp.sum(jnp.where(lane == 2 + k, route, 0.0), axis=-1, keepdims=True)
        y = _load_token_tiles(buf_ref.at[cur, k])
        f = gate * y if f is None else f + gate * y
    o_ref[...] = x_ref[...] + _rms(f, gf_ref[0] * postg_ref[...])


def _moe_combine(slots, ys, route, x, mod, l, p, tiles_per_batch):
    N, D = x.shape
    tm = TM_FFN
    n = N // tm
    slots = slots.reshape(n, 1, 2 * tm)
    return pl.pallas_call(
        _moe_combine_kernel,
        grid=(n,),
        in_specs=[
            pl.BlockSpec((1, 1, 2 * tm), lambda i: (i, 0, 0), memory_space=pltpu.SMEM),
            pl.BlockSpec((1, 1, 2 * tm), lambda i: (jnp.minimum(i + 1, n - 1), 0, 0), memory_space=pltpu.SMEM),
            pl.BlockSpec(memory_space=pl.ANY),
            pl.BlockSpec((tm, LANE), lambda i: (i, 0)),
            pl.BlockSpec((tm, D), lambda i: (i, 0)),
            _layer_block(p["post_ffn_g"], l),
            _mod_block(mod, l, 5, lambda i: i // tiles_per_batch),
        ],
        out_specs=pl.BlockSpec((tm, D), lambda i: (i, 0)),
        out_shape=jax.ShapeDtypeStruct((N, D), F32),
        scratch_shapes=[pltpu.VMEM((2, 2, tm * SUBLANE, LANE), F32), pltpu.SemaphoreType.DMA((2,))],
        compiler_params=pltpu.CompilerParams(dimension_semantics=("arbitrary",), vmem_limit_bytes=VMEM_LIMIT),
        name="moe_combine",
    )(slots, slots, ys, route, x, p["post_ffn_g"], mod)


def _moe_plan(route, n_slots):
    t = MOE_TILE
    pair_expert = route[:, :2].astype(jnp.int32).reshape(-1)
    onehot = (pair_expert[None, :] == jnp.arange(N_EXPERTS, dtype=jnp.int32)[:, None]).astype(jnp.int32)
    csum = jnp.cumsum(onehot, axis=1)
    count = csum[:, -1]
    padded = (count + t - 1) // t * t
    ends = jnp.cumsum(padded)
    starts = ends - padded
    slot = jnp.sum(onehot * (csum - 1 + starts[:, None]), axis=0)
    n_used = ends[-1:] // t
    tile_start = jnp.arange(n_slots // t, dtype=jnp.int32) * t
    tile_expert = jnp.sum((tile_start[:, None] >= ends[None, :]).astype(jnp.int32), axis=1)
    last_used = tile_expert[jnp.maximum(n_used[0] - 1, 0)]
    tile_expert = jnp.where(tile_start < ends[-1], tile_expert, last_used)
    pad_lo = jnp.concatenate([starts + count, ends[-1:]]).astype(jnp.int32)
    pad_hi = jnp.concatenate([ends, jnp.full((1,), n_slots, ends.dtype)]).astype(jnp.int32)
    return slot, pad_lo, pad_hi, tile_expert, n_used.astype(jnp.int32)


def _moe(h_tiles, x, route, mod, l, p, tiles_per_batch):
    N = x.shape[0]
    n_slots = 2 * N + N_EXPERTS * MOE_TILE
    slot, pad_lo, pad_hi, tile_expert, n_used = _moe_plan(route, n_slots)
    hs = _moe_dispatch(pad_lo, pad_hi, slot, h_tiles, n_slots)
    ys = _moe_experts(tile_expert, n_used, hs, p["moe_w_gate"], p["moe_w_up"], p["moe_w_down"], l // 2)
    return _moe_combine(slot, ys, route, x, mod, l, p, tiles_per_batch)


def _rope_partner(w):
    half = QK_ROPE // 2
    return jnp.concatenate([-w[..., half:], w[..., :half]], axis=-1)


def _rope_tables(S):
    pos = jnp.arange(S, dtype=F32)
    inv = ROPE_THETA ** (-jnp.arange(0, QK_ROPE, 2, dtype=F32) / QK_ROPE)
    ang = pos[:, None] * inv[None, :]
    cos, sin = jnp.cos(ang), jnp.sin(ang)
    cos2 = jnp.concatenate([cos, cos], axis=-1)
    sin2 = jnp.concatenate([sin, sin], axis=-1)
    tk = jnp.concatenate([cos2, sin2, jnp.zeros((S, LANE - 2 * QK_ROPE), F32)], axis=-1)
    return {"cosq": cos2.T * Q_SCALE, "sinq": sin2.T * Q_SCALE, "tk": tk}


def _prepare_params(w_in, q_norm_g, w_uq, kv_norm_g, w_ukv, sgu_ln_g, sgu_ln_b, sgu_w, sgu_b, conv_w, conv_b,
                    lru_wa, lru_ba, lru_wx, lru_bx, lru_lambda, pre_mix_g, post_mix_g, w_o, pre_ffn_g, post_ffn_g,
                    ffn_w_gate, ffn_w_up, ffn_w_down, moe_router, moe_w_gate, moe_w_up, moe_w_down):
    L, D, _ = w_in.shape
    o_q, o_kv = 0, Q_LORA
    o_kr = o_kv + KV_LORA
    o_u = o_kr + QK_ROPE
    o_v = o_u + SGU_WIDTH
    o_x = o_v + SGU_WIDTH
    o_y = o_x + LRU_WIDTH
    w_kr = w_in[..., o_kr:o_kr + QK_ROPE]
    w_in_p = jnp.concatenate([
        w_in[..., o_q:o_q + Q_LORA], w_in[..., o_kv:o_kv + KV_LORA], w_in[..., o_u:o_u + SGU_WIDTH],
        w_in[..., o_v:o_v + SGU_WIDTH], w_in[..., o_x:o_x + LRU_WIDTH], w_in[..., o_y:o_y + LRU_WIDTH],
        w_kr, _rope_partner(w_kr), jnp.zeros((L, D, LANE - 2 * QK_ROPE), F32)], axis=-1).astype(BF16)

    H = MLA_HEADS
    pad = HEAD_PAD - QK_NOPE - QK_ROPE
    wq = w_uq.reshape(L, Q_LORA, H, QK_NOPE + QK_ROPE)
    assert pad == QK_ROPE
    w_q = jnp.concatenate([wq, _rope_partner(wq[..., QK_NOPE:])], axis=-1).reshape(L, Q_LORA, H * HEAD_PAD)
    w_q = jnp.swapaxes(w_q, 1, 2).astype(BF16)

    wkv = w_ukv.reshape(L, KV_LORA, H, QK_NOPE + V_DIM)
    wk_lat = jnp.concatenate([wkv[..., :QK_NOPE], jnp.zeros((L, KV_LORA, H, HEAD_PAD - QK_NOPE), F32)],
                             axis=-1).reshape(L, KV_LORA, H * HEAD_PAD)
    eye = jnp.eye(QK_ROPE, dtype=F32)
    place = jnp.concatenate([jnp.zeros((QK_ROPE, QK_NOPE), F32), eye, jnp.zeros((QK_ROPE, pad), F32)], axis=-1)
    place = jnp.tile(place, (2, H))
    tail = jnp.concatenate([place, jnp.zeros((LANE - 2 * QK_ROPE, H * HEAD_PAD), F32)], axis=0)
    w_k = jnp.concatenate([wk_lat, jnp.broadcast_to(tail, (L,) + tail.shape)], axis=1).astype(BF16)
    w_v = jnp.swapaxes(
        jnp.concatenate([wkv[..., QK_NOPE:], jnp.zeros((L, KV_LORA, H, V_ROWS - V_DIM), F32)],
                        axis=-1).reshape(L, KV_LORA, H * V_ROWS), 1, 2).astype(BF16)

    def block_diag(w):
        sel = jnp.eye(LRU_HEADS, dtype=F32)
        return jnp.einsum("lhij,hg->lhigj", w, sel).reshape(L, LRU_WIDTH, LRU_WIDTH).astype(BF16)

    row = lambda a: a.reshape(a.shape[0], 1, -1)
    r = jnp.pad(moe_router, ((0, 0), (0, 0), (0, LANE - N_EXPERTS)))
    r_hi = r.astype(BF16)
    return {
        "pre_mix_g": row(pre_mix_g), "w_in": w_in_p, "qg": row(q_norm_g), "w_q": w_q, "kvg": row(kv_norm_g),
        "w_k": w_k, "w_v": w_v, "ln_g": row(sgu_ln_g), "ln_b": row(sgu_ln_b),
        "w_s": sgu_w.reshape(L, SGU_GROUPS * SGU_BLOCK, SGU_BLOCK),
        "b_s": jnp.repeat(jnp.swapaxes(sgu_b, 1, 2), SGU_GROUP_DIM, axis=2),
        "conv_w": conv_w, "conv_b": row(conv_b), "wa": block_diag(lru_wa), "ba": row(lru_ba),
        "wx": block_diag(lru_wx), "bx": row(lru_bx), "lam": row(lru_lambda),
        "w_o": w_o.astype(BF16), "post_mix_g": row(post_mix_g), "pre_ffn_g": row(pre_ffn_g),
        "post_ffn_g": row(post_ffn_g),
        "ffn_w_gate": ffn_w_gate.astype(BF16), "ffn_w_up": ffn_w_up.astype(BF16),
        "ffn_w_down": ffn_w_down.astype(BF16),
        "router": jnp.concatenate([r_hi, (r - r_hi.astype(F32)).astype(BF16)], axis=-1),
        "moe_w_gate": moe_w_gate, "moe_w_up": moe_w_up, "moe_w_down": moe_w_down,
    }


def kernel(x, c, w_mod, b_mod, pre_mix_g, post_mix_g, w_in, q_norm_g, w_uq, kv_norm_g, w_ukv, sgu_ln_g,
           sgu_ln_b, sgu_w, sgu_b, conv_w, conv_b, lru_wa, lru_ba, lru_wx, lru_bx, lru_lambda, w_o,
           pre_ffn_g, post_ffn_g, ffn_w_gate, ffn_w_up, ffn_w_down, moe_router, moe_w_gate, moe_w_up,
           moe_w_down):
    B, S, D = x.shape
    L = w_mod.shape[0]
    N = B * S
    tabs = _rope_tables(S)
    p = _prepare_params(w_in, q_norm_g, w_uq, kv_norm_g, w_ukv, sgu_ln_g, sgu_ln_b, sgu_w, sgu_b, conv_w, conv_b,
                        lru_wa, lru_ba, lru_wx, lru_bx, lru_lambda, pre_mix_g, post_mix_g, w_o, pre_ffn_g,
                        post_ffn_g, ffn_w_gate, ffn_w_up, ffn_w_down, moe_router, moe_w_gate, moe_w_up, moe_w_down)
    mod = _modulation(c, w_mod, b_mod).reshape(L, B * 6, 1, D)
    xf = x.reshape(N, D)
    for l in range(L):
        q, k, v, ymix = _mix_in(xf.reshape(B, S, D), mod, l, p, tabs)
        o = _attention(q, k, v)
        moe = l % 2 == 1
        res = _mix_out(o.reshape(N, -1), ymix.reshape(N, -1), xf, mod, l, p, p["router"] if moe else None,
                       S // TM_OUT)
        if moe:
            xf, h_tiles, route = res
            xf = _moe(h_tiles, xf, route, mod, l, p, S // TM_FFN)
        else:
            xf, h2 = res
            xf = _ffn(h2, xf, mod, l, p, S // TM_FFN)
    return xf.reshape(B, S, D)
```

```python
import jax
import jax.numpy as jnp
from jax import lax
from jax.experimental import pallas as pl
from jax.experimental.pallas import tpu as pltpu

F32 = jnp.float32
BF16 = jnp.bfloat16

MLA_HEADS = 8
QK_NOPE = 64
QK_ROPE = 32
V_DIM = 64
Q_LORA = 256
KV_LORA = 128
ROPE_THETA = 10000.0
CHUNK = 64
SGU_GROUPS = 4
SGU_GROUP_DIM = 64
SGU_WIDTH = SGU_GROUPS * SGU_GROUP_DIM
SGU_BLOCK = 128
LRU_HEADS = 4
LRU_HEAD_DIM = 64
LRU_WIDTH = LRU_HEADS * LRU_HEAD_DIM
CONV_W = 4
LRU_C = 8.0
N_EXPERTS = 8
NEG_INF = -1e30
RMS_EPS = 1e-6
LN_EPS = 1e-5
LOG2_E = 1.4426950408889634
GELU_C = 0.7978845608028654
GELU_A = 0.044715
MOD_CHUNKS = 6
MOD_SHIFT_MIX, MOD_SCALE_MIX, MOD_GATE_MIX, MOD_SHIFT_FFN, MOD_SCALE_FFN, MOD_GATE_FFN = range(MOD_CHUNKS)
ROUTE_EXPERT = 0
ROUTE_GATE = 2

LANE = 128
SUBLANE = 8
HEAD_PAD = LANE
V_ROWS = 80
Q_SCALE = (QK_NOPE + QK_ROPE) ** -0.5 * LOG2_E

C_Q = 0
C_KV = C_Q + Q_LORA
C_U = C_KV + KV_LORA
C_V = C_U + SGU_WIDTH
C_X = C_V + SGU_WIDTH
C_Y = C_X + LRU_WIDTH
C_KR = C_Y + LRU_WIDTH
D_IN_PAD = C_KR + LANE

VMEM_LIMIT = 56 * 1024 * 1024

TM_MIX = 1024
MIX_SPLIT = 4
TQ = 512
TK = 512
ATTN_HEADS_PER_STEP = 8
AHEAD = 3
TM_OUT = 1024
OUT_SPLIT = 4
TM_FFN = 512
MOE_TILE = 512
MOE_DMA_BLOCK = 2048
MOE_ZERO_CHUNK = 64


def _rms(x, g):
    return x * lax.rsqrt(jnp.mean(x * x, axis=-1, keepdims=True) + RMS_EPS) * g


def _gelu(x):
    k = -2.0 * GELU_C * LOG2_E
    return x / (1.0 + jnp.exp2(x * (k + (k * GELU_A) * (x * x))))


def _sigmoid(x):
    return 1.0 / (1.0 + jnp.exp(-x))


def _silu(x):
    return x * _sigmoid(x)


def _bdot(a, b):
    return jnp.dot(a, b, preferred_element_type=F32)


def _layer_block(a, l):
    rest = a.shape[1:]
    return pl.BlockSpec((None,) + rest, lambda *_: (l,) + (0,) * len(rest))


def _mod_block(mod, l, j, batch_of):
    D = mod.shape[-1]
    return pl.BlockSpec((None, 1, 1, D), lambda *g: (l, MOD_CHUNKS * batch_of(*g) + j, 0, 0))


def _split_bf16(a):
    hi = a.astype(BF16)
    return hi, (a - hi.astype(F32)).astype(BF16)


def _mod_kernel(c_ref, w_ref, b_ref, o_ref):
    c = c_ref[...]
    c_hi, c_lo = _split_bf16(_silu(c))
    w_hi, w_lo = _split_bf16(w_ref[0])
    o_ref[0] = _bdot(c_hi, w_hi) + (_bdot(c_hi, w_lo) + _bdot(c_lo, w_hi)) + b_ref[0]


def _modulation(c, w_mod, b_mod):
    L, D, D6 = w_mod.shape
    B = c.shape[0]
    return pl.pallas_call(
        _mod_kernel,
        grid=(L, D6 // D),
        in_specs=[
            pl.BlockSpec((B, D), lambda l, j: (0, 0)),
            pl.BlockSpec((1, D, D), lambda l, j: (l, 0, j)),
            pl.BlockSpec((1, 1, D), lambda l, j: (l, 0, j)),
        ],
        out_specs=pl.BlockSpec((1, B, D), lambda l, j: (l, 0, j)),
        out_shape=jax.ShapeDtypeStruct((L, B, D6), F32),
        compiler_params=pltpu.CompilerParams(vmem_limit_bytes=VMEM_LIMIT),
        name="modulation",
    )(c, w_mod, b_mod.reshape(L, 1, D6))


def _shift_rows(cur, prev8, k):
    rolled = pltpu.roll(cur, k, 0)
    fix = pltpu.roll(prev8, k, 0)
    row = lax.broadcasted_iota(jnp.int32, fix.shape, 0)
    top = jnp.where(row < k, fix, rolled[:SUBLANE])
    return jnp.concatenate([top, rolled[SUBLANE:]], axis=0)


def _mix_in_kernel(x_ref, sh_ref, sc_ref, preg_ref, win_ref, qg_ref, wq_ref, kvg_ref, wk_ref, wv_ref,
                   cosq_ref, sinq_ref, tk_ref, lng_ref, lnb_ref, ws_ref, bs_ref,
                   cw_ref, cb_ref, wa_ref, ba_ref, wx_ref, bx_ref, lam_ref,
                   q_ref, k_ref, v_ref, y_ref, prev_ref, hc_ref):
    tm = x_ref.shape[1]

    @pl.when(pl.program_id(1) == 0)
    def _():
        prev_ref[...] = jnp.zeros_like(prev_ref)
        hc_ref[...] = jnp.zeros_like(hc_ref)

    pre_scale = preg_ref[...] * (1.0 + sc_ref[0])
    ws = ws_ref[...]
    r_t = lax.broadcasted_iota(jnp.int32, ws.shape, 0) % SGU_BLOCK
    c_s = lax.broadcasted_iota(jnp.int32, ws.shape, 1)
    ws = jnp.where(c_s <= r_t, ws, 0.0).astype(BF16)
    lam = lam_ref[...]
    softplus_neg_lam = jnp.maximum(-lam, 0.0) + jnp.log(1.0 + jnp.exp(-jnp.abs(lam)))

    n = tm // MIX_SPLIT
    zs = []
    for part in range(MIX_SPLIT):
        x = x_ref[0, part * n:(part + 1) * n, :]
        h = _rms(x, pre_scale) + sh_ref[0]
        zs.append(_bdot(h.astype(BF16), win_ref[...]))
    for part in range(MIX_SPLIT):
        z = zs[part]
        r0 = part * n
        rows = slice(r0, r0 + n)

        nt = (((1,), (1,)), ((), ()))
        qn = _rms(z[:, C_Q:C_Q + Q_LORA], qg_ref[...]).astype(BF16)
        qq = lax.dot_general(wq_ref[...], qn, nt, preferred_element_type=F32)
        cosq = cosq_ref[:, rows]
        sinq = sinq_ref[:, rows]
        for hd in range(MLA_HEADS):
            lo = hd * HEAD_PAD
            ro, po = lo + QK_NOPE, lo + QK_NOPE + QK_ROPE
            q_ref[0, lo:ro, rows] = (qq[lo:ro] * Q_SCALE).astype(BF16)
            q_ref[0, ro:po, rows] = (qq[ro:po] * cosq + qq[po:po + QK_ROPE] * sinq).astype(BF16)
            q_ref[0, po:po + QK_ROPE, rows] = qq[po:po + QK_ROPE].astype(BF16)
        kvn = _rms(z[:, C_KV:C_KV + KV_LORA], kvg_ref[...]).astype(BF16)
        kr = (z[:, C_KR:C_KR + LANE] * tk_ref[rows, :]).astype(BF16)
        k_in = jnp.concatenate([kvn, kr], axis=-1)
        k_ref[0, rows, :] = _bdot(k_in, wk_ref[...]).astype(BF16)
        vv = lax.dot_general(wv_ref[...], kvn, nt, preferred_element_type=F32)
        ones_row = lax.broadcasted_iota(jnp.int32, vv.shape, 0) % V_ROWS == V_DIM
        vv = jnp.where(ones_row, 1.0, vv).astype(BF16)
        piece = min(n, TK)
        for c in range(n // piece):
            g0 = r0 + c * piece
            v_ref[0, g0 // TK, :, g0 % TK:g0 % TK + piece] = vv[:, c * piece:(c + 1) * piece]

        u = _gelu(z[:, C_U:C_U + SGU_WIDTH])
        gv = _gelu(z[:, C_V:C_V + SGU_WIDTH])
        mu = jnp.mean(gv, axis=-1, keepdims=True)
        var = jnp.mean(jnp.square(gv - mu), axis=-1, keepdims=True)
        vn = ((gv - mu) * lax.rsqrt(var + LN_EPS) * lng_ref[...] + lnb_ref[...]).astype(BF16)
        grp = lax.broadcasted_iota(jnp.int32, (SGU_BLOCK, SGU_WIDTH), 1) // SGU_GROUP_DIM
        for blk in range(n // SGU_BLOCK):
            b0 = blk * SGU_BLOCK
            res = _bdot(ws, vn[b0:b0 + SGU_BLOCK])
            mixed = bs_ref[...]
            for g in range(SGU_GROUPS):
                mixed = mixed + jnp.where(grp == g, res[g * SGU_BLOCK:(g + 1) * SGU_BLOCK], 0.0)
            y_ref[0, r0 + b0:r0 + b0 + SGU_BLOCK, 0:SGU_WIDTH] = (u[b0:b0 + SGU_BLOCK] * mixed).astype(BF16)

        zx = z[:, C_X:C_X + LRU_WIDTH]
        prev8 = prev_ref[...]
        cw = cw_ref[...]
        xc = cb_ref[...] + zx * cw[CONV_W - 1:CONV_W]
        for k in range(1, CONV_W):
            xc = xc + _shift_rows(zx, prev8, k) * cw[CONV_W - 1 - k:CONV_W - k]
        prev_ref[...] = zx[n - SUBLANE:]
        xcb = xc.astype(BF16)
        r = _sigmoid(_bdot(xcb, wa_ref[...]) + ba_ref[...])
        ig = _sigmoid(_bdot(xcb, wx_ref[...]) + bx_ref[...])
        log_a = -LRU_C * r * softplus_neg_lam
        a = jnp.exp(log_a)
        t = jnp.tanh(log_a)
        b = jnp.sqrt(-2.0 * t / (1.0 - t)) * (ig * xc)
        row = lax.broadcasted_iota(jnp.int32, a.shape, 0)
        k = 1
        while k < SUBLANE:
            valid = row >= k
            a_sh = pltpu.roll(a, k, 0)
            b_sh = pltpu.roll(b, k, 0)
            b = jnp.where(valid, b + a * b_sh, b)
            a = jnp.where(valid, a * a_sh, a)
            k *= 2
        while k < n:
            b = jnp.concatenate([b[:k], b[k:] + a[k:] * b[:-k]], axis=0)
            a = jnp.concatenate([a[:k], a[k:] * a[:-k]], axis=0)
            k *= 2
        hs = b + a * hc_ref[0:1]
        hc_ref[0:1] = hs[n - 1:n]
        y_ref[0, rows, SGU_WIDTH:SGU_WIDTH + LRU_WIDTH] = (hs * _gelu(z[:, C_Y:C_Y + LRU_WIDTH])).astype(BF16)


def _mix_in(x, mod, l, p, tabs):
    B, S, D = x.shape
    tm = TM_MIX
    batch = lambda b, s: b
    tab = pl.BlockSpec((tm, LANE), lambda b, s: (s, 0))
    tab_t = pl.BlockSpec((QK_ROPE, tm), lambda b, s: (0, s))
    assert tm % TK == 0
    consts = [p[k] for k in ("pre_mix_g", "w_in", "qg", "w_q", "kvg", "w_k", "w_v")]
    consts2 = [p[k] for k in ("ln_g", "ln_b", "w_s", "b_s", "conv_w", "conv_b", "wa", "ba", "wx", "bx", "lam")]
    qw = MLA_HEADS * HEAD_PAD
    return pl.pallas_call(
        _mix_in_kernel,
        grid=(B, S // tm),
        in_specs=[pl.BlockSpec((1, tm, D), lambda b, s: (b, s, 0)), _mod_block(mod, l, MOD_SHIFT_MIX, batch),
                  _mod_block(mod, l, MOD_SCALE_MIX, batch)]
        + [_layer_block(a, l) for a in consts] + [tab_t, tab_t, tab] + [_layer_block(a, l) for a in consts2],
        out_specs=[
            pl.BlockSpec((1, qw, tm), lambda b, s: (b, 0, s)),
            pl.BlockSpec((1, tm, qw), lambda b, s: (b, s, 0)),
            pl.BlockSpec((1, tm // TK, MLA_HEADS * V_ROWS, TK), lambda b, s: (b, s, 0, 0)),
            pl.BlockSpec((1, tm, SGU_WIDTH + LRU_WIDTH), lambda b, s: (b, s, 0)),
        ],
        out_shape=[
            jax.ShapeDtypeStruct((B, qw, S), BF16),
            jax.ShapeDtypeStruct((B, S, qw), BF16),
            jax.ShapeDtypeStruct((B, S // TK, MLA_HEADS * V_ROWS, TK), BF16),
            jax.ShapeDtypeStruct((B, S, SGU_WIDTH + LRU_WIDTH), BF16),
        ],
        scratch_shapes=[pltpu.VMEM((SUBLANE, LRU_WIDTH), F32), pltpu.VMEM((SUBLANE, LRU_WIDTH), F32)],
        compiler_params=pltpu.CompilerParams(
            dimension_semantics=("arbitrary", "arbitrary"), vmem_limit_bytes=VMEM_LIMIT),
        name="mix_in",
    )(x, mod, mod, *consts, tabs["cosq"], tabs["sinq"], tabs["tk"], *consts2)


def _attn_kernel(q_ref, k_ref, v_ref, o_ref, m_ref, acc_ref):
    i = pl.program_id(2)
    heads = q_ref.shape[1] // HEAD_PAD
    feat = lambda hd: slice(hd * HEAD_PAD, (hd + 1) * HEAD_PAD)

    m_ref[...] = jnp.full(m_ref.shape, NEG_INF, F32)
    acc_ref[...] = jnp.zeros(acc_ref.shape, F32)

    def advance(hd, s, v, lanes):
        m = m_ref[hd, :, lanes]
        m_new = jnp.maximum(m, jnp.max(s, axis=0, keepdims=True))
        p = jnp.exp2(s - m_new).astype(BF16)
        acc_ref[hd, 0:V_ROWS, lanes] = jnp.exp2(m - m_new) * acc_ref[hd, 0:V_ROWS, lanes] + _bdot(v, p)
        m_ref[hd, :, lanes] = m_new

    def step(t, key_lo, n_keys, query_lo, masked):
        keys = pl.ds(pl.multiple_of(t * TK + key_lo, n_keys), n_keys)
        lanes = slice(query_lo, TQ)

        def scores(hd):
            s = _bdot(k_ref[0, keys, feat(hd)], q_ref[0, feat(hd), lanes])
            if masked:
                kc = (lax.broadcasted_iota(jnp.int32, s.shape, 0) + key_lo) // CHUNK
                qc = (lax.broadcasted_iota(jnp.int32, s.shape, 1) + query_lo) // CHUNK
                s = jnp.where(kc <= qc, s, NEG_INF)
            return s

        ss = {hd: scores(hd) for hd in range(min(AHEAD, heads))}
        for hd in range(heads):
            if hd + AHEAD < heads:
                ss[hd + AHEAD] = scores(hd + AHEAD)
            advance(hd, ss.pop(hd), v_ref[0, t, hd * V_ROWS:(hd + 1) * V_ROWS, key_lo:key_lo + n_keys], lanes)

    def body(t, c):
        step(t, 0, TK, 0, False)
        return c

    lax.fori_loop(0, i, body, 0)
    step(i, 0, TK // 2, 0, True)
    step(i, TK // 2, TK // 2, TQ // 2, True)
    outs = []
    for hd in range(heads):
        acc = acc_ref[hd]
        outs.append((acc / acc[V_DIM:V_DIM + 1]).T[:, :V_DIM])
    o_ref[0] = jnp.concatenate(outs, axis=-1).astype(BF16)


def _attention(q, k, v):
    B, S, _ = k.shape
    hp = ATTN_HEADS_PER_STEP
    assert TQ == TK
    return pl.pallas_call(
        _attn_kernel,
        grid=(B, MLA_HEADS // hp, S // TQ),
        in_specs=[
            pl.BlockSpec((1, hp * HEAD_PAD, TQ), lambda b, h, i: (b, h, i)),
            pl.BlockSpec((1, S, hp * HEAD_PAD), lambda b, h, i: (b, 0, h)),
            pl.BlockSpec((1, S // TK, hp * V_ROWS, TK), lambda b, h, i: (b, 0, h, 0)),
        ],
        out_specs=pl.BlockSpec((1, TQ, hp * V_DIM), lambda b, h, i: (b, i, h)),
        out_shape=jax.ShapeDtypeStruct((B, S, MLA_HEADS * V_DIM), BF16),
        scratch_shapes=[
            pltpu.VMEM((hp, 1, TQ), F32),
            pltpu.VMEM((hp, HEAD_PAD, TQ), F32),
        ],
        compiler_params=pltpu.CompilerParams(
            dimension_semantics=("arbitrary", "arbitrary", "arbitrary"), vmem_limit_bytes=VMEM_LIMIT),
        name="attention",
    )(q, k, v)


def _mix_out_kernel(o_ref, y_ref, x_ref, wo_ref, postg_ref, gm_ref, preg_ref, scf_ref, shf_ref,
                    xo_ref, h_ref):
    no = o_ref.shape[1]
    tm = x_ref.shape[0]
    parts = [slice(r, r + tm // OUT_SPLIT) for r in range(0, tm, tm // OUT_SPLIT)]
    ys = [_bdot(o_ref[rows, :], wo_ref[0:no]) + _bdot(y_ref[rows, :], wo_ref[no:]) for rows in parts]
    post_scale = gm_ref[0] * postg_ref[...]
    pre_scale = preg_ref[...] * (1.0 + scf_ref[0])
    for rows, y in zip(parts, ys):
        x = x_ref[rows, :] + _rms(y, post_scale)
        xo_ref[rows, :] = x
        h_ref[rows, :] = (_rms(x, pre_scale) + shf_ref[0]).astype(BF16)


def _store_token_tiles(ref, x):
    rows, d = x.shape
    assert d == SUBLANE * LANE
    for j in range(SUBLANE):
        ref[pl.ds(j, rows, stride=SUBLANE), :] = x[:, j * LANE:(j + 1) * LANE]


def _load_token_tiles(ref):
    rows = ref.shape[0] // SUBLANE
    return jnp.concatenate([ref[pl.ds(j, rows, stride=SUBLANE), :] for j in range(SUBLANE)], axis=-1)


def _token_copy(src, dst, src_token, dst_token, sem):
    def rows(t):
        start = t * SUBLANE
        return pl.ds(start if isinstance(t, int) else pl.multiple_of(start, SUBLANE), SUBLANE)

    return pltpu.make_async_copy(src.at[rows(src_token)], dst.at[rows(dst_token)], sem)


def _mix_out_moe_kernel(o_ref, y_ref, x_ref, wo_ref, postg_ref, gm_ref, preg_ref, scf_ref, shf_ref,
                        router_ref, xo_ref, h_ref, route_ref):
    no = o_ref.shape[1]
    tm = x_ref.shape[0]
    parts = [slice(r, r + tm // OUT_SPLIT) for r in range(0, tm, tm // OUT_SPLIT)]
    ys = [_bdot(o_ref[rows, :], wo_ref[0:no]) + _bdot(y_ref[rows, :], wo_ref[no:]) for rows in parts]
    post_scale = gm_ref[0] * postg_ref[...]
    pre_scale = preg_ref[...] * (1.0 + scf_ref[0])
    for rows, y in zip(parts, ys):
        x = x_ref[rows, :] + _rms(y, post_scale)
        xo_ref[rows, :] = x
        h = _rms(x, pre_scale) + shf_ref[0]
        _store_token_tiles(h_ref.at[pl.ds(rows.start * SUBLANE, (rows.stop - rows.start) * SUBLANE)], h)
        h_hi, h_lo = _split_bf16(h)
        t = _bdot(h_hi, router_ref[...])
        logits = t[:, :LANE] + t[:, LANE:] + _bdot(h_lo, router_ref[:, :LANE])
        lane = lax.broadcasted_iota(jnp.int32, logits.shape, 1)
        logits = jnp.where(lane < N_EXPERTS, logits, -jnp.inf)
        m1 = jnp.max(logits, axis=-1, keepdims=True)
        i1 = jnp.min(jnp.where(logits == m1, lane, LANE), axis=-1, keepdims=True)
        rest = jnp.where(lane == i1, -jnp.inf, logits)
        m2 = jnp.max(rest, axis=-1, keepdims=True)
        i2 = jnp.min(jnp.where(rest == m2, lane, LANE), axis=-1, keepdims=True)
        e = jnp.exp(m2 - m1)
        g1 = 1.0 / (1.0 + e)
        route_ref[rows, :] = (jnp.where(lane == ROUTE_EXPERT, i1.astype(F32), 0.0)
                              + jnp.where(lane == ROUTE_EXPERT + 1, i2.astype(F32), 0.0)
                              + jnp.where(lane == ROUTE_GATE, g1, 0.0) + jnp.where(lane == ROUTE_GATE + 1, e * g1, 0.0))


def _mix_out(o, y, x, mod, l, p, router, tiles_per_batch):
    N, D = x.shape
    tm = TM_OUT
    row = lambda w: pl.BlockSpec((tm, w), lambda i: (i, 0))
    batch = lambda i: i // tiles_per_batch
    in_specs = [row(o.shape[1]), row(y.shape[1]), row(D), _layer_block(p["w_o"], l),
                _layer_block(p["post_mix_g"], l), _mod_block(mod, l, MOD_GATE_MIX, batch),
                _layer_block(p["pre_ffn_g"], l), _mod_block(mod, l, MOD_SCALE_FFN, batch),
                _mod_block(mod, l, MOD_SHIFT_FFN, batch)]
    out_specs = [row(D), row(D)]
    out_shape = [jax.ShapeDtypeStruct((N, D), F32), jax.ShapeDtypeStruct((N, D), BF16)]
    args = [o, y, x, p["w_o"], p["post_mix_g"], mod, p["pre_ffn_g"], mod, mod]
    body = _mix_out_kernel
    if router is not None:
        in_specs.append(_layer_block(router, l // 2))
        out_specs = [row(D), pl.BlockSpec((tm * SUBLANE, LANE), lambda i: (i, 0)), row(LANE)]
        out_shape = [jax.ShapeDtypeStruct((N, D), F32), jax.ShapeDtypeStruct((N * SUBLANE, LANE), F32),
                     jax.ShapeDtypeStruct((N, LANE), F32)]
        args.append(router)
        body = _mix_out_moe_kernel
    return pl.pallas_call(
        body,
        grid=(N // tm,),
        in_specs=in_specs,
        out_specs=out_specs,
        out_shape=out_shape,
        compiler_params=pltpu.CompilerParams(dimension_semantics=("arbitrary",), vmem_limit_bytes=VMEM_LIMIT),
        name="mix_out",
    )(*args)


def _swiglu(h, wg, wu, wd):
    g = _bdot(h, wg)
    u = _bdot(h, wu)
    return _bdot((_silu(g) * u).astype(BF16), wd)


def _ffn_kernel(h_ref, x_ref, wg_ref, wu_ref, wd_ref, postg_ref, gf_ref, o_ref, acc_ref):
    j = pl.program_id(1)

    @pl.when(j == 0)
    def _():
        acc_ref[...] = jnp.zeros_like(acc_ref)

    acc_ref[...] += _swiglu(h_ref[...], wg_ref[...], wu_ref[...], wd_ref[...])

    @pl.when(j == pl.num_programs(1) - 1)
    def _():
        o_ref[...] = x_ref[...] + _rms(acc_ref[...], gf_ref[0] * postg_ref[...])


def _ffn(h, x, mod, l, p, tiles_per_batch):
    N, D = x.shape
    wg, wu, wd = p["ffn_w_gate"], p["ffn_w_up"], p["ffn_w_down"]
    FF = wg.shape[2]
    tm = TM_FFN
    tf = FF // 2
    lf = l // 2
    return pl.pallas_call(
        _ffn_kernel,
        grid=(N // tm, FF // tf),
        in_specs=[
            pl.BlockSpec((tm, D), lambda i, j: (i, 0)),
            pl.BlockSpec((tm, D), lambda i, j: (i, 0)),
            pl.BlockSpec((None, D, tf), lambda i, j: (lf, 0, j)),
            pl.BlockSpec((None, D, tf), lambda i, j: (lf, 0, j)),
            pl.BlockSpec((None, tf, D), lambda i, j: (lf, j, 0)),
            _layer_block(p["post_ffn_g"], l),
            _mod_block(mod, l, MOD_GATE_FFN, lambda i, j: i // tiles_per_batch),
        ],
        out_specs=pl.BlockSpec((tm, D), lambda i, j: (i, 0)),
        out_shape=jax.ShapeDtypeStruct((N, D), F32),
        scratch_shapes=[pltpu.VMEM((tm, D), F32)],
        compiler_params=pltpu.CompilerParams(
            dimension_semantics=("arbitrary", "arbitrary"), vmem_limit_bytes=VMEM_LIMIT),
        name="ffn_dense",
    )(h, x, wg, wu, wd, p["post_ffn_g"], mod)


def _moe_dispatch_kernel(pad_lo_ref, pad_hi_ref, slot_ref, h_ref, hs_hbm, zero_ref, sem):
    tm = h_ref.shape[0] // SUBLANE

    @pl.when(pl.program_id(0) == 0)
    def _():
        zero_ref[...] = jnp.zeros(zero_ref.shape, zero_ref.dtype)
        chunk = zero_ref.shape[0] // SUBLANE

        def big(s):
            rows = pl.ds(pl.multiple_of(s * SUBLANE, SUBLANE), chunk * SUBLANE)
            return pltpu.make_async_copy(zero_ref, hs_hbm.at[rows], sem)

        for e in range(pad_lo_ref.shape[0]):
            lo, hi = pad_lo_ref[e], pad_hi_ref[e]
            n_big = (hi - lo) // chunk
            mid = lo + n_big * chunk

            def start_big(j, c):
                big(lo + j * chunk).start()
                return c

            def wait_big(j, c):
                big(lo + j * chunk).wait()
                return c

            def start_pad(s, c):
                _token_copy(zero_ref, hs_hbm, 0, s, sem).start()
                return c

            def wait_pad(s, c):
                _token_copy(zero_ref, hs_hbm, 0, s, sem).wait()
                return c

            lax.fori_loop(0, n_big, start_big, 0)
            lax.fori_loop(mid, hi, start_pad, 0)
            lax.fori_loop(0, n_big, wait_big, 0)
            lax.fori_loop(mid, hi, wait_pad, 0)

    def start(r, c):
        for k in range(2):
            _token_copy(h_ref, hs_hbm, r, slot_ref[0, 0, 2 * r + k], sem).start(priority=k)
        return c

    lax.fori_loop(0, tm, start, 0, unroll=4)

    def wait(r, c):
        for k in range(2):
            _token_copy(h_ref, hs_hbm, r, 0, sem).wait()
        return c

    lax.fori_loop(0, tm, wait, 0, unroll=4)


def _moe_dispatch(pad_lo, pad_hi, slots, h_tiles, n_slots):
    N = h_tiles.shape[0] // SUBLANE
    tm = MOE_DMA_BLOCK
    return pl.pallas_call(
        _moe_dispatch_kernel,
        grid_spec=pltpu.PrefetchScalarGridSpec(
            num_scalar_prefetch=2,
            grid=(N // tm,),
            in_specs=[
                pl.BlockSpec((1, 1, 2 * tm), lambda i, lo, hi: (i, 0, 0), memory_space=pltpu.SMEM),
                pl.BlockSpec((tm * SUBLANE, LANE), lambda i, lo, hi: (i, 0)),
            ],
            out_specs=pl.BlockSpec(memory_space=pl.ANY),
            scratch_shapes=[pltpu.VMEM((MOE_ZERO_CHUNK * SUBLANE, LANE), F32), pltpu.SemaphoreType.DMA(())],
        ),
        out_shape=jax.ShapeDtypeStruct((n_slots * SUBLANE, LANE), F32),
        compiler_params=pltpu.CompilerParams(dimension_semantics=("arbitrary",)),
        name="moe_dispatch",
    )(pad_lo, pad_hi, slots.reshape(N // tm, 1, 2 * tm), h_tiles)


def _moe_expert_kernel(texp_ref, nused_ref, hs_ref, wg_ref, wu_ref, wd_ref, ys_ref):
    del texp_ref

    @pl.when(pl.program_id(0) < nused_ref[0])
    def _():
        x = _load_token_tiles(hs_ref).astype(BF16)
        cast = lambda w_ref: w_ref[...].astype(BF16)
        _store_token_tiles(ys_ref, _swiglu(x, cast(wg_ref), cast(wu_ref), cast(wd_ref)))

    @pl.when(pl.program_id(0) >= nused_ref[0])
    def _():
        ys_ref[...] = jnp.zeros(ys_ref.shape, F32)


def _moe_experts(tile_expert, n_used, hs, wg, wu, wd, lm):
    n_slots = hs.shape[0] // SUBLANE
    _, E, D, FF = wg.shape
    t = MOE_TILE
    tile = pl.BlockSpec((t * SUBLANE, LANE), lambda i, te, nu: (i, 0))
    return pl.pallas_call(
        _moe_expert_kernel,
        grid_spec=pltpu.PrefetchScalarGridSpec(
            num_scalar_prefetch=2,
            grid=(n_slots // t,),
            in_specs=[
                tile,
                pl.BlockSpec((None, None, D, FF), lambda i, te, nu: (lm, te[i], 0, 0)),
                pl.BlockSpec((None, None, D, FF), lambda i, te, nu: (lm, te[i], 0, 0)),
                pl.BlockSpec((None, None, FF, D), lambda i, te, nu: (lm, te[i], 0, 0)),
            ],
            out_specs=tile,
        ),
        out_shape=jax.ShapeDtypeStruct(hs.shape, F32),
        compiler_params=pltpu.CompilerParams(dimension_semantics=("arbitrary",), vmem_limit_bytes=VMEM_LIMIT),
        name="moe_experts",
    )(tile_expert, n_used, hs, wg, wu, wd)


def _moe_combine_kernel(slot_ref, next_slot_ref, ys_hbm, route_ref, x_ref, postg_ref, gf_ref, o_ref, buf_ref, sem):
    tm = x_ref.shape[0]
    i = pl.program_id(0)
    cur = lax.rem(i, 2)

    def gather(slots, half):
        def start(r, c):
            for k in range(2):
                _token_copy(ys_hbm, buf_ref.at[half, k], slots[0, 0, 2 * r + k], r, sem.at[half]).start(priority=k)
            return c

        lax.fori_loop(0, tm, start, 0, unroll=4)

    @pl.when(i == 0)
    def _():
        gather(slot_ref, 0)

    @pl.when(i + 1 < pl.num_programs(0))
    def _():
        gather(next_slot_ref, 1 - cur)

    def wait(r, c):
        for k in range(2):
            _token_copy(ys_hbm, buf_ref.at[cur, k], 0, r, sem.at[cur]).wait()
        return c

    lax.fori_loop(0, tm, wait, 0, unroll=4)
    route = route_ref[...]
    lane = lax.broadcasted_iota(jnp.int32, route.shape, 1)
    f = None
    for k in range(2):
        gate = jnp.sum(jnp.where(lane == ROUTE_GATE + k, route, 0.0), axis=-1, keepdims=True)
        y = _load_token_tiles(buf_ref.at[cur, k])
        f = gate * y if f is None else f + gate * y
    o_ref[...] = x_ref[...] + _rms(f, gf_ref[0] * postg_ref[...])


def _moe_combine(slots, ys, route, x, mod, l, p, tiles_per_batch):
    N, D = x.shape
    tm = TM_FFN
    n = N // tm
    slots = slots.reshape(n, 1, 2 * tm)
    return pl.pallas_call(
        _moe_combine_kernel,
        grid=(n,),
        in_specs=[
            pl.BlockSpec((1, 1, 2 * tm), lambda i: (i, 0, 0), memory_space=pltpu.SMEM),
            pl.BlockSpec((1, 1, 2 * tm), lambda i: (jnp.minimum(i + 1, n - 1), 0, 0), memory_space=pltpu.SMEM),
            pl.BlockSpec(memory_space=pl.ANY),
            pl.BlockSpec((tm, LANE), lambda i: (i, 0)),
            pl.BlockSpec((tm, D), lambda i: (i, 0)),
            _layer_block(p["post_ffn_g"], l),
            _mod_block(mod, l, MOD_GATE_FFN, lambda i: i // tiles_per_batch),
        ],
        out_specs=pl.BlockSpec((tm, D), lambda i: (i, 0)),
        out_shape=jax.ShapeDtypeStruct((N, D), F32),
        scratch_shapes=[pltpu.VMEM((2, 2, tm * SUBLANE, LANE), F32), pltpu.SemaphoreType.DMA((2,))],
        compiler_params=pltpu.CompilerParams(dimension_semantics=("arbitrary",), vmem_limit_bytes=VMEM_LIMIT),
        name="moe_combine",
    )(slots, slots, ys, route, x, p["post_ffn_g"], mod)


def _moe_plan(route, n_slots):
    t = MOE_TILE
    pair_expert = route[:, ROUTE_EXPERT:ROUTE_EXPERT + 2].astype(jnp.int32).reshape(-1)
    onehot = (pair_expert[None, :] == jnp.arange(N_EXPERTS, dtype=jnp.int32)[:, None]).astype(jnp.int32)
    csum = jnp.cumsum(onehot, axis=1)
    count = csum[:, -1]
    padded = (count + t - 1) // t * t
    ends = jnp.cumsum(padded)
    starts = ends - padded
    slot = jnp.sum(onehot * (csum - 1 + starts[:, None]), axis=0)
    n_used = ends[-1:] // t
    tile_start = jnp.arange(n_slots // t, dtype=jnp.int32) * t
    tile_expert = jnp.sum((tile_start[:, None] >= ends[None, :]).astype(jnp.int32), axis=1)
    last_used = tile_expert[jnp.maximum(n_used[0] - 1, 0)]
    tile_expert = jnp.where(tile_start < ends[-1], tile_expert, last_used)
    pad_lo = jnp.concatenate([starts + count, ends[-1:]]).astype(jnp.int32)
    pad_hi = jnp.concatenate([ends, jnp.full((1,), n_slots, ends.dtype)]).astype(jnp.int32)
    return slot, pad_lo, pad_hi, tile_expert, n_used.astype(jnp.int32)


def _moe(h_tiles, x, route, mod, l, p, tiles_per_batch):
    N = x.shape[0]
    n_slots = 2 * N + N_EXPERTS * MOE_TILE
    slot, pad_lo, pad_hi, tile_expert, n_used = _moe_plan(route, n_slots)
    hs = _moe_dispatch(pad_lo, pad_hi, slot, h_tiles, n_slots)
    ys = _moe_experts(tile_expert, n_used, hs, p["moe_w_gate"], p["moe_w_up"], p["moe_w_down"], l // 2)
    return _moe_combine(slot, ys, route, x, mod, l, p, tiles_per_batch)


def _rope_partner(w):
    half = QK_ROPE // 2
    return jnp.concatenate([-w[..., half:], w[..., :half]], axis=-1)


def _rope_tables(S):
    pos = jnp.arange(S, dtype=F32)
    inv = ROPE_THETA ** (-jnp.arange(0, QK_ROPE, 2, dtype=F32) / QK_ROPE)
    ang = pos[:, None] * inv[None, :]
    cos, sin = jnp.cos(ang), jnp.sin(ang)
    cos2 = jnp.concatenate([cos, cos], axis=-1)
    sin2 = jnp.concatenate([sin, sin], axis=-1)
    tk = jnp.concatenate([cos2, sin2, jnp.zeros((S, LANE - 2 * QK_ROPE), F32)], axis=-1)
    return {"cosq": cos2.T * Q_SCALE, "sinq": sin2.T * Q_SCALE, "tk": tk}


def _prepare_params(w_in, q_norm_g, w_uq, kv_norm_g, w_ukv, sgu_ln_g, sgu_ln_b, sgu_w, sgu_b, conv_w, conv_b,
                    lru_wa, lru_ba, lru_wx, lru_bx, lru_lambda, pre_mix_g, post_mix_g, w_o, pre_ffn_g, post_ffn_g,
                    ffn_w_gate, ffn_w_up, ffn_w_down, moe_router, moe_w_gate, moe_w_up, moe_w_down):
    L, D, _ = w_in.shape
    o_q, o_kv = 0, Q_LORA
    o_kr = o_kv + KV_LORA
    o_u = o_kr + QK_ROPE
    o_v = o_u + SGU_WIDTH
    o_x = o_v + SGU_WIDTH
    o_y = o_x + LRU_WIDTH
    w_kr = w_in[..., o_kr:o_kr + QK_ROPE]
    w_in_p = jnp.concatenate([
        w_in[..., o_q:o_q + Q_LORA], w_in[..., o_kv:o_kv + KV_LORA], w_in[..., o_u:o_u + SGU_WIDTH],
        w_in[..., o_v:o_v + SGU_WIDTH], w_in[..., o_x:o_x + LRU_WIDTH], w_in[..., o_y:o_y + LRU_WIDTH],
        w_kr, _rope_partner(w_kr), jnp.zeros((L, D, LANE - 2 * QK_ROPE), F32)], axis=-1).astype(BF16)

    H = MLA_HEADS
    pad = HEAD_PAD - QK_NOPE - QK_ROPE
    wq = w_uq.reshape(L, Q_LORA, H, QK_NOPE + QK_ROPE)
    assert pad == QK_ROPE
    w_q = jnp.concatenate([wq, _rope_partner(wq[..., QK_NOPE:])], axis=-1).reshape(L, Q_LORA, H * HEAD_PAD)
    w_q = jnp.swapaxes(w_q, 1, 2).astype(BF16)

    wkv = w_ukv.reshape(L, KV_LORA, H, QK_NOPE + V_DIM)
    wk_lat = jnp.concatenate([wkv[..., :QK_NOPE], jnp.zeros((L, KV_LORA, H, HEAD_PAD - QK_NOPE), F32)],
                             axis=-1).reshape(L, KV_LORA, H * HEAD_PAD)
    eye = jnp.eye(QK_ROPE, dtype=F32)
    place = jnp.concatenate([jnp.zeros((QK_ROPE, QK_NOPE), F32), eye, jnp.zeros((QK_ROPE, pad), F32)], axis=-1)
    place = jnp.tile(place, (2, H))
    tail = jnp.concatenate([place, jnp.zeros((LANE - 2 * QK_ROPE, H * HEAD_PAD), F32)], axis=0)
    w_k = jnp.concatenate([wk_lat, jnp.broadcast_to(tail, (L,) + tail.shape)], axis=1).astype(BF16)
    w_v = jnp.swapaxes(
        jnp.concatenate([wkv[..., QK_NOPE:], jnp.zeros((L, KV_LORA, H, V_ROWS - V_DIM), F32)],
                        axis=-1).reshape(L, KV_LORA, H * V_ROWS), 1, 2).astype(BF16)

    def block_diag(w):
        sel = jnp.eye(LRU_HEADS, dtype=F32)
        return jnp.einsum("lhij,hg->lhigj", w, sel).reshape(L, LRU_WIDTH, LRU_WIDTH).astype(BF16)

    row = lambda a: a.reshape(a.shape[0], 1, -1)
    r = jnp.pad(moe_router, ((0, 0), (0, 0), (0, LANE - N_EXPERTS)))
    r_hi = r.astype(BF16)
    return {
        "pre_mix_g": row(pre_mix_g), "w_in": w_in_p, "qg": row(q_norm_g), "w_q": w_q, "kvg": row(kv_norm_g),
        "w_k": w_k, "w_v": w_v, "ln_g": row(sgu_ln_g), "ln_b": row(sgu_ln_b),
        "w_s": sgu_w.reshape(L, SGU_GROUPS * SGU_BLOCK, SGU_BLOCK),
        "b_s": jnp.repeat(jnp.swapaxes(sgu_b, 1, 2), SGU_GROUP_DIM, axis=2),
        "conv_w": conv_w, "conv_b": row(conv_b), "wa": block_diag(lru_wa), "ba": row(lru_ba),
        "wx": block_diag(lru_wx), "bx": row(lru_bx), "lam": row(lru_lambda),
        "w_o": w_o.astype(BF16), "post_mix_g": row(post_mix_g), "pre_ffn_g": row(pre_ffn_g),
        "post_ffn_g": row(post_ffn_g),
        "ffn_w_gate": ffn_w_gate.astype(BF16), "ffn_w_up": ffn_w_up.astype(BF16),
        "ffn_w_down": ffn_w_down.astype(BF16),
        "router": jnp.concatenate([r_hi, (r - r_hi.astype(F32)).astype(BF16)], axis=-1),
        "moe_w_gate": moe_w_gate, "moe_w_up": moe_w_up, "moe_w_down": moe_w_down,
    }


def kernel(x, c, w_mod, b_mod, pre_mix_g, post_mix_g, w_in, q_norm_g, w_uq, kv_norm_g, w_ukv, sgu_ln_g,
           sgu_ln_b, sgu_w, sgu_b, conv_w, conv_b, lru_wa, lru_ba, lru_wx, lru_bx, lru_lambda, w_o,
           pre_ffn_g, post_ffn_g, ffn_w_gate, ffn_w_up, ffn_w_down, moe_router, moe_w_gate, moe_w_up,
           moe_w_down):
    B, S, D = x.shape
    L = w_mod.shape[0]
    N = B * S
    tabs = _rope_tables(S)
    p = _prepare_params(w_in, q_norm_g, w_uq, kv_norm_g, w_ukv, sgu_ln_g, sgu_ln_b, sgu_w, sgu_b, conv_w, conv_b,
                        lru_wa, lru_ba, lru_wx, lru_bx, lru_lambda, pre_mix_g, post_mix_g, w_o, pre_ffn_g,
                        post_ffn_g, ffn_w_gate, ffn_w_up, ffn_w_down, moe_router, moe_w_gate, moe_w_up, moe_w_down)
    mod = _modulation(c, w_mod, b_mod).reshape(L, B * MOD_CHUNKS, 1, D)
    xf = x.reshape(N, D)
    for l in range(L):
        q, k, v, ymix = _mix_in(xf.reshape(B, S, D), mod, l, p, tabs)
        o = _attention(q, k, v)
        moe = l % 2 == 1
        res = _mix_out(o.reshape(N, -1), ymix.reshape(N, -1), xf, mod, l, p, p["router"] if moe else None,
                       S // TM_OUT)
        if moe:
            xf, h_tiles, route = res
            xf = _moe(h_tiles, xf, route, mod, l, p, S // TM_FFN)
        else:
            xf, h2 = res
            xf = _ffn(h2, xf, mod, l, p, S // TM_FFN)
    return xf.reshape(B, S, D)
```

```python
import jax
import jax.numpy as jnp
from jax import lax
from jax.experimental import pallas as pl
from jax.experimental.pallas import tpu as pltpu

F32 = jnp.float32
BF16 = jnp.bfloat16

MLA_HEADS = 8
QK_NOPE = 64
QK_ROPE = 32
V_DIM = 64
Q_LORA = 256
KV_LORA = 128
ROPE_THETA = 10000.0
CHUNK = 64
SGU_GROUPS = 4
SGU_GROUP_DIM = 64
SGU_WIDTH = SGU_GROUPS * SGU_GROUP_DIM
SGU_BLOCK = 128
LRU_HEADS = 4
LRU_HEAD_DIM = 64
LRU_WIDTH = LRU_HEADS * LRU_HEAD_DIM
CONV_W = 4
LRU_C = 8.0
N_EXPERTS = 8
NEG_INF = -1e30
RMS_EPS = 1e-6
LN_EPS = 1e-5
LOG2_E = 1.4426950408889634
GELU_C = 0.7978845608028654
GELU_A = 0.044715
MOD_CHUNKS = 6
MOD_SHIFT_MIX, MOD_SCALE_MIX, MOD_GATE_MIX, MOD_SHIFT_FFN, MOD_SCALE_FFN, MOD_GATE_FFN = range(MOD_CHUNKS)
ROUTE_EXPERT = 0
ROUTE_GATE = 2

LANE = 128
SUBLANE = 8
HEAD_PAD = LANE
V_ROWS = 80
Q_SCALE = (QK_NOPE + QK_ROPE) ** -0.5 * LOG2_E

C_Q = 0
C_KV = C_Q + Q_LORA
C_U = C_KV + KV_LORA
C_V = C_U + SGU_WIDTH
C_X = C_V + SGU_WIDTH
C_Y = C_X + LRU_WIDTH
C_KR = C_Y + LRU_WIDTH
D_IN_PAD = C_KR + LANE

VMEM_LIMIT = 56 * 1024 * 1024

TM_MIX = 1024
MIX_SPLIT = 4
TQ = 512
TK = 512
ATTN_HEADS_PER_STEP = 8
AHEAD = 3
TM_OUT = 1024
OUT_SPLIT = 4
TM_FFN = 512
MOE_TILE = 512
MOE_DMA_BLOCK = 2048
MOE_ZERO_CHUNK = 64


def _rms(x, g):
    return x * lax.rsqrt(jnp.mean(x * x, axis=-1, keepdims=True) + RMS_EPS) * g


def _gelu(x):
    k = -2.0 * GELU_C * LOG2_E
    return x / (1.0 + jnp.exp2(x * (k + (k * GELU_A) * (x * x))))


def _sigmoid(x):
    return 1.0 / (1.0 + jnp.exp(-x))


def _silu(x):
    return x * _sigmoid(x)


def _bdot(a, b):
    return jnp.dot(a, b, preferred_element_type=F32)


def _layer_block(a, l):
    rest = a.shape[1:]
    return pl.BlockSpec((None,) + rest, lambda *_: (l,) + (0,) * len(rest))


def _mod_block(mod, l, j, batch_of):
    D = mod.shape[-1]
    return pl.BlockSpec((None, 1, 1, D), lambda *g: (l, MOD_CHUNKS * batch_of(*g) + j, 0, 0))


def _split_bf16(a):
    hi = a.astype(BF16)
    return hi, (a - hi.astype(F32)).astype(BF16)


def _mod_kernel(c_ref, w_ref, b_ref, o_ref):
    c = c_ref[...]
    c_hi, c_lo = _split_bf16(_silu(c))
    w_hi, w_lo = _split_bf16(w_ref[0])
    o_ref[0] = _bdot(c_hi, w_hi) + (_bdot(c_hi, w_lo) + _bdot(c_lo, w_hi)) + b_ref[0]


def _modulation(c, w_mod, b_mod):
    L, D, D6 = w_mod.shape
    B = c.shape[0]
    return pl.pallas_call(
        _mod_kernel,
        grid=(L, D6 // D),
        in_specs=[
            pl.BlockSpec((B, D), lambda l, j: (0, 0)),
            pl.BlockSpec((1, D, D), lambda l, j: (l, 0, j)),
            pl.BlockSpec((1, 1, D), lambda l, j: (l, 0, j)),
        ],
        out_specs=pl.BlockSpec((1, B, D), lambda l, j: (l, 0, j)),
        out_shape=jax.ShapeDtypeStruct((L, B, D6), F32),
        compiler_params=pltpu.CompilerParams(vmem_limit_bytes=VMEM_LIMIT),
        name="modulation",
    )(c, w_mod, b_mod.reshape(L, 1, D6))


def _shift_rows(cur, prev8, k):
    rolled = pltpu.roll(cur, k, 0)
    fix = pltpu.roll(prev8, k, 0)
    row = lax.broadcasted_iota(jnp.int32, fix.shape, 0)
    top = jnp.where(row < k, fix, rolled[:SUBLANE])
    return jnp.concatenate([top, rolled[SUBLANE:]], axis=0)


def _mix_in_kernel(x_ref, sh_ref, sc_ref, preg_ref, win_ref, qg_ref, wq_ref, kvg_ref, wk_ref, wv_ref,
                   cosq_ref, sinq_ref, tk_ref, lng_ref, lnb_ref, ws_ref, bs_ref,
                   cw_ref, cb_ref, wa_ref, ba_ref, wx_ref, bx_ref, lam_ref,
                   q_ref, k_ref, v_ref, y_ref, prev_ref, hc_ref):
    tm = x_ref.shape[1]

    @pl.when(pl.program_id(1) == 0)
    def _():
        prev_ref[...] = jnp.zeros_like(prev_ref)
        hc_ref[...] = jnp.zeros_like(hc_ref)

    pre_scale = preg_ref[...] * (1.0 + sc_ref[0])
    ws = ws_ref[...]
    r_t = lax.broadcasted_iota(jnp.int32, ws.shape, 0) % SGU_BLOCK
    c_s = lax.broadcasted_iota(jnp.int32, ws.shape, 1)
    ws = jnp.where(c_s <= r_t, ws, 0.0).astype(BF16)
    lam = lam_ref[...]
    softplus_neg_lam = jnp.maximum(-lam, 0.0) + jnp.log(1.0 + jnp.exp(-jnp.abs(lam)))

    n = tm // MIX_SPLIT
    zs = []
    for part in range(MIX_SPLIT):
        x = x_ref[0, part * n:(part + 1) * n, :]
        h = _rms(x, pre_scale) + sh_ref[0]
        zs.append(_bdot(h.astype(BF16), win_ref[...]))
    for part in range(MIX_SPLIT):
        z = zs[part]
        r0 = part * n
        rows = slice(r0, r0 + n)

        nt = (((1,), (1,)), ((), ()))
        qn = _rms(z[:, C_Q:C_Q + Q_LORA], qg_ref[...]).astype(BF16)
        qq = lax.dot_general(wq_ref[...], qn, nt, preferred_element_type=F32)
        cosq = cosq_ref[:, rows]
        sinq = sinq_ref[:, rows]
        for hd in range(MLA_HEADS):
            lo = hd * HEAD_PAD
            ro, po = lo + QK_NOPE, lo + QK_NOPE + QK_ROPE
            q_ref[0, lo:ro, rows] = (qq[lo:ro] * Q_SCALE).astype(BF16)
            q_ref[0, ro:po, rows] = (qq[ro:po] * cosq + qq[po:po + QK_ROPE] * sinq).astype(BF16)
            q_ref[0, po:po + QK_ROPE, rows] = qq[po:po + QK_ROPE].astype(BF16)
        kvn = _rms(z[:, C_KV:C_KV + KV_LORA], kvg_ref[...]).astype(BF16)
        kr = (z[:, C_KR:C_KR + LANE] * tk_ref[rows, :]).astype(BF16)
        k_in = jnp.concatenate([kvn, kr], axis=-1)
        k_ref[0, rows, :] = _bdot(k_in, wk_ref[...]).astype(BF16)
        vv = lax.dot_general(wv_ref[...], kvn, nt, preferred_element_type=F32)
        ones_row = lax.broadcasted_iota(jnp.int32, vv.shape, 0) % V_ROWS == V_DIM
        vv = jnp.where(ones_row, 1.0, vv).astype(BF16)
        piece = min(n, TK)
        for c in range(n // piece):
            g0 = r0 + c * piece
            v_ref[0, g0 // TK, :, g0 % TK:g0 % TK + piece] = vv[:, c * piece:(c + 1) * piece]

        u = _gelu(z[:, C_U:C_U + SGU_WIDTH])
        gv = _gelu(z[:, C_V:C_V + SGU_WIDTH])
        mu = jnp.mean(gv, axis=-1, keepdims=True)
        var = jnp.mean(jnp.square(gv - mu), axis=-1, keepdims=True)
        vn = ((gv - mu) * lax.rsqrt(var + LN_EPS) * lng_ref[...] + lnb_ref[...]).astype(BF16)
        grp = lax.broadcasted_iota(jnp.int32, (SGU_BLOCK, SGU_WIDTH), 1) // SGU_GROUP_DIM
        for blk in range(n // SGU_BLOCK):
            b0 = blk * SGU_BLOCK
            res = _bdot(ws, vn[b0:b0 + SGU_BLOCK])
            mixed = bs_ref[...]
            for g in range(SGU_GROUPS):
                mixed = mixed + jnp.where(grp == g, res[g * SGU_BLOCK:(g + 1) * SGU_BLOCK], 0.0)
            y_ref[0, r0 + b0:r0 + b0 + SGU_BLOCK, 0:SGU_WIDTH] = (u[b0:b0 + SGU_BLOCK] * mixed).astype(BF16)

        zx = z[:, C_X:C_X + LRU_WIDTH]
        prev8 = prev_ref[...]
        cw = cw_ref[...]
        xc = cb_ref[...] + zx * cw[CONV_W - 1:CONV_W]
        for k in range(1, CONV_W):
            xc = xc + _shift_rows(zx, prev8, k) * cw[CONV_W - 1 - k:CONV_W - k]
        prev_ref[...] = zx[n - SUBLANE:]
        xcb = xc.astype(BF16)
        r = _sigmoid(_bdot(xcb, wa_ref[...]) + ba_ref[...])
        ig = _sigmoid(_bdot(xcb, wx_ref[...]) + bx_ref[...])
        log_a = -LRU_C * r * softplus_neg_lam
        a = jnp.exp(log_a)
        t = jnp.tanh(log_a)
        b = jnp.sqrt(-2.0 * t / (1.0 - t)) * (ig * xc)
        row = lax.broadcasted_iota(jnp.int32, a.shape, 0)
        k = 1
        while k < SUBLANE:
            valid = row >= k
            a_sh = pltpu.roll(a, k, 0)
            b_sh = pltpu.roll(b, k, 0)
            b = jnp.where(valid, b + a * b_sh, b)
            a = jnp.where(valid, a * a_sh, a)
            k *= 2
        while k < n:
            b = jnp.concatenate([b[:k], b[k:] + a[k:] * b[:-k]], axis=0)
            a = jnp.concatenate([a[:k], a[k:] * a[:-k]], axis=0)
            k *= 2
        hs = b + a * hc_ref[0:1]
        hc_ref[0:1] = hs[n - 1:n]
        y_ref[0, rows, SGU_WIDTH:SGU_WIDTH + LRU_WIDTH] = (hs * _gelu(z[:, C_Y:C_Y + LRU_WIDTH])).astype(BF16)


def _mix_in(x, mod, l, p, tabs):
    B, S, D = x.shape
    tm = TM_MIX
    batch = lambda b, s: b
    tab = pl.BlockSpec((tm, LANE), lambda b, s: (s, 0))
    tab_t = pl.BlockSpec((QK_ROPE, tm), lambda b, s: (0, s))
    assert tm % TK == 0
    consts = [p[k] for k in ("pre_mix_g", "w_in", "qg", "w_q", "kvg", "w_k", "w_v")]
    consts2 = [p[k] for k in ("ln_g", "ln_b", "w_s", "b_s", "conv_w", "conv_b", "wa", "ba", "wx", "bx", "lam")]
    qw = MLA_HEADS * HEAD_PAD
    return pl.pallas_call(
        _mix_in_kernel,
        grid=(B, S // tm),
        in_specs=[pl.BlockSpec((1, tm, D), lambda b, s: (b, s, 0)), _mod_block(mod, l, MOD_SHIFT_MIX, batch),
                  _mod_block(mod, l, MOD_SCALE_MIX, batch)]
        + [_layer_block(a, l) for a in consts] + [tab_t, tab_t, tab] + [_layer_block(a, l) for a in consts2],
        out_specs=[
            pl.BlockSpec((1, qw, tm), lambda b, s: (b, 0, s)),
            pl.BlockSpec((1, tm, qw), lambda b, s: (b, s, 0)),
            pl.BlockSpec((1, tm // TK, MLA_HEADS * V_ROWS, TK), lambda b, s: (b, s, 0, 0)),
            pl.BlockSpec((1, tm, SGU_WIDTH + LRU_WIDTH), lambda b, s: (b, s, 0)),
        ],
        out_shape=[
            jax.ShapeDtypeStruct((B, qw, S), BF16),
            jax.ShapeDtypeStruct((B, S, qw), BF16),
            jax.ShapeDtypeStruct((B, S // TK, MLA_HEADS * V_ROWS, TK), BF16),
            jax.ShapeDtypeStruct((B, S, SGU_WIDTH + LRU_WIDTH), BF16),
        ],
        scratch_shapes=[pltpu.VMEM((SUBLANE, LRU_WIDTH), F32), pltpu.VMEM((SUBLANE, LRU_WIDTH), F32)],
        compiler_params=pltpu.CompilerParams(
            dimension_semantics=("arbitrary", "arbitrary"), vmem_limit_bytes=VMEM_LIMIT),
        name="mix_in",
    )(x, mod, mod, *consts, tabs["cosq"], tabs["sinq"], tabs["tk"], *consts2)


def _attn_kernel(q_ref, k_ref, v_ref, o_ref, m_ref, acc_ref):
    i = pl.program_id(2)
    heads = q_ref.shape[1] // HEAD_PAD
    feat = lambda hd: slice(hd * HEAD_PAD, (hd + 1) * HEAD_PAD)

    m_ref[...] = jnp.full(m_ref.shape, NEG_INF, F32)
    acc_ref[...] = jnp.zeros(acc_ref.shape, F32)

    def advance(hd, s, v, lanes):
        m = m_ref[hd, :, lanes]
        m_new = jnp.maximum(m, jnp.max(s, axis=0, keepdims=True))
        p = jnp.exp2(s - m_new).astype(BF16)
        acc_ref[hd, 0:V_ROWS, lanes] = jnp.exp2(m - m_new) * acc_ref[hd, 0:V_ROWS, lanes] + _bdot(v, p)
        m_ref[hd, :, lanes] = m_new

    def step(t, key_lo, n_keys, query_lo, masked):
        keys = pl.ds(pl.multiple_of(t * TK + key_lo, n_keys), n_keys)
        lanes = slice(query_lo, TQ)

        def scores(hd):
            s = _bdot(k_ref[0, keys, feat(hd)], q_ref[0, feat(hd), lanes])
            if masked:
                kc = (lax.broadcasted_iota(jnp.int32, s.shape, 0) + key_lo) // CHUNK
                qc = (lax.broadcasted_iota(jnp.int32, s.shape, 1) + query_lo) // CHUNK
                s = jnp.where(kc <= qc, s, NEG_INF)
            return s

        ss = {hd: scores(hd) for hd in range(min(AHEAD, heads))}
        for hd in range(heads):
            if hd + AHEAD < heads:
                ss[hd + AHEAD] = scores(hd + AHEAD)
            advance(hd, ss.pop(hd), v_ref[0, t, hd * V_ROWS:(hd + 1) * V_ROWS, key_lo:key_lo + n_keys], lanes)

    def body(t, c):
        step(t, 0, TK, 0, False)
        return c

    lax.fori_loop(0, i, body, 0)
    step(i, 0, TK // 2, 0, True)
    step(i, TK // 2, TK // 2, TQ // 2, True)
    outs = []
    for hd in range(heads):
        acc = acc_ref[hd]
        outs.append((acc / acc[V_DIM:V_DIM + 1]).T[:, :V_DIM])
    o_ref[0] = jnp.concatenate(outs, axis=-1).astype(BF16)


def _attention(q, k, v):
    B, S, _ = k.shape
    hp = ATTN_HEADS_PER_STEP
    assert TQ == TK
    return pl.pallas_call(
        _attn_kernel,
        grid=(B, MLA_HEADS // hp, S // TQ),
        in_specs=[
            pl.BlockSpec((1, hp * HEAD_PAD, TQ), lambda b, h, i: (b, h, i)),
            pl.BlockSpec((1, S, hp * HEAD_PAD), lambda b, h, i: (b, 0, h)),
            pl.BlockSpec((1, S // TK, hp * V_ROWS, TK), lambda b, h, i: (b, 0, h, 0)),
        ],
        out_specs=pl.BlockSpec((1, TQ, hp * V_DIM), lambda b, h, i: (b, i, h)),
        out_shape=jax.ShapeDtypeStruct((B, S, MLA_HEADS * V_DIM), BF16),
        scratch_shapes=[
            pltpu.VMEM((hp, 1, TQ), F32),
            pltpu.VMEM((hp, HEAD_PAD, TQ), F32),
        ],
        compiler_params=pltpu.CompilerParams(
            dimension_semantics=("arbitrary", "arbitrary", "arbitrary"), vmem_limit_bytes=VMEM_LIMIT),
        name="attention",
    )(q, k, v)


def _mix_out_kernel(o_ref, y_ref, x_ref, wo_ref, postg_ref, gm_ref, preg_ref, scf_ref, shf_ref,
                    xo_ref, h_ref):
    no = o_ref.shape[1]
    tm = x_ref.shape[0]
    parts = [slice(r, r + tm // OUT_SPLIT) for r in range(0, tm, tm // OUT_SPLIT)]
    ys = [_bdot(o_ref[rows, :], wo_ref[0:no]) + _bdot(y_ref[rows, :], wo_ref[no:]) for rows in parts]
    post_scale = gm_ref[0] * postg_ref[...]
    pre_scale = preg_ref[...] * (1.0 + scf_ref[0])
    for rows, y in zip(parts, ys):
        x = x_ref[rows, :] + _rms(y, post_scale)
        xo_ref[rows, :] = x
        h_ref[rows, :] = (_rms(x, pre_scale) + shf_ref[0]).astype(BF16)


def _store_token_tiles(ref, x):
    rows, d = x.shape
    assert d == SUBLANE * LANE
    for j in range(SUBLANE):
        ref[pl.ds(j, rows, stride=SUBLANE), :] = x[:, j * LANE:(j + 1) * LANE]


def _load_token_tiles(ref):
    rows = ref.shape[0] // SUBLANE
    return jnp.concatenate([ref[pl.ds(j, rows, stride=SUBLANE), :] for j in range(SUBLANE)], axis=-1)


def _token_copy(src, dst, src_token, dst_token, sem):
    def rows(t):
        start = t * SUBLANE
        return pl.ds(start if isinstance(t, int) else pl.multiple_of(start, SUBLANE), SUBLANE)

    return pltpu.make_async_copy(src.at[rows(src_token)], dst.at[rows(dst_token)], sem)


def _mix_out_moe_kernel(o_ref, y_ref, x_ref, wo_ref, postg_ref, gm_ref, preg_ref, scf_ref, shf_ref,
                        router_ref, xo_ref, h_ref, route_ref):
    no = o_ref.shape[1]
    tm = x_ref.shape[0]
    parts = [slice(r, r + tm // OUT_SPLIT) for r in range(0, tm, tm // OUT_SPLIT)]
    ys = [_bdot(o_ref[rows, :], wo_ref[0:no]) + _bdot(y_ref[rows, :], wo_ref[no:]) for rows in parts]
    post_scale = gm_ref[0] * postg_ref[...]
    pre_scale = preg_ref[...] * (1.0 + scf_ref[0])
    for rows, y in zip(parts, ys):
        x = x_ref[rows, :] + _rms(y, post_scale)
        xo_ref[rows, :] = x
        h = _rms(x, pre_scale) + shf_ref[0]
        _store_token_tiles(h_ref.at[pl.ds(rows.start * SUBLANE, (rows.stop - rows.start) * SUBLANE)], h)
        h_hi, h_lo = _split_bf16(h)
        t = _bdot(h_hi, router_ref[...])
        logits = t[:, :LANE] + t[:, LANE:] + _bdot(h_lo, router_ref[:, :LANE])
        lane = lax.broadcasted_iota(jnp.int32, logits.shape, 1)
        logits = jnp.where(lane < N_EXPERTS, logits, -jnp.inf)
        m1 = jnp.max(logits, axis=-1, keepdims=True)
        i1 = jnp.min(jnp.where(logits == m1, lane, LANE), axis=-1, keepdims=True)
        rest = jnp.where(lane == i1, -jnp.inf, logits)
        m2 = jnp.max(rest, axis=-1, keepdims=True)
        i2 = jnp.min(jnp.where(rest == m2, lane, LANE), axis=-1, keepdims=True)
        e = jnp.exp(m2 - m1)
        g1 = 1.0 / (1.0 + e)
        route_ref[rows, :] = (jnp.where(lane == ROUTE_EXPERT, i1.astype(F32), 0.0)
                              + jnp.where(lane == ROUTE_EXPERT + 1, i2.astype(F32), 0.0)
                              + jnp.where(lane == ROUTE_GATE, g1, 0.0) + jnp.where(lane == ROUTE_GATE + 1, e * g1, 0.0))


def _mix_out(o, y, x, mod, l, p, router, tiles_per_batch):
    N, D = x.shape
    tm = TM_OUT
    row = lambda w: pl.BlockSpec((tm, w), lambda i: (i, 0))
    batch = lambda i: i // tiles_per_batch
    in_specs = [row(o.shape[1]), row(y.shape[1]), row(D), _layer_block(p["w_o"], l),
                _layer_block(p["post_mix_g"], l), _mod_block(mod, l, MOD_GATE_MIX, batch),
                _layer_block(p["pre_ffn_g"], l), _mod_block(mod, l, MOD_SCALE_FFN, batch),
                _mod_block(mod, l, MOD_SHIFT_FFN, batch)]
    out_specs = [row(D), row(D)]
    out_shape = [jax.ShapeDtypeStruct((N, D), F32), jax.ShapeDtypeStruct((N, D), BF16)]
    args = [o, y, x, p["w_o"], p["post_mix_g"], mod, p["pre_ffn_g"], mod, mod]
    body = _mix_out_kernel
    if router is not None:
        in_specs.append(_layer_block(router, l // 2))
        out_specs = [row(D), pl.BlockSpec((tm * SUBLANE, LANE), lambda i: (i, 0)), row(LANE)]
        out_shape = [jax.ShapeDtypeStruct((N, D), F32), jax.ShapeDtypeStruct((N * SUBLANE, LANE), F32),
                     jax.ShapeDtypeStruct((N, LANE), F32)]
        args.append(router)
        body = _mix_out_moe_kernel
    return pl.pallas_call(
        body,
        grid=(N // tm,),
        in_specs=in_specs,
        out_specs=out_specs,
        out_shape=out_shape,
        compiler_params=pltpu.CompilerParams(dimension_semantics=("arbitrary",), vmem_limit_bytes=VMEM_LIMIT),
        name="mix_out",
    )(*args)


def _swiglu(h, wg, wu, wd):
    g = _bdot(h, wg)
    u = _bdot(h, wu)
    return _bdot((_silu(g) * u).astype(BF16), wd)


def _ffn_kernel(h_ref, x_ref, wg_ref, wu_ref, wd_ref, postg_ref, gf_ref, o_ref, acc_ref):
    j = pl.program_id(1)

    @pl.when(j == 0)
    def _():
        acc_ref[...] = jnp.zeros_like(acc_ref)

    acc_ref[...] += _swiglu(h_ref[...], wg_ref[...], wu_ref[...], wd_ref[...])

    @pl.when(j == pl.num_programs(1) - 1)
    def _():
        o_ref[...] = x_ref[...] + _rms(acc_ref[...], gf_ref[0] * postg_ref[...])


def _ffn(h, x, mod, l, p, tiles_per_batch):
    N, D = x.shape
    wg, wu, wd = p["ffn_w_gate"], p["ffn_w_up"], p["ffn_w_down"]
    FF = wg.shape[2]
    tm = TM_FFN
    tf = FF
    lf = l // 2
    once = pl.Buffered(1)
    return pl.pallas_call(
        _ffn_kernel,
        grid=(N // tm, FF // tf),
        in_specs=[
            pl.BlockSpec((tm, D), lambda i, j: (i, 0)),
            pl.BlockSpec((tm, D), lambda i, j: (i, 0)),
            pl.BlockSpec((None, D, tf), lambda i, j: (lf, 0, j), pipeline_mode=once),
            pl.BlockSpec((None, D, tf), lambda i, j: (lf, 0, j), pipeline_mode=once),
            pl.BlockSpec((None, tf, D), lambda i, j: (lf, j, 0), pipeline_mode=once),
            _layer_block(p["post_ffn_g"], l),
            _mod_block(mod, l, MOD_GATE_FFN, lambda i, j: i // tiles_per_batch),
        ],
        out_specs=pl.BlockSpec((tm, D), lambda i, j: (i, 0)),
        out_shape=jax.ShapeDtypeStruct((N, D), F32),
        scratch_shapes=[pltpu.VMEM((tm, D), F32)],
        compiler_params=pltpu.CompilerParams(
            dimension_semantics=("arbitrary", "arbitrary"), vmem_limit_bytes=VMEM_LIMIT),
        name="ffn_dense",
    )(h, x, wg, wu, wd, p["post_ffn_g"], mod)


def _moe_dispatch_kernel(pad_lo_ref, pad_hi_ref, slot_ref, h_ref, hs_hbm, zero_ref, sem):
    tm = h_ref.shape[0] // SUBLANE

    @pl.when(pl.program_id(0) == 0)
    def _():
        zero_ref[...] = jnp.zeros(zero_ref.shape, zero_ref.dtype)
        chunk = zero_ref.shape[0] // SUBLANE

        def big(s):
            rows = pl.ds(pl.multiple_of(s * SUBLANE, SUBLANE), chunk * SUBLANE)
            return pltpu.make_async_copy(zero_ref, hs_hbm.at[rows], sem)

        for e in range(pad_lo_ref.shape[0]):
            lo, hi = pad_lo_ref[e], pad_hi_ref[e]
            n_big = (hi - lo) // chunk
            mid = lo + n_big * chunk

            def start_big(j, c):
                big(lo + j * chunk).start()
                return c

            def wait_big(j, c):
                big(lo + j * chunk).wait()
                return c

            def start_pad(s, c):
                _token_copy(zero_ref, hs_hbm, 0, s, sem).start()
                return c

            def wait_pad(s, c):
                _token_copy(zero_ref, hs_hbm, 0, s, sem).wait()
                return c

            lax.fori_loop(0, n_big, start_big, 0)
            lax.fori_loop(mid, hi, start_pad, 0)
            lax.fori_loop(0, n_big, wait_big, 0)
            lax.fori_loop(mid, hi, wait_pad, 0)

    def start(r, c):
        for k in range(2):
            _token_copy(h_ref, hs_hbm, r, slot_ref[0, 0, 2 * r + k], sem).start(priority=k)
        return c

    lax.fori_loop(0, tm, start, 0, unroll=4)

    def wait(r, c):
        for k in range(2):
            _token_copy(h_ref, hs_hbm, r, 0, sem).wait()
        return c

    lax.fori_loop(0, tm, wait, 0, unroll=4)


def _moe_dispatch(pad_lo, pad_hi, slots, h_tiles, n_slots):
    N = h_tiles.shape[0] // SUBLANE
    tm = MOE_DMA_BLOCK
    return pl.pallas_call(
        _moe_dispatch_kernel,
        grid_spec=pltpu.PrefetchScalarGridSpec(
            num_scalar_prefetch=2,
            grid=(N // tm,),
            in_specs=[
                pl.BlockSpec((1, 1, 2 * tm), lambda i, lo, hi: (i, 0, 0), memory_space=pltpu.SMEM),
                pl.BlockSpec((tm * SUBLANE, LANE), lambda i, lo, hi: (i, 0)),
            ],
            out_specs=pl.BlockSpec(memory_space=pl.ANY),
            scratch_shapes=[pltpu.VMEM((MOE_ZERO_CHUNK * SUBLANE, LANE), F32), pltpu.SemaphoreType.DMA(())],
        ),
        out_shape=jax.ShapeDtypeStruct((n_slots * SUBLANE, LANE), F32),
        compiler_params=pltpu.CompilerParams(dimension_semantics=("arbitrary",)),
        name="moe_dispatch",
    )(pad_lo, pad_hi, slots.reshape(N // tm, 1, 2 * tm), h_tiles)


def _moe_expert_kernel(texp_ref, nused_ref, hs_ref, wg_ref, wu_ref, wd_ref, ys_ref):
    del texp_ref

    @pl.when(pl.program_id(0) < nused_ref[0])
    def _():
        x = _load_token_tiles(hs_ref).astype(BF16)
        cast = lambda w_ref: w_ref[...].astype(BF16)
        _store_token_tiles(ys_ref, _swiglu(x, cast(wg_ref), cast(wu_ref), cast(wd_ref)))

    @pl.when(pl.program_id(0) >= nused_ref[0])
    def _():
        ys_ref[...] = jnp.zeros(ys_ref.shape, F32)


def _moe_experts(tile_expert, n_used, hs, wg, wu, wd, lm):
    n_slots = hs.shape[0] // SUBLANE
    _, E, D, FF = wg.shape
    t = MOE_TILE
    tile = pl.BlockSpec((t * SUBLANE, LANE), lambda i, te, nu: (i, 0))
    return pl.pallas_call(
        _moe_expert_kernel,
        grid_spec=pltpu.PrefetchScalarGridSpec(
            num_scalar_prefetch=2,
            grid=(n_slots // t,),
            in_specs=[
                tile,
                pl.BlockSpec((None, None, D, FF), lambda i, te, nu: (lm, te[i], 0, 0)),
                pl.BlockSpec((None, None, D, FF), lambda i, te, nu: (lm, te[i], 0, 0)),
                pl.BlockSpec((None, None, FF, D), lambda i, te, nu: (lm, te[i], 0, 0)),
            ],
            out_specs=tile,
        ),
        out_shape=jax.ShapeDtypeStruct(hs.shape, F32),
        compiler_params=pltpu.CompilerParams(dimension_semantics=("arbitrary",), vmem_limit_bytes=VMEM_LIMIT),
        name="moe_experts",
    )(tile_expert, n_used, hs, wg, wu, wd)


def _moe_combine_kernel(slot_ref, next_slot_ref, ys_hbm, route_ref, x_ref, postg_ref, gf_ref, o_ref, buf_ref, sem):
    tm = x_ref.shape[0]
    i = pl.program_id(0)
    cur = lax.rem(i, 2)

    def gather(slots, half):
        def start(r, c):
            for k in range(2):
                _token_copy(ys_hbm, buf_ref.at[half, k], slots[0, 0, 2 * r + k], r, sem.at[half]).start(priority=k)
            return c

        lax.fori_loop(0, tm, start, 0, unroll=4)

    @pl.when(i == 0)
    def _():
        gather(slot_ref, 0)

    @pl.when(i + 1 < pl.num_programs(0))
    def _():
        gather(next_slot_ref, 1 - cur)

    def wait(r, c):
        for k in range(2):
            _token_copy(ys_hbm, buf_ref.at[cur, k], 0, r, sem.at[cur]).wait()
        return c

    lax.fori_loop(0, tm, wait, 0, unroll=4)
    route = route_ref[...]
    lane = lax.broadcasted_iota(jnp.int32, route.shape, 1)
    f = None
    for k in range(2):
        gate = jnp.sum(jnp.where(lane == ROUTE_GATE + k, route, 0.0), axis=-1, keepdims=True)
        y = _load_token_tiles(buf_ref.at[cur, k])
        f = gate * y if f is None else f + gate * y
    o_ref[...] = x_ref[...] + _rms(f, gf_ref[0] * postg_ref[...])


def _moe_combine(slots, ys, route, x, mod, l, p, tiles_per_batch):
    N, D = x.shape
    tm = TM_FFN
    n = N // tm
    slots = slots.reshape(n, 1, 2 * tm)
    return pl.pallas_call(
        _moe_combine_kernel,
        grid=(n,),
        in_specs=[
            pl.BlockSpec((1, 1, 2 * tm), lambda i: (i, 0, 0), memory_space=pltpu.SMEM),
            pl.BlockSpec((1, 1, 2 * tm), lambda i: (jnp.minimum(i + 1, n - 1), 0, 0), memory_space=pltpu.SMEM),
            pl.BlockSpec(memory_space=pl.ANY),
            pl.BlockSpec((tm, LANE), lambda i: (i, 0)),
            pl.BlockSpec((tm, D), lambda i: (i, 0)),
            _layer_block(p["post_ffn_g"], l),
            _mod_block(mod, l, MOD_GATE_FFN, lambda i: i // tiles_per_batch),
        ],
        out_specs=pl.BlockSpec((tm, D), lambda i: (i, 0)),
        out_shape=jax.ShapeDtypeStruct((N, D), F32),
        scratch_shapes=[pltpu.VMEM((2, 2, tm * SUBLANE, LANE), F32), pltpu.SemaphoreType.DMA((2,))],
        compiler_params=pltpu.CompilerParams(dimension_semantics=("arbitrary",), vmem_limit_bytes=VMEM_LIMIT),
        name="moe_combine",
    )(slots, slots, ys, route, x, p["post_ffn_g"], mod)


def _moe_plan(route, n_slots):
    t = MOE_TILE
    pair_expert = route[:, ROUTE_EXPERT:ROUTE_EXPERT + 2].astype(jnp.int32).reshape(-1)
    onehot = (pair_expert[None, :] == jnp.arange(N_EXPERTS, dtype=jnp.int32)[:, None]).astype(jnp.int32)
    csum = jnp.cumsum(onehot, axis=1)
    count = csum[:, -1]
    padded = (count + t - 1) // t * t
    ends = jnp.cumsum(padded)
    starts = ends - padded
    slot = jnp.sum(onehot * (csum - 1 + starts[:, None]), axis=0)
    n_used = ends[-1:] // t
    tile_start = jnp.arange(n_slots // t, dtype=jnp.int32) * t
    tile_expert = jnp.sum((tile_start[:, None] >= ends[None, :]).astype(jnp.int32), axis=1)
    last_used = tile_expert[jnp.maximum(n_used[0] - 1, 0)]
    tile_expert = jnp.where(tile_start < ends[-1], tile_expert, last_used)
    pad_lo = jnp.concatenate([starts + count, ends[-1:]]).astype(jnp.int32)
    pad_hi = jnp.concatenate([ends, jnp.full((1,), n_slots, ends.dtype)]).astype(jnp.int32)
    return slot, pad_lo, pad_hi, tile_expert, n_used.astype(jnp.int32)


def _moe(h_tiles, x, route, mod, l, p, tiles_per_batch):
    N = x.shape[0]
    n_slots = 2 * N + N_EXPERTS * MOE_TILE
    slot, pad_lo, pad_hi, tile_expert, n_used = _moe_plan(route, n_slots)
    hs = _moe_dispatch(pad_lo, pad_hi, slot, h_tiles, n_slots)
    ys = _moe_experts(tile_expert, n_used, hs, p["moe_w_gate"], p["moe_w_up"], p["moe_w_down"], l // 2)
    return _moe_combine(slot, ys, route, x, mod, l, p, tiles_per_batch)


def _rope_partner(w):
    half = QK_ROPE // 2
    return jnp.concatenate([-w[..., half:], w[..., :half]], axis=-1)


def _rope_tables(S):
    pos = jnp.arange(S, dtype=F32)
    inv = ROPE_THETA ** (-jnp.arange(0, QK_ROPE, 2, dtype=F32) / QK_ROPE)
    ang = pos[:, None] * inv[None, :]
    cos, sin = jnp.cos(ang), jnp.sin(ang)
    cos2 = jnp.concatenate([cos, cos], axis=-1)
    sin2 = jnp.concatenate([sin, sin], axis=-1)
    tk = jnp.concatenate([cos2, sin2, jnp.zeros((S, LANE - 2 * QK_ROPE), F32)], axis=-1)
    return {"cosq": cos2.T * Q_SCALE, "sinq": sin2.T * Q_SCALE, "tk": tk}


def _prepare_params(w_in, q_norm_g, w_uq, kv_norm_g, w_ukv, sgu_ln_g, sgu_ln_b, sgu_w, sgu_b, conv_w, conv_b,
                    lru_wa, lru_ba, lru_wx, lru_bx, lru_lambda, pre_mix_g, post_mix_g, w_o, pre_ffn_g, post_ffn_g,
                    ffn_w_gate, ffn_w_up, ffn_w_down, moe_router, moe_w_gate, moe_w_up, moe_w_down):
    L, D, _ = w_in.shape
    o_q, o_kv = 0, Q_LORA
    o_kr = o_kv + KV_LORA
    o_u = o_kr + QK_ROPE
    o_v = o_u + SGU_WIDTH
    o_x = o_v + SGU_WIDTH
    o_y = o_x + LRU_WIDTH
    w_kr = w_in[..., o_kr:o_kr + QK_ROPE]
    w_in_p = jnp.concatenate([
        w_in[..., o_q:o_q + Q_LORA], w_in[..., o_kv:o_kv + KV_LORA], w_in[..., o_u:o_u + SGU_WIDTH],
        w_in[..., o_v:o_v + SGU_WIDTH], w_in[..., o_x:o_x + LRU_WIDTH], w_in[..., o_y:o_y + LRU_WIDTH],
        w_kr, _rope_partner(w_kr), jnp.zeros((L, D, LANE - 2 * QK_ROPE), F32)], axis=-1).astype(BF16)

    H = MLA_HEADS
    pad = HEAD_PAD - QK_NOPE - QK_ROPE
    wq = w_uq.reshape(L, Q_LORA, H, QK_NOPE + QK_ROPE)
    assert pad == QK_ROPE
    w_q = jnp.concatenate([wq, _rope_partner(wq[..., QK_NOPE:])], axis=-1).reshape(L, Q_LORA, H * HEAD_PAD)
    w_q = jnp.swapaxes(w_q, 1, 2).astype(BF16)

    wkv = w_ukv.reshape(L, KV_LORA, H, QK_NOPE + V_DIM)
    wk_lat = jnp.concatenate([wkv[..., :QK_NOPE], jnp.zeros((L, KV_LORA, H, HEAD_PAD - QK_NOPE), F32)],
                             axis=-1).reshape(L, KV_LORA, H * HEAD_PAD)
    eye = jnp.eye(QK_ROPE, dtype=F32)
    place = jnp.concatenate([jnp.zeros((QK_ROPE, QK_NOPE), F32), eye, jnp.zeros((QK_ROPE, pad), F32)], axis=-1)
    place = jnp.tile(place, (2, H))
    tail = jnp.concatenate([place, jnp.zeros((LANE - 2 * QK_ROPE, H * HEAD_PAD), F32)], axis=0)
    w_k = jnp.concatenate([wk_lat, jnp.broadcast_to(tail, (L,) + tail.shape)], axis=1).astype(BF16)
    w_v = jnp.swapaxes(
        jnp.concatenate([wkv[..., QK_NOPE:], jnp.zeros((L, KV_LORA, H, V_ROWS - V_DIM), F32)],
                        axis=-1).reshape(L, KV_LORA, H * V_ROWS), 1, 2).astype(BF16)

    def block_diag(w):
        sel = jnp.eye(LRU_HEADS, dtype=F32)
        return jnp.einsum("lhij,hg->lhigj", w, sel).reshape(L, LRU_WIDTH, LRU_WIDTH).astype(BF16)

    row = lambda a: a.reshape(a.shape[0], 1, -1)
    r = jnp.pad(moe_router, ((0, 0), (0, 0), (0, LANE - N_EXPERTS)))
    r_hi = r.astype(BF16)
    return {
        "pre_mix_g": row(pre_mix_g), "w_in": w_in_p, "qg": row(q_norm_g), "w_q": w_q, "kvg": row(kv_norm_g),
        "w_k": w_k, "w_v": w_v, "ln_g": row(sgu_ln_g), "ln_b": row(sgu_ln_b),
        "w_s": sgu_w.reshape(L, SGU_GROUPS * SGU_BLOCK, SGU_BLOCK),
        "b_s": jnp.repeat(jnp.swapaxes(sgu_b, 1, 2), SGU_GROUP_DIM, axis=2),
        "conv_w": conv_w, "conv_b": row(conv_b), "wa": block_diag(lru_wa), "ba": row(lru_ba),
        "wx": block_diag(lru_wx), "bx": row(lru_bx), "lam": row(lru_lambda),
        "w_o": w_o.astype(BF16), "post_mix_g": row(post_mix_g), "pre_ffn_g": row(pre_ffn_g),
        "post_ffn_g": row(post_ffn_g),
        "ffn_w_gate": ffn_w_gate.astype(BF16), "ffn_w_up": ffn_w_up.astype(BF16),
        "ffn_w_down": ffn_w_down.astype(BF16),
        "router": jnp.concatenate([r_hi, (r - r_hi.astype(F32)).astype(BF16)], axis=-1),
        "moe_w_gate": moe_w_gate, "moe_w_up": moe_w_up, "moe_w_down": moe_w_down,
    }


def kernel(x, c, w_mod, b_mod, pre_mix_g, post_mix_g, w_in, q_norm_g, w_uq, kv_norm_g, w_ukv, sgu_ln_g,
           sgu_ln_b, sgu_w, sgu_b, conv_w, conv_b, lru_wa, lru_ba, lru_wx, lru_bx, lru_lambda, w_o,
           pre_ffn_g, post_ffn_g, ffn_w_gate, ffn_w_up, ffn_w_down, moe_router, moe_w_gate, moe_w_up,
           moe_w_down):
    B, S, D = x.shape
    L = w_mod.shape[0]
    N = B * S
    tabs = _rope_tables(S)
    p = _prepare_params(w_in, q_norm_g, w_uq, kv_norm_g, w_ukv, sgu_ln_g, sgu_ln_b, sgu_w, sgu_b, conv_w, conv_b,
                        lru_wa, lru_ba, lru_wx, lru_bx, lru_lambda, pre_mix_g, post_mix_g, w_o, pre_ffn_g,
                        post_ffn_g, ffn_w_gate, ffn_w_up, ffn_w_down, moe_router, moe_w_gate, moe_w_up, moe_w_down)
    mod = _modulation(c, w_mod, b_mod).reshape(L, B * MOD_CHUNKS, 1, D)
    xf = x.reshape(N, D)
    for l in range(L):
        q, k, v, ymix = _mix_in(xf.reshape(B, S, D), mod, l, p, tabs)
        o = _attention(q, k, v)
        moe = l % 2 == 1
        res = _mix_out(o.reshape(N, -1), ymix.reshape(N, -1), xf, mod, l, p, p["router"] if moe else None,
                       S // TM_OUT)
        if moe:
            xf, h_tiles, route = res
            xf = _moe(h_tiles, xf, route, mod, l, p, S // TM_FFN)
        else:
            xf, h2 = res
            xf = _ffn(h2, xf, mod, l, p, S // TM_FFN)
    return xf.reshape(B, S, D)
```
